```python
import math
import jax, jax.numpy as jnp
from jax import lax
import numpy as np

D_MODEL = 1024
BATCH = 4
SEQ = 4096
DEPTH = 4

N_MIXERS = 3
A_GROUPS = ((128, 1), (512, 4), (2048, 16))
A_HEADS = 16
A_HEAD_DIM = D_MODEL // A_HEADS
A_BLOCK = 128
NUM_BUCKETS = 32
MAX_DISTANCE = 2048
B_CHUNK = 128
B_WIDTH = 2 * D_MODEL
B_GROUPS = 16
C_HEADS = 8
C_HEAD_DIM = D_MODEL // C_HEADS
C_CONV = 4
C_CHUNK = 64
MOE_GROUPS = 8
MOE_PER_GROUP = 8
MOE_EXPERTS = MOE_GROUPS * MOE_PER_GROUP
MOE_TOPK = 2
MOE_HIDDEN = D_MODEL // 2
LN_EPS = 1e-5
RMS_EPS = 1e-6
DEEPNORM_ALPHA = (2 * DEPTH) ** 0.25
DEEPNORM_BETA = (8 * DEPTH) ** -0.25
N_A = (DEPTH + 2) // 3
N_B = (DEPTH + 1) // 3
N_C = DEPTH // 3

kernel_name = 'hybrid_dilated_sgu_deltanet_hmoe'


def layer_norm(x, g, b):
    xf = x.astype(jnp.float32)
    mu = jnp.mean(xf, axis=-1, keepdims=True)
    xc = xf - mu
    var = jnp.mean(xc * xc, axis=-1, keepdims=True)
    y = xc * lax.rsqrt(var + LN_EPS) * g.astype(jnp.float32) + b.astype(jnp.float32)
    return y.astype(x.dtype)


def t5_bucket(dist):
    max_exact = NUM_BUCKETS // 2
    d = jnp.maximum(dist, 1).astype(jnp.float32)
    large = max_exact + (jnp.log(d / max_exact) / math.log(MAX_DISTANCE / max_exact)
                         * (NUM_BUCKETS - max_exact)).astype(jnp.int32)
    return jnp.where(dist < max_exact, dist, jnp.minimum(large, NUM_BUCKETS - 1))


def _dilated_group(q, k, v, window, dil, rel_bias):
    B, S, H, Dh = q.shape
    steps = window // dil
    L = S // dil
    nb = -(-L // A_BLOCK)
    Lp = nb * A_BLOCK

    def by_residue(t):
        t = t.reshape(B, L, dil, H, Dh).transpose(0, 2, 3, 1, 4)
        return jnp.pad(t, ((0, 0), (0, 0), (0, 0), (0, Lp - L), (0, 0)))

    qr, kr, vr = by_residue(q), by_residue(k), by_residue(v)
    qb = qr.reshape(B, dil, H, nb, A_BLOCK, Dh)

    def band(t):
        prev = jnp.pad(t, ((0, 0), (0, 0), (0, 0), (A_BLOCK, 0), (0, 0)))[:, :, :, :Lp]
        return jnp.concatenate([prev.reshape(B, dil, H, nb, A_BLOCK, Dh),
                                t.reshape(B, dil, H, nb, A_BLOCK, Dh)], axis=4)

    kb, vb = band(kr), band(vr)
    s = jnp.einsum('brhnqd,brhnkd->brhnqk', qb, kb,
                   preferred_element_type=jnp.float32) * (Dh ** -0.5)
    qi = jnp.arange(A_BLOCK)[:, None]
    ki = jnp.arange(2 * A_BLOCK)[None, :]
    rel = qi + A_BLOCK - ki
    key_pos = jnp.arange(nb)[:, None, None] * A_BLOCK + ki[None] - A_BLOCK
    valid = (rel >= 0) & (rel <= steps) & (key_pos >= 0)
    bias = rel_bias[t5_bucket(jnp.maximum(rel, 0) * dil)].astype(jnp.float32)
    bias = bias.transpose(2, 0, 1)
    s = jnp.where(valid, s + bias[:, None], -jnp.inf)
    m = jnp.max(s, axis=-1, keepdims=True)
    p = jnp.exp(s - m)
    l = jnp.sum(p, axis=-1)
    o = jnp.einsum('brhnqk,brhnkd->brhnqd', p.astype(v.dtype), vb,
                   preferred_element_type=jnp.float32) / l[..., None]
    lse = m[..., 0] + jnp.log(l)
    o = o.reshape(B, dil, H, Lp, Dh)[:, :, :, :L].transpose(0, 3, 1, 2, 4).reshape(B, S, H, Dh)
    lse = lse.reshape(B, dil, H, Lp)[..., :L].transpose(0, 3, 1, 2).reshape(B, S, H)
    return o, lse


def dilated_attention(x, w_in, w_out, rel_bias):
    B, S, _ = x.shape
    G = len(A_GROUPS)
    qkv = (x @ w_in).reshape(B, S, G, 3, A_HEADS, A_HEAD_DIM)
    outs, lses = [], []
    for g, (window, dil) in enumerate(A_GROUPS):
        o, lse = _dilated_group(qkv[:, :, g, 0], qkv[:, :, g, 1], qkv[:, :, g, 2],
                                window, dil, rel_bias)
        outs.append(o)
        lses.append(lse)
    wts = jax.nn.softmax(jnp.stack(lses), axis=0)
    o = jnp.einsum('gbsh,gbshd->bshd', wts, jnp.stack(outs))
    return o.reshape(B, S, A_HEADS * A_HEAD_DIM).astype(x.dtype) @ w_out


def chunked_sgu(x, w_in, norm_g, norm_b, w_s, b_s, w_out):
    B, S, _ = x.shape
    z = jax.nn.gelu(x @ w_in)
    u, v = jnp.split(z, 2, axis=-1)
    v = layer_norm(v, norm_g, norm_b)
    n = S // B_CHUNK
    vc = v.reshape(B, n, B_CHUNK, B_GROUPS, B_WIDTH // B_GROUPS)
    w_causal = w_s * jnp.tril(jnp.ones((B_CHUNK, B_CHUNK), w_s.dtype))
    f = jnp.einsum('gts,bnsgc->bntgc', w_causal, vc) + b_s.T[:, :, None]
    return (u * f.reshape(B, S, B_WIDTH)) @ w_out


def causal_depthwise_conv(x, w):
    C = x.shape[-1]
    return lax.conv_general_dilated(x, w[:, None, :].astype(x.dtype), window_strides=(1,),
                                    padding=[(w.shape[0] - 1, 0)],
                                    dimension_numbers=('NWC', 'WIO', 'NWC'),
                                    feature_group_count=C)


def l2norm(t):
    t = t.astype(jnp.float32)
    return t * lax.rsqrt(jnp.sum(t * t, axis=-1, keepdims=True) + RMS_EPS)


def chunk_gated_delta_rule(q, k, v, g, beta):
    B, S, H, Dk = q.shape
    Dv = v.shape[-1]
    C = C_CHUNK
    N = S // C

    def chunks(t):
        return t.reshape(B, N, C, H, t.shape[-1]).transpose(1, 0, 3, 2, 4)

    q = chunks(q * (Dk ** -0.5))
    k = chunks(k)
    v = chunks(v)
    g = jnp.cumsum(g.reshape(B, N, C, H).transpose(1, 0, 3, 2), axis=-1)
    beta = beta.reshape(B, N, C, H).transpose(1, 0, 3, 2)
    kb = k * beta[..., None]
    vb = v * beta[..., None]
    tri = jnp.tril(jnp.ones((C, C), bool))
    strict = jnp.tril(jnp.ones((C, C), jnp.float32), -1)
    decay = jnp.exp(jnp.where(tri, g[..., :, None] - g[..., None, :], -jnp.inf))
    lower = jnp.einsum('nbhcd,nbhsd->nbhcs', kb, k) * decay * strict
    eye = jnp.eye(C, dtype=jnp.float32)
    t_inv = lax.linalg.triangular_solve(eye + lower, jnp.broadcast_to(eye, lower.shape),
                                        left_side=True, lower=True)
    w_val = t_inv @ vb
    k_cum = t_inv @ (kb * jnp.exp(g)[..., None])
    intra = jnp.einsum('nbhcd,nbhsd->nbhcs', q, k) * decay

    def step(state, xs):
        q_i, k_i, w_i, kc_i, g_i, a_i = xs
        v_new = w_i - kc_i @ state
        o = (q_i * jnp.exp(g_i)[..., None]) @ state + a_i @ v_new
        g_last = g_i[..., -1]
        state = state * jnp.exp(g_last)[..., None, None] + jnp.einsum(
            'bhcd,bhce->bhde', k_i * jnp.exp(g_last[..., None] - g_i)[..., None], v_new)
        return state, o

    state0 = jnp.zeros((B, H, Dk, Dv), jnp.float32)
    _, o = lax.scan(step, state0, (q, k, w_val, k_cum, g, intra))
    return o.transpose(1, 0, 3, 2, 4).reshape(B, S, H, Dv)


def gated_deltanet(x, w_in, conv_w, a_log, dt_bias, norm_w, w_out):
    B, S, _ = x.shape
    HD = C_HEADS * C_HEAD_DIM
    proj = x @ w_in
    qkv, z, b_raw, a_raw = jnp.split(proj, [3 * HD, 4 * HD, 4 * HD + C_HEADS], axis=-1)
    qkv = jax.nn.silu(causal_depthwise_conv(qkv, conv_w))
    q, k, v = [t.reshape(B, S, C_HEADS, C_HEAD_DIM) for t in jnp.split(qkv, 3, axis=-1)]
    q = l2norm(q)
    k = l2norm(k)
    beta = jax.nn.sigmoid(b_raw.astype(jnp.float32))
    g = -jnp.exp(a_log.astype(jnp.float32)) * jax.nn.softplus(
        a_raw.astype(jnp.float32) + dt_bias.astype(jnp.float32))
    o = chunk_gated_delta_rule(q, k, v.astype(jnp.float32), g, beta)
    o = o * lax.rsqrt(jnp.mean(o * o, axis=-1, keepdims=True) + RMS_EPS) * norm_w.astype(jnp.float32)
    o = o * jax.nn.silu(z.astype(jnp.float32).reshape(B, S, C_HEADS, C_HEAD_DIM))
    return o.reshape(B, S, HD).astype(x.dtype) @ w_out


def hierarchical_moe(x, w_coarse, w_fine, w_gate, w_up, w_down):
    B, S, D = x.shape
    xt = x.reshape(-1, D)
    T = xt.shape[0]
    coarse = jax.nn.softmax((xt @ w_coarse).astype(jnp.float32), axis=-1)
    p_grp, grp = lax.top_k(coarse, 1)
    fine_all = jnp.einsum('td,dge->tge', xt, w_fine).astype(jnp.float32)
    fine = fine_all[jnp.arange(T), grp[:, 0]]
    p_exp, idx = lax.top_k(jax.nn.softmax(fine, axis=-1), MOE_TOPK)
    gates = p_grp * p_exp / jnp.sum(p_exp, axis=-1, keepdims=True)
    expert = (grp * MOE_PER_GROUP + idx).reshape(-1)
    order = jnp.argsort(expert)
    tok = order // MOE_TOPK
    sizes = jnp.bincount(expert, length=MOE_EXPERTS).astype(jnp.int32)
    xs = xt[tok]
    h = jax.nn.silu(lax.ragged_dot(xs, w_gate, sizes)) * lax.ragged_dot(xs, w_up, sizes)
    y = lax.ragged_dot(h, w_down, sizes) * gates.reshape(-1)[order][:, None].astype(x.dtype)
    return jnp.zeros_like(xt).at[tok].add(y).reshape(B, S, D)


def setup_inputs(seed: int = 0) -> dict:
    key = jax.random.key(seed)
    ks = jax.random.split(key, 24)
    f32 = jnp.float32

    def nrm(k, shape, fan_in, scale=1.0):
        return jax.random.normal(k, shape, f32) * (scale * fan_in ** -0.5)

    a_qkv = 3 * len(A_GROUPS) * A_HEADS * A_HEAD_DIM
    hd_a = A_HEADS * A_HEAD_DIM
    hd_c = C_HEADS * C_HEAD_DIM
    c_in = 4 * hd_c + 2 * C_HEADS
    dt = jnp.exp(jax.random.uniform(ks[13], (N_C, C_HEADS), f32,
                                    minval=math.log(1e-3), maxval=math.log(0.1)))
    return {
        'x': jax.random.normal(ks[0], (BATCH, SEQ, D_MODEL), f32),
        'rel_bias': 0.1 * jax.random.normal(ks[1], (NUM_BUCKETS, A_HEADS), f32),
        'a_w_in': nrm(ks[2], (N_A, D_MODEL, a_qkv), D_MODEL),
        'a_w_out': nrm(ks[3], (N_A, hd_a, D_MODEL), hd_a, DEEPNORM_BETA),
        'b_w_in': nrm(ks[4], (N_B, D_MODEL, 2 * B_WIDTH), D_MODEL),
        'b_norm_g': 1.0 + 0.02 * jax.random.normal(ks[5], (N_B, B_WIDTH), f32),
        'b_norm_b': 0.02 * jax.random.normal(ks[6], (N_B, B_WIDTH), f32),
        'b_w_s': nrm(ks[7], (N_B, B_GROUPS, B_CHUNK, B_CHUNK), B_CHUNK),
        'b_b_s': 1.0 + 0.1 * jax.random.normal(ks[8], (N_B, B_GROUPS, B_CHUNK), f32),
        'b_w_out': nrm(ks[9], (N_B, B_WIDTH, D_MODEL), B_WIDTH, DEEPNORM_BETA),
        'c_w_in': nrm(ks[10], (N_C, D_MODEL, c_in), D_MODEL),
        'c_conv': nrm(ks[11], (N_C, C_CONV, 3 * hd_c), C_CONV),
        'c_a_log': jnp.log(jax.random.uniform(ks[12], (N_C, C_HEADS), f32, minval=1.0, maxval=16.0)),
        'c_dt_bias': dt + jnp.log(-jnp.expm1(-dt)),
        'c_norm_w': 1.0 + 0.02 * jax.random.normal(ks[14], (N_C, C_HEAD_DIM), f32),
        'c_w_out': nrm(ks[15], (N_C, hd_c, D_MODEL), hd_c, DEEPNORM_BETA),
        'ln_g': 1.0 + 0.02 * jax.random.normal(ks[16], (DEPTH, 2, D_MODEL), f32),
        'ln_b': 0.02 * jax.random.normal(ks[17], (DEPTH, 2, D_MODEL), f32),
        'moe_w_coarse': nrm(ks[18], (DEPTH, D_MODEL, MOE_GROUPS), D_MODEL),
        'moe_w_fine': nrm(ks[19], (DEPTH, D_MODEL, MOE_GROUPS, MOE_PER_GROUP), D_MODEL),
        'moe_w_gate': nrm(ks[20], (DEPTH, MOE_EXPERTS, D_MODEL, MOE_HIDDEN), D_MODEL),
        'moe_w_up': nrm(ks[21], (DEPTH, MOE_EXPERTS, D_MODEL, MOE_HIDDEN), D_MODEL),
        'moe_w_down': nrm(ks[22], (DEPTH, MOE_EXPERTS, MOE_HIDDEN, D_MODEL), MOE_HIDDEN, DEEPNORM_BETA),
    }


def reference(x, rel_bias, a_w_in, a_w_out, b_w_in, b_norm_g, b_norm_b, b_w_s, b_b_s, b_w_out,
              c_w_in, c_conv, c_a_log, c_dt_bias, c_norm_w, c_w_out, ln_g, ln_b,
              moe_w_coarse, moe_w_fine, moe_w_gate, moe_w_up, moe_w_down):
    for i in range(DEPTH):
        kind, j = i % N_MIXERS, i // N_MIXERS
        if kind == 0:
            h = dilated_attention(x, a_w_in[j], a_w_out[j], rel_bias)
        elif kind == 1:
            h = chunked_sgu(x, b_w_in[j], b_norm_g[j], b_norm_b[j], b_w_s[j], b_b_s[j], b_w_out[j])
        else:
            h = gated_deltanet(x, c_w_in[j], c_conv[j], c_a_log[j], c_dt_bias[j], c_norm_w[j], c_w_out[j])
        x = layer_norm(DEEPNORM_ALPHA * x + h, ln_g[i, 0], ln_b[i, 0])
        h = hierarchical_moe(x, moe_w_coarse[i], moe_w_fine[i], moe_w_gate[i], moe_w_up[i], moe_w_down[i])
        x = layer_norm(DEEPNORM_ALPHA * x + h, ln_g[i, 1], ln_b[i, 1])
    return x
```

```python
import functools
import math

import numpy as np
import jax
import jax.numpy as jnp
from jax import lax
from jax.experimental import pallas as pl
from jax.experimental.pallas import tpu as pltpu

F32 = jnp.float32
BF16 = jnp.bfloat16
U32 = jnp.uint32
I32 = jnp.int32

D_MODEL = 1024
DEPTH = 4
A_GROUPS = ((128, 1), (512, 4), (2048, 16))
A_HEADS = 16
A_HEAD_DIM = 64
A_BLOCK = 128
NUM_BUCKETS = 32
MAX_DISTANCE = 2048
B_CHUNK = 128
B_WIDTH = 2 * D_MODEL
B_GROUPS = 16
C_HEADS = 8
C_HEAD_DIM = 128
C_CONV = 4
C_CHUNK = 64
MOE_GROUPS = 8
MOE_PER_GROUP = 8
MOE_EXPERTS = 64
MOE_HIDDEN = 512
LN_EPS = 1e-5
RMS_EPS = 1e-6
ALPHA = (2 * DEPTH) ** 0.25

LANES = 128
NEG = -1e30
VMEM_LIMIT = 56 * 1024 * 1024
MOE_TM = 256
C_SUPER = 256


def _cparams(n_axes, vmem=VMEM_LIMIT):
    return pltpu.CompilerParams(dimension_semantics=("arbitrary",) * n_axes, vmem_limit_bytes=vmem)


def _const_spec(shape):
    nd = len(shape)
    return pl.BlockSpec(shape, lambda *_: (0,) * nd, pipeline_mode=pl.Buffered(1))


def _ln_rows(y, g, b):
    mu = jnp.mean(y, axis=-1, keepdims=True)
    yc = y - mu
    var = jnp.mean(yc * yc, axis=-1, keepdims=True)
    return yc * lax.rsqrt(var + LN_EPS) * g + b


def _pack_pairs(y):
    w = y.shape[1] // 2
    lo = lax.bitcast_convert_type(y[:, :w].astype(BF16).astype(F32), U32)
    hi = lax.bitcast_convert_type(y[:, w:].astype(BF16).astype(F32), U32)
    return (lo >> 16) | (hi & jnp.uint32(0xFFFF0000))


def _unpack_lo(p):
    return lax.bitcast_convert_type(p << 16, F32)


def _unpack_hi(p):
    return lax.bitcast_convert_type(p & jnp.uint32(0xFFFF0000), F32)


def _split3(a):
    h = a.astype(BF16)
    r = a - h.astype(F32)
    m = r.astype(BF16)
    l = (r - m.astype(F32)).astype(BF16)
    return h, m, l


def _dot(a, b):
    return jnp.dot(a, b, preferred_element_type=F32)


def _dot_nt(a, b):
    return lax.dot_general(a, b, (((1,), (1,)), ((), ())), preferred_element_type=F32)


def _gelu_tanh(x):
    return 0.5 * x * (1.0 + jnp.tanh(0.7978845608028654 * (x + 0.044715 * (x * x * x))))


def _sigmoid(x):
    return 1.0 / (1.0 + jnp.exp(-x))


def _proj_body(x_ref, w_ref, o_ref, xb_ref):
    @pl.when(pl.program_id(1) == 0)
    def _():
        xb_ref[...] = x_ref[...].astype(BF16)

    o_ref[...] = _dot(xb_ref[...], w_ref[...]).astype(o_ref.dtype)


def _proj(x2, w, out_dtype, tm=1024, tn=1024):
    t, k = x2.shape
    n = w.shape[1]
    tn = min(tn, n)
    return pl.pallas_call(
        _proj_body,
        grid=(t // tm, n // tn),
        in_specs=[pl.BlockSpec((tm, k), lambda i, j: (i, 0)),
                  pl.BlockSpec((k, tn), lambda i, j: (0, j))],
        out_specs=pl.BlockSpec((tm, tn), lambda i, j: (i, j)),
        out_shape=jax.ShapeDtypeStruct((t, n), out_dtype),
        scratch_shapes=[pltpu.VMEM((tm, k), BF16)],
        compiler_params=_cparams(2),
        name="proj",
    )(x2, w)


def _proj3_body(x_ref, wh_ref, wl_ref, o_ref):
    x = x_ref[...]
    xh = x.astype(BF16)
    xl = (x - xh.astype(F32)).astype(BF16)
    wh = wh_ref[...]
    o_ref[...] = _dot(xh, wh) + _dot(xl, wh) + _dot(xh, wl_ref[...])


def _proj3(x2, w, tm=1024):
    t, k = x2.shape
    n = w.shape[1]
    wh = w.astype(BF16)
    wl = (w - wh.astype(F32)).astype(BF16)
    return pl.pallas_call(
        _proj3_body,
        grid=(t // tm,),
        in_specs=[pl.BlockSpec((tm, k), lambda i: (i, 0)), _const_spec((k, n)), _const_spec((k, n))],
        out_specs=pl.BlockSpec((tm, n), lambda i: (i, 0)),
        out_shape=jax.ShapeDtypeStruct((t, n), F32),
        compiler_params=_cparams(1),
        name="proj3",
    )(x2, wh, wl)


def _mm_res_ln_body(a_ref, w_ref, x_ref, g_ref, b_ref, o_ref, p_ref):
    y = _dot(a_ref[...], w_ref[...])
    xn = _ln_rows(ALPHA * x_ref[...] + y, g_ref[...], b_ref[...])
    o_ref[...] = xn
    p_ref[...] = _pack_pairs(xn)


def _mm_res_ln(a, w, x2, g, b, tm=512):
    t, k = a.shape
    d = w.shape[1]
    return pl.pallas_call(
        _mm_res_ln_body,
        grid=(t // tm,),
        in_specs=[pl.BlockSpec((tm, k), lambda i: (i, 0)), _const_spec((k, d)),
                  pl.BlockSpec((tm, d), lambda i: (i, 0)), _const_spec((1, d)), _const_spec((1, d))],
        out_specs=[pl.BlockSpec((tm, d), lambda i: (i, 0)), pl.BlockSpec((tm, d // 2), lambda i: (i, 0))],
        out_shape=[jax.ShapeDtypeStruct((t, d), F32), jax.ShapeDtypeStruct((t, d // 2), U32)],
        compiler_params=_cparams(1),
        name="mm_res_ln",
    )(a, w, x2, g.reshape(1, d), b.reshape(1, d))


def _t5_bucket(dist):
    max_exact = NUM_BUCKETS // 2
    d = jnp.maximum(dist, 1).astype(F32)
    large = max_exact + (jnp.log(d / max_exact) / math.log(MAX_DISTANCE / max_exact)
                         * (NUM_BUCKETS - max_exact)).astype(I32)
    return jnp.where(dist < max_exact, dist, jnp.minimum(large, NUM_BUCKETS - 1))


def _attn_bias(rel_bias, window, dil):
    steps = window // dil
    qi = jnp.arange(A_BLOCK)[:, None]
    ki = jnp.arange(2 * A_BLOCK)[None, :]
    rel = qi + A_BLOCK - ki
    valid = (rel >= 0) & (rel <= steps)
    bias = rel_bias[_t5_bucket(jnp.maximum(rel, 0) * dil)].astype(F32)
    bias = jnp.where(valid[:, :, None], bias, NEG)
    return bias.transpose(2, 0, 1)


def _attn_body(q_ref, k_ref, v_ref, kp_ref, vp_ref, bias_ref, o_ref, lse_ref, *, nblk):
    li = pl.program_id(2)
    lane = lax.broadcasted_iota(I32, (1, LANES), 1)
    col2 = lax.broadcasted_iota(I32, (1, 2 * A_BLOCK), 1)
    first_pen = jnp.where(col2 < A_BLOCK, jnp.where(li == 0, NEG, 0.0).astype(F32), 0.0)
    head0 = lane < A_HEAD_DIM
    lse_ref[...] = jnp.zeros_like(lse_ref)

    def hp_body(hp, carry):
        c0 = pl.multiple_of(hp * LANES, LANES)
        for j in range(nblk):
            rows = pl.ds(j * A_BLOCK, A_BLOCK)
            q = q_ref[rows, pl.ds(c0, LANES)]
            if j == 0:
                kb = jnp.concatenate([kp_ref[:, pl.ds(c0, LANES)], k_ref[rows, pl.ds(c0, LANES)]], axis=0)
                vb = jnp.concatenate([vp_ref[:, pl.ds(c0, LANES)], v_ref[rows, pl.ds(c0, LANES)]], axis=0)
            else:
                band = pl.ds((j - 1) * A_BLOCK, 2 * A_BLOCK)
                kb = k_ref[band, pl.ds(c0, LANES)]
                vb = v_ref[band, pl.ds(c0, LANES)]
            outs = []
            lses = []
            for e in range(2):
                qm = jnp.where(head0 if e == 0 else jnp.logical_not(head0), q, jnp.zeros_like(q))
                s = _dot_nt(qm, kb) + bias_ref[2 * hp + e]
                if j == 0:
                    s = s + first_pen
                m = jnp.max(s, axis=-1, keepdims=True)
                p = jnp.exp(s - m)
                l = jnp.sum(p, axis=-1, keepdims=True)
                outs.append(_dot(p.astype(BF16), vb) / l)
                lses.append(m + jnp.log(l))
            o_ref[rows, pl.ds(c0, LANES)] = jnp.where(head0, outs[0], outs[1]).astype(o_ref.dtype)
            cur = lse_ref[rows, :]
            cur = jnp.where(lane == 2 * hp, lses[0], cur)
            cur = jnp.where(lane == 2 * hp + 1, lses[1], cur)
            lse_ref[rows, :] = cur
        return carry

    lax.fori_loop(0, A_HEADS // 2, hp_body, 0)


def _dilated_group(qkv, g, window, dil, bias, bsz, seq):
    hd = A_HEADS * A_HEAD_DIM
    L = seq // dil
    tl = min(512, L)
    nblk = tl // A_BLOCK
    qkv4 = qkv.reshape(bsz, L, dil * 9 * hd)

    def blk(col):
        return pl.BlockSpec((None, tl, hd), lambda b, r, li: (b, li, r * 9 + col))

    def prev(col):
        return pl.BlockSpec((None, A_BLOCK, hd),
                            lambda b, r, li: (b, jnp.maximum(li * nblk - 1, 0), r * 9 + col))

    o, lse = pl.pallas_call(
        functools.partial(_attn_body, nblk=nblk),
        grid=(bsz, dil, L // tl),
        in_specs=[blk(3 * g), blk(3 * g + 1), blk(3 * g + 2), prev(3 * g + 1), prev(3 * g + 2),
                  _const_spec((A_HEADS, A_BLOCK, 2 * A_BLOCK))],
        out_specs=[pl.BlockSpec((None, tl, hd), lambda b, r, li: (b, li, r)),
                   pl.BlockSpec((None, tl, LANES), lambda b, r, li: (b, li, r))],
        out_shape=[jax.ShapeDtypeStruct((bsz, L, dil * hd), BF16),
                   jax.ShapeDtypeStruct((bsz, L, dil * LANES), F32)],
        compiler_params=_cparams(3),
        name=f"dilated_attn_d{dil}",
    )(qkv4, qkv4, qkv4, qkv4, qkv4, bias)
    return o.reshape(bsz * seq, hd), lse.reshape(bsz * seq, LANES)


def _attn_out_body(o1_ref, o2_ref, o3_ref, l1_ref, l2_ref, l3_ref, e_ref, w_ref, x_ref, g_ref, b_ref,
                   xo_ref, p_ref):
    l1, l2, l3 = l1_ref[...], l2_ref[...], l3_ref[...]
    m = jnp.maximum(jnp.maximum(l1, l2), l3)
    e1, e2, e3 = jnp.exp(l1 - m), jnp.exp(l2 - m), jnp.exp(l3 - m)
    inv = 1.0 / (e1 + e2 + e3)
    ex = e_ref[...]

    def expand(wt):
        h = wt.astype(BF16)
        lo = (wt - h.astype(F32)).astype(BF16)
        return _dot(h, ex) + _dot(lo, ex)

    comb = (expand(e1 * inv) * o1_ref[...].astype(F32) + expand(e2 * inv) * o2_ref[...].astype(F32)
            + expand(e3 * inv) * o3_ref[...].astype(F32))
    y = _dot(comb.astype(BF16), w_ref[...])
    xn = _ln_rows(ALPHA * x_ref[...] + y, g_ref[...], b_ref[...])
    xo_ref[...] = xn
    p_ref[...] = _pack_pairs(xn)


def _attn_out(os, lses, w_out, x2, g, b, tm=512):
    t, d = x2.shape
    expand = np.zeros((LANES, d), np.float32)
    for h in range(A_HEADS):
        expand[h, h * A_HEAD_DIM:(h + 1) * A_HEAD_DIM] = 1.0
    row = lambda n: pl.BlockSpec((tm, n), lambda i: (i, 0))
    return pl.pallas_call(
        _attn_out_body,
        grid=(t // tm,),
        in_specs=[row(d), row(d), row(d), row(LANES), row(LANES), row(LANES),
                  _const_spec((LANES, d)), _const_spec((d, d)), row(d), _const_spec((1, d)), _const_spec((1, d))],
        out_specs=[row(d), row(d // 2)],
        out_shape=[jax.ShapeDtypeStruct((t, d), F32), jax.ShapeDtypeStruct((t, d // 2), U32)],
        compiler_params=_cparams(1),
        name="attn_out",
    )(*os, *lses, jnp.asarray(expand, BF16), w_out, x2, g.reshape(1, d), b.reshape(1, d))


def _mixer_a(x2, w_in, w_out, rel_bias, g, b, bsz, seq):
    hd = A_HEADS * A_HEAD_DIM
    scale = np.ones((9 * hd,), np.float32)
    for gi in range(len(A_GROUPS)):
        scale[3 * gi * hd:(3 * gi + 1) * hd] = A_HEAD_DIM ** -0.5
    qkv = _proj(x2, (w_in * scale).astype(BF16), BF16)
    os, lses = [], []
    for gi, (window, dil) in enumerate(A_GROUPS):
        o, lse = _dilated_group(qkv, gi, window, dil, _attn_bias(rel_bias, window, dil), bsz, seq)
        os.append(o)
        lses.append(lse)
    return _attn_out(os, lses, w_out.astype(BF16), x2, g, b)


def _sgu_body(x_ref, wu_ref, wv_ref, ng_ref, nb_ref, wc_ref, bs_ref, a_ref, vb_ref, *, tm):
    xb = x_ref[...].astype(BF16)
    v = _gelu_tanh(_dot(xb, wv_ref[...]))
    vb_ref[...] = _ln_rows(v, ng_ref[...], nb_ref[...]).astype(BF16)
    gw = B_WIDTH // B_GROUPS
    ucols = 512
    for j in range(B_WIDTH // ucols):
        u = _gelu_tanh(_dot(xb, wu_ref[:, j * ucols:(j + 1) * ucols]))
        for c in range(tm // B_CHUNK):
            rows = slice(c * B_CHUNK, (c + 1) * B_CHUNK)
            for gg in range(ucols // gw):
                gi = j * (ucols // gw) + gg
                cols = slice(gi * gw, (gi + 1) * gw)
                f = _dot(wc_ref[gi], vb_ref[rows, cols]) + bs_ref[:, cols]
                a_ref[rows, cols] = (u[rows, gg * gw:(gg + 1) * gw] * f).astype(a_ref.dtype)


def _mixer_b(x2, w_in, norm_g, norm_b, w_s, b_s, w_out, g, b, tm=256):
    t, d = x2.shape
    wu = w_in[:, :B_WIDTH].astype(BF16)
    wv = w_in[:, B_WIDTH:].astype(BF16)
    wc = (w_s * jnp.tril(jnp.ones((B_CHUNK, B_CHUNK), w_s.dtype))).astype(BF16)
    bs_full = jnp.repeat(b_s.T, B_WIDTH // B_GROUPS, axis=1)
    a = pl.pallas_call(
        functools.partial(_sgu_body, tm=tm),
        grid=(t // tm,),
        in_specs=[pl.BlockSpec((tm, d), lambda i: (i, 0)), _const_spec((d, B_WIDTH)), _const_spec((d, B_WIDTH)),
                  _const_spec((1, B_WIDTH)), _const_spec((1, B_WIDTH)),
                  _const_spec((B_GROUPS, B_CHUNK, B_CHUNK)), _const_spec((B_CHUNK, B_WIDTH))],
        out_specs=pl.BlockSpec((tm, B_WIDTH), lambda i: (i, 0)),
        out_shape=jax.ShapeDtypeStruct((t, B_WIDTH), BF16),
        scratch_shapes=[pltpu.VMEM((tm, B_WIDTH), BF16)],
        compiler_params=_cparams(1),
        name="sgu",
    )(x2, wu, wv, norm_g.reshape(1, -1), norm_b.reshape(1, -1), wc, bs_full)
    return _mm_res_ln(a, w_out.astype(BF16), x2, g, b)


def _delta_body(qkvz_ref, gates_ref, cw_ref, alog_ref, dtb_ref, nw_ref, o_ref, ext_ref, state_ref, vnew_ref):
    ts = C_SUPER
    hd = C_HEADS * C_HEAD_DIM
    dk = C_HEAD_DIM
    nch = ts // C_CHUNK

    @pl.when(pl.program_id(1) == 0)
    def _():
        ext_ref[0:8, :] = jnp.zeros((8, 3 * hd), F32)
        state_ref[...] = jnp.zeros_like(state_ref)

    ext_ref[8:8 + ts, :] = qkvz_ref[:, 0:3 * hd].astype(F32)

    r_i = lax.broadcasted_iota(I32, (ts, ts), 0)
    c_i = lax.broadcasted_iota(I32, (ts, ts), 1)
    same = (r_i >> 6) == (c_i >> 6)
    le = jnp.logical_and(same, c_i <= r_i)
    strict = jnp.logical_and(same, c_i < r_i)
    eye = jnp.where(r_i == c_i, 1.0, 0.0).astype(F32)
    u_cum = jnp.where(jnp.logical_and(same, r_i <= c_i), 1.0, 0.0).astype(BF16)
    u_tot = jnp.where(same, 1.0, 0.0).astype(BF16)

    gt = gates_ref[...].T
    zt = gt + dtb_ref[...]
    g_rows = -jnp.exp(alog_ref[...]) * (jnp.maximum(zt, 0.0) + jnp.log(1.0 + jnp.exp(-jnp.abs(zt))))
    gh, gm, gl = _split3(g_rows)
    gcum_rows = _dot(gh, u_cum) + _dot(gm, u_cum) + _dot(gl, u_cum)
    gtot_rows = _dot(gh, u_tot) + _dot(gm, u_tot) + _dot(gl, u_tot)
    gcum_cols = gcum_rows.T
    gtot_cols = gtot_rows.T
    beta_cols = _sigmoid(gates_ref[...])
    lane = lax.broadcasted_iota(I32, (1, LANES), 1)
    sub = lax.broadcasted_iota(I32, (LANES, 1), 0)
    csel = lax.broadcasted_iota(I32, (1, ts), 1) >> 6

    def conv_silu(c0):
        cols = pl.ds(pl.multiple_of(c0, dk), dk)
        y = (cw_ref[0:1, cols] * ext_ref[pl.ds(5, ts), cols] + cw_ref[1:2, cols] * ext_ref[pl.ds(6, ts), cols]
             + cw_ref[2:3, cols] * ext_ref[pl.ds(7, ts), cols] + cw_ref[3:4, cols] * ext_ref[pl.ds(8, ts), cols])
        return y * _sigmoid(y)

    def head_body(h, carry):
        c0 = pl.multiple_of(h * dk, dk)
        pick_a = lane == (C_HEADS + h)
        gcol = jnp.sum(jnp.where(pick_a, gcum_cols, 0.0), axis=1, keepdims=True)
        glcol = jnp.sum(jnp.where(pick_a, gtot_cols, 0.0), axis=1, keepdims=True)
        bcol = jnp.sum(jnp.where(lane == h, beta_cols, 0.0), axis=1, keepdims=True)
        pick_r = sub == (C_HEADS + h)
        grow = jnp.sum(jnp.where(pick_r, gcum_rows, 0.0), axis=0, keepdims=True)
        glrow = jnp.sum(jnp.where(pick_r, gtot_rows, 0.0), axis=0, keepdims=True)

        q = conv_silu(c0)
        k = conv_silu(c0 + hd)
        v = conv_silu(c0 + 2 * hd)
        q = q * lax.rsqrt(jnp.sum(q * q, axis=-1, keepdims=True) + RMS_EPS) * (dk ** -0.5)
        k = k * lax.rsqrt(jnp.sum(k * k, axis=-1, keepdims=True) + RMS_EPS)
        eg = jnp.exp(gcol)
        kb = k * bcol
        kbf = k.astype(BF16)
        decay = jnp.exp(jnp.where(le, gcol - grow, NEG))
        lower = jnp.where(strict, _dot_nt(kb.astype(BF16), kbf) * decay, 0.0)
        intra = (_dot_nt(q.astype(BF16), kbf) * decay).astype(BF16)

        mpow = -lower
        tinv = eye + mpow
        for _ in range(5):
            mb = mpow.astype(BF16)
            mpow = _dot(mb, mb)
            tinv = tinv + _dot(tinv.astype(BF16), mpow.astype(BF16))
        wk = _dot(tinv.astype(BF16), jnp.concatenate([v * bcol, kb * eg], axis=1).astype(BF16))
        wv_, kc = wk[:, :dk], wk[:, dk:]
        qe = q * eg
        ktil_t = (k * jnp.exp(glcol - gcol)).T

        vnew_ref[...] = jnp.zeros_like(vnew_ref)
        s = state_ref[h]
        outs = []
        for j in range(nch):
            rows = slice(j * C_CHUNK, (j + 1) * C_CHUNK)
            a1 = _dot(jnp.concatenate([kc[rows], qe[rows]], axis=0).astype(BF16), s.astype(BF16))
            vnew_ref[rows, :] = (wv_[rows] - a1[:C_CHUNK]).astype(BF16)
            vn_all = vnew_ref[...]
            outs.append(a1[C_CHUNK:] + _dot(intra[rows, :], vn_all))
            kt_j = jnp.where(csel == j, ktil_t, 0.0).astype(BF16)
            dg = jnp.exp(jnp.sum(jnp.where(csel == j, glrow, 0.0), axis=1, keepdims=True) * (1.0 / C_CHUNK))
            s = s * dg + _dot(kt_j, vn_all)
        state_ref[h] = s
        o = jnp.concatenate(outs, axis=0)
        o = o * lax.rsqrt(jnp.mean(o * o, axis=-1, keepdims=True) + RMS_EPS) * nw_ref[...]
        z = qkvz_ref[:, pl.ds(pl.multiple_of(c0 + 3 * hd, dk), dk)].astype(F32)
        o_ref[:, pl.ds(c0, dk)] = (o * (z * _sigmoid(z))).astype(o_ref.dtype)
        return carry

    lax.fori_loop(0, C_HEADS, head_body, 0)
    ext_ref[0:8, :] = ext_ref[ts:ts + 8, :]


def _mixer_c(x2, w_in, conv_w, a_log, dt_bias, norm_w, w_out, g, b, bsz, seq):
    t, d = x2.shape
    hd = C_HEADS * C_HEAD_DIM
    qkvz = _proj(x2, w_in[:, :4 * hd].astype(BF16), BF16)
    wg = jnp.zeros((d, LANES), F32).at[:, :2 * C_HEADS].set(w_in[:, 4 * hd:])
    gates = _proj3(x2, wg)
    col = lambda v: jnp.zeros((LANES, 1), F32).at[C_HEADS:2 * C_HEADS, 0].set(v.astype(F32))
    nsteps = seq // C_SUPER
    o = pl.pallas_call(
        _delta_body,
        grid=(bsz, nsteps),
        in_specs=[pl.BlockSpec((C_SUPER, 4 * hd), lambda bi, i: (bi * nsteps + i, 0)),
                  pl.BlockSpec((C_SUPER, LANES), lambda bi, i: (bi * nsteps + i, 0)),
                  _const_spec((C_CONV, 3 * hd)), _const_spec((LANES, 1)), _const_spec((LANES, 1)),
                  _const_spec((1, C_HEAD_DIM))],
        out_specs=pl.BlockSpec((C_SUPER, hd), lambda bi, i: (bi * nsteps + i, 0)),
        out_shape=jax.ShapeDtypeStruct((t, hd), BF16),
        scratch_shapes=[pltpu.VMEM((C_SUPER + 8, 3 * hd), F32),
                        pltpu.VMEM((C_HEADS, C_HEAD_DIM, C_HEAD_DIM), F32),
                        pltpu.VMEM((C_SUPER, C_HEAD_DIM), BF16)],
        compiler_params=_cparams(2),
        name="deltanet",
    )(qkvz, gates, conv_w.astype(F32), col(a_log), col(dt_bias), norm_w.reshape(1, -1).astype(F32))
    return _mm_res_ln(o, w_out.astype(BF16), x2, g, b)


def _route_body(x_ref, wh_ref, wl_ref, o_ref, cnt_ref, run_ref, *, tm):
    @pl.when(pl.program_id(0) == 0)
    def _():
        run_ref[...] = jnp.zeros_like(run_ref)

    x = x_ref[...]
    xh = x.astype(BF16)
    xl = (x - xh.astype(F32)).astype(BF16)
    wh = wh_ref[...]
    logits = _dot(xh, wh) + _dot(xl, wh) + _dot(xh, wl_ref[...])
    lane = lax.broadcasted_iota(I32, (1, LANES), 1)
    lane_f = lane.astype(F32)

    def top1(vals):
        m = jnp.max(vals, axis=-1, keepdims=True)
        idx = jnp.min(jnp.where(vals == m, lane_f, 1e9), axis=-1, keepdims=True)
        return m, idx.astype(I32)

    lc = jnp.where(lane < MOE_GROUPS, logits, NEG)
    mc, grp = top1(lc)
    p_grp = 1.0 / jnp.sum(jnp.exp(lc - mc), axis=-1, keepdims=True)
    lo = MOE_GROUPS + MOE_PER_GROUP * grp
    lf = jnp.where(jnp.logical_and(lane >= lo, lane < lo + MOE_PER_GROUP), logits, NEG)
    m1, i1 = top1(lf)
    m2, i2 = top1(jnp.where(lane == i1, NEG, lf))
    e21 = jnp.exp(m2 - m1)
    g1 = p_grp / (1.0 + e21)
    g2 = p_grp * e21 / (1.0 + e21)
    ex1 = i1 - MOE_GROUPS
    ex2 = i2 - MOE_GROUPS

    oh1 = jnp.where(lane == ex1, 1.0, 0.0).astype(F32)
    oh2 = jnp.where(lane == ex2, 1.0, 0.0).astype(F32)
    ohs = oh1 + oh2
    r_i = lax.broadcasted_iota(I32, (tm, tm), 0)
    c_i = lax.broadcasted_iota(I32, (tm, tm), 1)
    before = _dot(jnp.where(c_i < r_i, 1.0, 0.0).astype(BF16), ohs.astype(BF16)) + run_ref[...]
    rank1 = jnp.sum(oh1 * before, axis=-1, keepdims=True)
    rank2 = jnp.sum(oh2 * before, axis=-1, keepdims=True)
    run_ref[...] = run_ref[...] + jnp.sum(ohs, axis=0, keepdims=True)

    out = jnp.where(lane == 0, ex1.astype(F32), 0.0)
    out = jnp.where(lane == 1, ex2.astype(F32), out)
    out = jnp.where(lane == 2, g1, out)
    out = jnp.where(lane == 3, g2, out)
    out = jnp.where(lane == 4, rank1, out)
    out = jnp.where(lane == 5, rank2, out)
    o_ref[...] = out
    cnt_ref[...] = jnp.broadcast_to(run_ref[...], cnt_ref.shape)


def _route(x2, w_coarse, w_fine, tm=512):
    t, d = x2.shape
    wr = jnp.zeros((d, LANES), F32)
    wr = wr.at[:, :MOE_GROUPS].set(w_coarse).at[:, MOE_GROUPS:MOE_GROUPS + MOE_EXPERTS].set(
        w_fine.reshape(d, MOE_EXPERTS))
    wh = wr.astype(BF16)
    wl = (wr - wh.astype(F32)).astype(BF16)
    return pl.pallas_call(
        functools.partial(_route_body, tm=tm),
        grid=(t // tm,),
        in_specs=[pl.BlockSpec((tm, d), lambda i: (i, 0)), _const_spec((d, LANES)), _const_spec((d, LANES))],
        out_specs=[pl.BlockSpec((tm, LANES), lambda i: (i, 0)), pl.BlockSpec((8, LANES), lambda i: (0, 0))],
        out_shape=[jax.ShapeDtypeStruct((t, LANES), F32), jax.ShapeDtypeStruct((8, LANES), F32)],
        scratch_shapes=[pltpu.VMEM((1, LANES), F32)],
        compiler_params=_cparams(1),
        name="moe_route",
    )(x2, wh, wl)


def _gather_body(nv_ref, idx_ref, xp_ref, o_ref, *, tm):
    i = pl.program_id(0)

    @pl.when(i < nv_ref[0])
    def _():
        def row(r, carry):
            o_ref[pl.ds(r, 1), :] = xp_ref[pl.ds(idx_ref[0, r], 1), :]
            return carry
        lax.fori_loop(0, tm, row, 0, unroll=8)

    @pl.when(i >= nv_ref[0])
    def _():
        o_ref[...] = jnp.zeros_like(o_ref)


def _dispatch(xp, src_tok, nvalid, n_tiles):
    t, w = xp.shape
    tm = MOE_TM
    return pl.pallas_call(
        functools.partial(_gather_body, tm=tm),
        grid_spec=pltpu.PrefetchScalarGridSpec(
            num_scalar_prefetch=1,
            grid=(n_tiles,),
            in_specs=[pl.BlockSpec((None, 1, tm), lambda i, nv: (i, 0, 0), memory_space=pltpu.SMEM),
                      pl.BlockSpec((t, w), lambda i, nv: (0, 0), pipeline_mode=pl.Buffered(1))],
            out_specs=pl.BlockSpec((tm, w), lambda i, nv: (i, 0)),
        ),
        out_shape=jax.ShapeDtypeStruct((n_tiles * tm, w), U32),
        compiler_params=_cparams(1),
        name="moe_dispatch",
    )(nvalid, src_tok.reshape(n_tiles, 1, tm), xp)


def _expert_body(te_ref, nv_ref, xs_ref, wg_ref, wu_ref, wd_ref, y_ref, wgb_ref, wub_ref, wdb_ref):
    i = pl.program_id(0)
    half = D_MODEL // 2

    @pl.when(i < nv_ref[0])
    def _():
        changed = jnp.logical_or(i == 0, te_ref[i] != te_ref[jnp.maximum(i - 1, 0)])

        @pl.when(changed)
        def _():
            wgb_ref[...] = wg_ref[...].astype(BF16)
            wub_ref[...] = wu_ref[...].astype(BF16)
            wdb_ref[...] = wd_ref[...].astype(BF16)

        xs = xs_ref[...]
        xlo = _unpack_lo(xs).astype(BF16)
        xhi = _unpack_hi(xs).astype(BF16)
        hg = _dot(xlo, wgb_ref[0:half, :]) + _dot(xhi, wgb_ref[half:, :])
        hu = _dot(xlo, wub_ref[0:half, :]) + _dot(xhi, wub_ref[half:, :])
        h = (hg * _sigmoid(hg) * hu).astype(BF16)
        packed = _pack_pairs(_dot(h, wdb_ref[...]))
        for s in range(half // LANES):
            y_ref[s] = packed[:, s * LANES:(s + 1) * LANES]

    @pl.when(i >= nv_ref[0])
    def _():
        y_ref[...] = jnp.zeros_like(y_ref)


def _experts(xs, tile_expert, nvalid, w_gate, w_up, w_down, n_tiles):
    tm = MOE_TM
    d, hdn = D_MODEL, MOE_HIDDEN
    nslab = d // 2 // LANES
    return pl.pallas_call(
        _expert_body,
        grid_spec=pltpu.PrefetchScalarGridSpec(
            num_scalar_prefetch=2,
            grid=(n_tiles,),
            in_specs=[pl.BlockSpec((tm, d // 2), lambda i, te, nv: (i, 0)),
                      pl.BlockSpec((None, d, hdn), lambda i, te, nv: (te[i], 0, 0)),
                      pl.BlockSpec((None, d, hdn), lambda i, te, nv: (te[i], 0, 0)),
                      pl.BlockSpec((None, hdn, d), lambda i, te, nv: (te[i], 0, 0))],
            out_specs=pl.BlockSpec((nslab, tm, LANES), lambda i, te, nv: (0, i, 0)),
            scratch_shapes=[pltpu.VMEM((d, hdn), BF16), pltpu.VMEM((d, hdn), BF16), pltpu.VMEM((hdn, d), BF16)],
        ),
        out_shape=jax.ShapeDtypeStruct((nslab, n_tiles * tm, LANES), U32),
        compiler_params=_cparams(1),
        name="moe_experts",
    )(tile_expert, nvalid, xs, w_gate, w_up, w_down)


def _combine_body(p1_ref, p2_ref, y_ref, r_ref, lo_ref, hi_ref, b1_ref, b2_ref, *, tm):
    def row(r, carry):
        b1_ref[pl.ds(r, 1), :] = y_ref[pl.ds(p1_ref[0, r], 1), :]
        b2_ref[pl.ds(r, 1), :] = y_ref[pl.ds(p2_ref[0, r], 1), :]
        return carry
    lax.fori_loop(0, tm, row, 0, unroll=8)
    g1 = r_ref[:, 2:3]
    g2 = r_ref[:, 3:4]
    y1 = b1_ref[...]
    y2 = b2_ref[...]
    lo_ref[...] = g1 * _unpack_lo(y1) + g2 * _unpack_lo(y2)
    hi_ref[...] = g1 * _unpack_hi(y1) + g2 * _unpack_hi(y2)


def _combine(y, pos1, pos2, route, tm=512):
    nslab, s_pad, _ = y.shape
    t = pos1.shape[0]
    nt = t // tm
    return pl.pallas_call(
        functools.partial(_combine_body, tm=tm),
        grid=(nslab, nt),
        in_specs=[pl.BlockSpec((None, 1, tm), lambda s, i: (i, 0, 0), memory_space=pltpu.SMEM),
                  pl.BlockSpec((None, 1, tm), lambda s, i: (i, 0, 0), memory_space=pltpu.SMEM),
                  pl.BlockSpec((None, s_pad, LANES), lambda s, i: (s, 0, 0), pipeline_mode=pl.Buffered(1)),
                  pl.BlockSpec((tm, LANES), lambda s, i: (i, 0))],
        out_specs=[pl.BlockSpec((tm, LANES), lambda s, i: (i, s)), pl.BlockSpec((tm, LANES), lambda s, i: (i, s))],
        out_shape=[jax.ShapeDtypeStruct((t, nslab * LANES), F32), jax.ShapeDtypeStruct((t, nslab * LANES), F32)],
        scratch_shapes=[pltpu.VMEM((tm, LANES), U32), pltpu.VMEM((tm, LANES), U32)],
        compiler_params=_cparams(2),
        name="moe_combine",
    )(pos1.reshape(nt, 1, tm), pos2.reshape(nt, 1, tm), y, route)


def _res_ln_body(x_ref, lo_ref, hi_ref, g_ref, b_ref, o_ref):
    h = jnp.concatenate([lo_ref[...], hi_ref[...]], axis=1)
    o_ref[...] = _ln_rows(ALPHA * x_ref[...] + h, g_ref[...], b_ref[...])


def _res_ln(x2, lo, hi, g, b, tm=1024):
    t, d = x2.shape
    return pl.pallas_call(
        _res_ln_body,
        grid=(t // tm,),
        in_specs=[pl.BlockSpec((tm, d), lambda i: (i, 0)), pl.BlockSpec((tm, d // 2), lambda i: (i, 0)),
                  pl.BlockSpec((tm, d // 2), lambda i: (i, 0)), _const_spec((1, d)), _const_spec((1, d))],
        out_specs=pl.BlockSpec((tm, d), lambda i: (i, 0)),
        out_shape=jax.ShapeDtypeStruct((t, d), F32),
        compiler_params=_cparams(1),
        name="res_ln",
    )(x2, lo, hi, g.reshape(1, d), b.reshape(1, d))


def _moe(x2, xp, w_coarse, w_fine, w_gate, w_up, w_down, g, b):
    t = x2.shape[0]
    tm = MOE_TM
    n_tiles = (2 * t) // tm + MOE_EXPERTS
    route, cnt = _route(x2, w_coarse, w_fine)
    ex = route[:, 0:2].astype(I32)
    rank = route[:, 4:6].astype(I32)
    counts = cnt[0, :MOE_EXPERTS].astype(I32)
    ptiles = (counts + tm - 1) // tm
    tile_end = jnp.cumsum(ptiles)
    pstart = (tile_end - ptiles) * tm
    pos = pstart[ex] + rank
    nvalid = tile_end[-1:].astype(I32)
    tidx = jnp.arange(n_tiles, dtype=I32)
    te = jnp.searchsorted(tile_end, tidx, side="right").astype(I32)
    te = jnp.minimum(te, jnp.searchsorted(tile_end, nvalid[0] - 1, side="right").astype(I32))
    tok = jnp.arange(t, dtype=I32)
    src_tok = jnp.zeros((n_tiles * tm,), I32).at[pos.reshape(-1)].set(jnp.repeat(tok, 2))
    xs = _dispatch(xp, src_tok, nvalid, n_tiles)
    y = _experts(xs, te, nvalid, w_gate, w_up, w_down, n_tiles)
    lo, hi = _combine(y, pos[:, 0], pos[:, 1], route)
    return _res_ln(x2, lo, hi, g, b)


def kernel(x, rel_bias, a_w_in, a_w_out, b_w_in, b_norm_g, b_norm_b, b_w_s, b_b_s, b_w_out, c_w_in, c_conv,
           c_a_log, c_dt_bias, c_norm_w, c_w_out, ln_g, ln_b, moe_w_coarse, moe_w_fine, moe_w_gate, moe_w_up,
           moe_w_down):
    bsz, seq, d = x.shape
    x2 = x.reshape(bsz * seq, d)
    for i in range(DEPTH):
        kind, j = i % 3, i // 3
        g1, b1 = ln_g[i, 0], ln_b[i, 0]
        if kind == 0:
            x2, xp = _mixer_a(x2, a_w_in[j], a_w_out[j], rel_bias, g1, b1, bsz, seq)
        elif kind == 1:
            x2, xp = _mixer_b(x2, b_w_in[j], b_norm_g[j], b_norm_b[j], b_w_s[j], b_b_s[j], b_w_out[j], g1, b1)
        else:
            x2, xp = _mixer_c(x2, c_w_in[j], c_conv[j], c_a_log[j], c_dt_bias[j], c_norm_w[j], c_w_out[j],
                              g1, b1, bsz, seq)
        x2 = _moe(x2, xp, moe_w_coarse[i], moe_w_fine[i], moe_w_gate[i], moe_w_up[i], moe_w_down[i],
                  ln_g[i, 1], ln_b[i, 1])
    return x2.reshape(bsz, seq, d)
```

```python
import functools
import math

import numpy as np
import jax
import jax.numpy as jnp
from jax import lax
from jax.experimental import pallas as pl
from jax.experimental.pallas import tpu as pltpu

F32 = jnp.float32
BF16 = jnp.bfloat16
U32 = jnp.uint32
I32 = jnp.int32

D_MODEL = 1024
DEPTH = 4
A_GROUPS = ((128, 1), (512, 4), (2048, 16))
A_HEADS = 16
A_HEAD_DIM = 64
A_BLOCK = 128
NUM_BUCKETS = 32
MAX_DISTANCE = 2048
B_CHUNK = 128
B_WIDTH = 2 * D_MODEL
B_GROUPS = 16
C_HEADS = 8
C_HEAD_DIM = 128
C_CONV = 4
C_CHUNK = 64
MOE_GROUPS = 8
MOE_PER_GROUP = 8
MOE_EXPERTS = 64
MOE_HIDDEN = 512
LN_EPS = 1e-5
RMS_EPS = 1e-6
ALPHA = (2 * DEPTH) ** 0.25

LANES = 128
NEG = -1e30
VMEM_LIMIT = 56 * 1024 * 1024
MOE_TM = 256
C_SUPER = 256


def _cparams(n_axes, vmem=VMEM_LIMIT):
    return pltpu.CompilerParams(dimension_semantics=("arbitrary",) * n_axes, vmem_limit_bytes=vmem)


def _const_spec(shape):
    nd = len(shape)
    return pl.BlockSpec(shape, lambda *_: (0,) * nd, pipeline_mode=pl.Buffered(1))


def _ln_rows(y, g, b):
    mu = jnp.mean(y, axis=-1, keepdims=True)
    yc = y - mu
    var = jnp.mean(yc * yc, axis=-1, keepdims=True)
    return yc * lax.rsqrt(var + LN_EPS) * g + b


def _pack_pairs(y):
    w = y.shape[1] // 2
    lo = lax.bitcast_convert_type(y[:, :w].astype(BF16).astype(F32), U32)
    hi = lax.bitcast_convert_type(y[:, w:].astype(BF16).astype(F32), U32)
    return (lo >> 16) | (hi & jnp.uint32(0xFFFF0000))


def _unpack_lo(p):
    return lax.bitcast_convert_type(p << 16, F32)


def _unpack_hi(p):
    return lax.bitcast_convert_type(p & jnp.uint32(0xFFFF0000), F32)


def _split3(a):
    h = a.astype(BF16)
    r = a - h.astype(F32)
    m = r.astype(BF16)
    l = (r - m.astype(F32)).astype(BF16)
    return h, m, l


def _dot(a, b):
    return jnp.dot(a, b, preferred_element_type=F32)


def _dot_nt(a, b):
    return lax.dot_general(a, b, (((1,), (1,)), ((), ())), preferred_element_type=F32)


def _gelu_tanh(x):
    return 0.5 * x * (1.0 + jnp.tanh(0.7978845608028654 * (x + 0.044715 * (x * x * x))))


def _sigmoid(x):
    return 1.0 / (1.0 + jnp.exp(-x))


def _proj_body(x_ref, w_ref, o_ref, xb_ref):
    @pl.when(pl.program_id(1) == 0)
    def _():
        xb_ref[...] = x_ref[...].astype(BF16)

    o_ref[...] = _dot(xb_ref[...], w_ref[...]).astype(o_ref.dtype)


def _proj(x2, w, out_dtype, tm=1024, tn=1024):
    t, k = x2.shape
    n = w.shape[1]
    tn = min(tn, n)
    return pl.pallas_call(
        _proj_body,
        grid=(t // tm, n // tn),
        in_specs=[pl.BlockSpec((tm, k), lambda i, j: (i, 0)),
                  pl.BlockSpec((k, tn), lambda i, j: (0, j))],
        out_specs=pl.BlockSpec((tm, tn), lambda i, j: (i, j)),
        out_shape=jax.ShapeDtypeStruct((t, n), out_dtype),
        scratch_shapes=[pltpu.VMEM((tm, k), BF16)],
        compiler_params=_cparams(2),
        name="proj",
    )(x2, w)


def _proj3_body(x_ref, wh_ref, wl_ref, o_ref):
    x = x_ref[...]
    xh = x.astype(BF16)
    xl = (x - xh.astype(F32)).astype(BF16)
    wh = wh_ref[...]
    o_ref[...] = _dot(xh, wh) + _dot(xl, wh) + _dot(xh, wl_ref[...])


def _proj3(x2, w, tm=1024):
    t, k = x2.shape
    n = w.shape[1]
    wh = w.astype(BF16)
    wl = (w - wh.astype(F32)).astype(BF16)
    return pl.pallas_call(
        _proj3_body,
        grid=(t // tm,),
        in_specs=[pl.BlockSpec((tm, k), lambda i: (i, 0)), _const_spec((k, n)), _const_spec((k, n))],
        out_specs=pl.BlockSpec((tm, n), lambda i: (i, 0)),
        out_shape=jax.ShapeDtypeStruct((t, n), F32),
        compiler_params=_cparams(1),
        name="proj3",
    )(x2, wh, wl)


def _mm_res_ln_body(a_ref, w_ref, x_ref, g_ref, b_ref, o_ref, p_ref):
    y = _dot(a_ref[...], w_ref[...])
    xn = _ln_rows(ALPHA * x_ref[...] + y, g_ref[...], b_ref[...])
    o_ref[...] = xn
    p_ref[...] = _pack_pairs(xn)


def _mm_res_ln(a, w, x2, g, b, tm=512):
    t, k = a.shape
    d = w.shape[1]
    return pl.pallas_call(
        _mm_res_ln_body,
        grid=(t // tm,),
        in_specs=[pl.BlockSpec((tm, k), lambda i: (i, 0)), _const_spec((k, d)),
                  pl.BlockSpec((tm, d), lambda i: (i, 0)), _const_spec((1, d)), _const_spec((1, d))],
        out_specs=[pl.BlockSpec((tm, d), lambda i: (i, 0)), pl.BlockSpec((tm, d // 2), lambda i: (i, 0))],
        out_shape=[jax.ShapeDtypeStruct((t, d), F32), jax.ShapeDtypeStruct((t, d // 2), U32)],
        compiler_params=_cparams(1),
        name="mm_res_ln",
    )(a, w, x2, g.reshape(1, d), b.reshape(1, d))


def _t5_bucket(dist):
    max_exact = NUM_BUCKETS // 2
    d = jnp.maximum(dist, 1).astype(F32)
    large = max_exact + (jnp.log(d / max_exact) / math.log(MAX_DISTANCE / max_exact)
                         * (NUM_BUCKETS - max_exact)).astype(I32)
    return jnp.where(dist < max_exact, dist, jnp.minimum(large, NUM_BUCKETS - 1))


def _attn_bias(rel_bias, window, dil):
    steps = window // dil
    qi = jnp.arange(A_BLOCK)[:, None]
    ki = jnp.arange(2 * A_BLOCK)[None, :]
    rel = qi + A_BLOCK - ki
    valid = (rel >= 0) & (rel <= steps)
    bias = rel_bias[_t5_bucket(jnp.maximum(rel, 0) * dil)].astype(F32)
    bias = jnp.where(valid[:, :, None], bias, NEG)
    return bias.transpose(2, 0, 1)


def _attn_body(q_ref, k_ref, v_ref, kp_ref, vp_ref, bias_ref, o_ref, lse_ref, *, nblk):
    li = pl.program_id(2)
    lane = lax.broadcasted_iota(I32, (1, LANES), 1)
    col2 = lax.broadcasted_iota(I32, (1, 2 * A_BLOCK), 1)
    first_pen = jnp.where(col2 < A_BLOCK, jnp.where(li == 0, NEG, 0.0).astype(F32), 0.0)
    head0 = lane < A_HEAD_DIM
    lse_ref[...] = jnp.zeros_like(lse_ref)

    def hp_body(hp, carry):
        c0 = pl.multiple_of(hp * LANES, LANES)
        for j in range(nblk):
            rows = pl.ds(j * A_BLOCK, A_BLOCK)
            q = q_ref[rows, pl.ds(c0, LANES)]
            if j == 0:
                kb = jnp.concatenate([kp_ref[:, pl.ds(c0, LANES)], k_ref[rows, pl.ds(c0, LANES)]], axis=0)
                vb = jnp.concatenate([vp_ref[:, pl.ds(c0, LANES)], v_ref[rows, pl.ds(c0, LANES)]], axis=0)
            else:
                band = pl.ds((j - 1) * A_BLOCK, 2 * A_BLOCK)
                kb = k_ref[band, pl.ds(c0, LANES)]
                vb = v_ref[band, pl.ds(c0, LANES)]
            outs = []
            lses = []
            for e in range(2):
                qm = jnp.where(head0 if e == 0 else jnp.logical_not(head0), q, jnp.zeros_like(q))
                s = _dot_nt(qm, kb) + bias_ref[2 * hp + e]
                if j == 0:
                    s = s + first_pen
                m = jnp.max(s, axis=-1, keepdims=True)
                p = jnp.exp(s - m)
                l = jnp.sum(p, axis=-1, keepdims=True)
                outs.append(_dot(p.astype(BF16), vb) / l)
                lses.append(m + jnp.log(l))
            o_ref[rows, pl.ds(c0, LANES)] = jnp.where(head0, outs[0], outs[1]).astype(o_ref.dtype)
            cur = lse_ref[rows, :]
            cur = jnp.where(lane == 2 * hp, lses[0], cur)
            cur = jnp.where(lane == 2 * hp + 1, lses[1], cur)
            lse_ref[rows, :] = cur
        return carry

    lax.fori_loop(0, A_HEADS // 2, hp_body, 0)


def _proj_perm_body(x_ref, w_ref, o_ref, xb_ref, y_ref, *, dil):
    @pl.when(pl.program_id(1) == 0)
    def _():
        xb_ref[...] = x_ref[...].astype(BF16)

    y = _dot(xb_ref[...], w_ref[...])
    if dil == 1:
        o_ref[0] = y.astype(o_ref.dtype)
    else:
        nc = y.shape[1] // LANES
        for c in range(nc):
            y_ref[c] = y[:, c * LANES:(c + 1) * LANES]
        n = y.shape[0] // dil
        for r in range(dil):
            o_ref[r] = jnp.concatenate([y_ref[c, pl.ds(r, n, stride=dil), :] for c in range(nc)],
                                       axis=1).astype(o_ref.dtype)


def _proj_perm(x2, w, g, dil, bsz, seq, tm=1024, tn=1024):
    t, k = x2.shape
    hd = A_HEADS * A_HEAD_DIM
    tpb = seq // tm
    return pl.pallas_call(
        functools.partial(_proj_perm_body, dil=dil),
        grid=(t // tm, 3 * hd // tn),
        in_specs=[pl.BlockSpec((tm, k), lambda i, j: (i, 0)),
                  pl.BlockSpec((k, tn), lambda i, j: (0, 3 * g * (hd // tn) + j))],
        out_specs=pl.BlockSpec((None, dil, tm // dil, tn), lambda i, j: (i // tpb, 0, i % tpb, j)),
        out_shape=jax.ShapeDtypeStruct((bsz, dil, seq // dil, 3 * hd), BF16),
        scratch_shapes=[pltpu.VMEM((tm, k), BF16), pltpu.VMEM((tn // LANES, tm, LANES), F32)],
        compiler_params=_cparams(2),
        name=f"proj_d{dil}",
    )(x2, w)


def _dilated_group(qkv, dil, bias, bsz, seq):
    hd = A_HEADS * A_HEAD_DIM
    L = seq // dil
    tl = min(512, L)
    nblk = tl // A_BLOCK

    def blk(col):
        return pl.BlockSpec((None, None, tl, hd), lambda b, r, li: (b, r, li, col))

    def prev(col):
        return pl.BlockSpec((None, None, A_BLOCK, hd),
                            lambda b, r, li: (b, r, jnp.maximum(li * nblk - 1, 0), col))

    return pl.pallas_call(
        functools.partial(_attn_body, nblk=nblk),
        grid=(bsz, dil, L // tl),
        in_specs=[blk(0), blk(1), blk(2), prev(1), prev(2), _const_spec((A_HEADS, A_BLOCK, 2 * A_BLOCK))],
        out_specs=[pl.BlockSpec((None, None, tl, hd), lambda b, r, li: (b, r, li, 0)),
                   pl.BlockSpec((None, None, tl, LANES), lambda b, r, li: (b, r, li, 0))],
        out_shape=[jax.ShapeDtypeStruct((bsz, dil, L, hd), BF16),
                   jax.ShapeDtypeStruct((bsz, dil, L, LANES), F32)],
        compiler_params=_cparams(3),
        name=f"dilated_attn_d{dil}",
    )(qkv, qkv, qkv, qkv, qkv, bias)


def _attn_out_body(o1_ref, o2_ref, o3_ref, l1_ref, l2_ref, l3_ref, e_ref, w_ref, x_ref, g_ref, b_ref,
                   xo_ref, p_ref, so_ref, sl2_ref, sl3_ref, *, dils):
    def natural(ref, scr, dil):
        if dil == 1:
            return ref[0].astype(F32)
        n = ref.shape[1]
        nc = ref.shape[2] // LANES
        for r in range(dil):
            blk = ref[r].astype(F32)
            for c in range(nc):
                scr[c, pl.ds(r, n, stride=dil), :] = blk[:, c * LANES:(c + 1) * LANES]
        return jnp.concatenate([scr[c] for c in range(nc)], axis=1) if nc > 1 else scr[0]

    l1 = natural(l1_ref, None, dils[0])
    l2 = natural(l2_ref, sl2_ref, dils[1])
    l3 = natural(l3_ref, sl3_ref, dils[2])
    m = jnp.maximum(jnp.maximum(l1, l2), l3)
    e1, e2, e3 = jnp.exp(l1 - m), jnp.exp(l2 - m), jnp.exp(l3 - m)
    inv = 1.0 / (e1 + e2 + e3)
    ex = e_ref[...]

    def expand(wt):
        h = wt.astype(BF16)
        lo = (wt - h.astype(F32)).astype(BF16)
        return _dot(h, ex) + _dot(lo, ex)

    comb = expand(e1 * inv) * natural(o1_ref, None, dils[0])
    comb = comb + expand(e2 * inv) * natural(o2_ref, so_ref, dils[1])
    comb = comb + expand(e3 * inv) * natural(o3_ref, so_ref, dils[2])
    y = _dot(comb.astype(BF16), w_ref[...])
    xn = _ln_rows(ALPHA * x_ref[...] + y, g_ref[...], b_ref[...])
    xo_ref[...] = xn
    p_ref[...] = _pack_pairs(xn)


def _attn_out(os, lses, dils, w_out, x2, g, b, seq, tm=512):
    t, d = x2.shape
    tpb = seq // tm
    expand = np.zeros((LANES, d), np.float32)
    for h in range(A_HEADS):
        expand[h, h * A_HEAD_DIM:(h + 1) * A_HEAD_DIM] = 1.0
    row = lambda n: pl.BlockSpec((tm, n), lambda i: (i, 0))
    res = lambda dil, n: pl.BlockSpec((None, dil, tm // dil, n), lambda i: (i // tpb, 0, i % tpb, 0))
    return pl.pallas_call(
        functools.partial(_attn_out_body, dils=dils),
        grid=(t // tm,),
        in_specs=[res(dils[0], d), res(dils[1], d), res(dils[2], d),
                  res(dils[0], LANES), res(dils[1], LANES), res(dils[2], LANES),
                  _const_spec((LANES, d)), _const_spec((d, d)), row(d), _const_spec((1, d)), _const_spec((1, d))],
        out_specs=[row(d), row(d // 2)],
        out_shape=[jax.ShapeDtypeStruct((t, d), F32), jax.ShapeDtypeStruct((t, d // 2), U32)],
        scratch_shapes=[pltpu.VMEM((d // LANES, tm, LANES), F32), pltpu.VMEM((1, tm, LANES), F32),
                        pltpu.VMEM((1, tm, LANES), F32)],
        compiler_params=_cparams(1),
        name="attn_out",
    )(*os, *lses, jnp.asarray(expand, BF16), w_out, x2, g.reshape(1, d), b.reshape(1, d))


def _mixer_a(x2, w_in, w_out, rel_bias, g, b, bsz, seq):
    hd = A_HEADS * A_HEAD_DIM
    scale = np.ones((9 * hd,), np.float32)
    for gi in range(len(A_GROUPS)):
        scale[3 * gi * hd:(3 * gi + 1) * hd] = A_HEAD_DIM ** -0.5
    wb = (w_in * scale).astype(BF16)
    os, lses = [], []
    for gi, (window, dil) in enumerate(A_GROUPS):
        qkv = _proj_perm(x2, wb, gi, dil, bsz, seq)
        o, lse = _dilated_group(qkv, dil, _attn_bias(rel_bias, window, dil), bsz, seq)
        os.append(o)
        lses.append(lse)
    return _attn_out(os, lses, tuple(dl for _, dl in A_GROUPS), w_out.astype(BF16), x2, g, b, seq)


def _sgu_body(x_ref, wu_ref, wv_ref, ng_ref, nb_ref, wc_ref, bs_ref, a_ref, vb_ref, *, tm):
    xb = x_ref[...].astype(BF16)
    v = _gelu_tanh(_dot(xb, wv_ref[...]))
    vb_ref[...] = _ln_rows(v, ng_ref[...], nb_ref[...]).astype(BF16)
    gw = B_WIDTH // B_GROUPS
    ucols = 512
    for j in range(B_WIDTH // ucols):
        u = _gelu_tanh(_dot(xb, wu_ref[:, j * ucols:(j + 1) * ucols]))
        for c in range(tm // B_CHUNK):
            rows = slice(c * B_CHUNK, (c + 1) * B_CHUNK)
            for gg in range(ucols // gw):
                gi = j * (ucols // gw) + gg
                cols = slice(gi * gw, (gi + 1) * gw)
                f = _dot(wc_ref[gi], vb_ref[rows, cols]) + bs_ref[:, cols]
                a_ref[rows, cols] = (u[rows, gg * gw:(gg + 1) * gw] * f).astype(a_ref.dtype)


def _mixer_b(x2, w_in, norm_g, norm_b, w_s, b_s, w_out, g, b, tm=256):
    t, d = x2.shape
    wu = w_in[:, :B_WIDTH].astype(BF16)
    wv = w_in[:, B_WIDTH:].astype(BF16)
    wc = (w_s * jnp.tril(jnp.ones((B_CHUNK, B_CHUNK), w_s.dtype))).astype(BF16)
    bs_full = jnp.repeat(b_s.T, B_WIDTH // B_GROUPS, axis=1)
    a = pl.pallas_call(
        functools.partial(_sgu_body, tm=tm),
        grid=(t // tm,),
        in_specs=[pl.BlockSpec((tm, d), lambda i: (i, 0)), _const_spec((d, B_WIDTH)), _const_spec((d, B_WIDTH)),
                  _const_spec((1, B_WIDTH)), _const_spec((1, B_WIDTH)),
                  _const_spec((B_GROUPS, B_CHUNK, B_CHUNK)), _const_spec((B_CHUNK, B_WIDTH))],
        out_specs=pl.BlockSpec((tm, B_WIDTH), lambda i: (i, 0)),
        out_shape=jax.ShapeDtypeStruct((t, B_WIDTH), BF16),
        scratch_shapes=[pltpu.VMEM((tm, B_WIDTH), BF16)],
        compiler_params=_cparams(1),
        name="sgu",
    )(x2, wu, wv, norm_g.reshape(1, -1), norm_b.reshape(1, -1), wc, bs_full)
    return _mm_res_ln(a, w_out.astype(BF16), x2, g, b)


def _delta_body(qkvz_ref, gates_ref, cw_ref, alog_ref, dtb_ref, nw_ref, o_ref, ext_ref, state_ref, vnew_ref):
    ts = C_SUPER
    hd = C_HEADS * C_HEAD_DIM
    dk = C_HEAD_DIM
    nch = ts // C_CHUNK
    pr = 2 * C_CHUNK

    @pl.when(pl.program_id(1) == 0)
    def _():
        ext_ref[0:8, :] = jnp.zeros((8, 3 * hd), F32)
        state_ref[...] = jnp.zeros_like(state_ref)

    ext_ref[8:8 + ts, :] = qkvz_ref[:, 0:3 * hd].astype(F32)

    r_i = lax.broadcasted_iota(I32, (ts, ts), 0)
    c_i = lax.broadcasted_iota(I32, (ts, ts), 1)
    same = (r_i >> 6) == (c_i >> 6)
    u_cum = jnp.where(jnp.logical_and(same, r_i <= c_i), 1.0, 0.0).astype(BF16)
    u_tot = jnp.where(same, 1.0, 0.0).astype(BF16)
    r_p = lax.broadcasted_iota(I32, (pr, pr), 0)
    c_p = lax.broadcasted_iota(I32, (pr, pr), 1)
    same_p = (r_p >> 6) == (c_p >> 6)
    le = jnp.logical_and(same_p, c_p <= r_p)
    strict = jnp.logical_and(same_p, c_p < r_p)
    eye = jnp.where(r_p == c_p, 1.0, 0.0).astype(F32)

    gt = gates_ref[...].T
    zt = gt + dtb_ref[...]
    g_rows = -jnp.exp(alog_ref[...]) * (jnp.maximum(zt, 0.0) + jnp.log(1.0 + jnp.exp(-jnp.abs(zt))))
    gh, gm, gl = _split3(g_rows)
    gcum_rows = _dot(gh, u_cum) + _dot(gm, u_cum) + _dot(gl, u_cum)
    gtot_rows = _dot(gh, u_tot) + _dot(gm, u_tot) + _dot(gl, u_tot)
    gcum_cols = gcum_rows.T
    gtot_cols = gtot_rows.T
    beta_cols = _sigmoid(gates_ref[...])
    lane = lax.broadcasted_iota(I32, (1, LANES), 1)
    sub = lax.broadcasted_iota(I32, (LANES, 1), 0)
    csel = lax.broadcasted_iota(I32, (1, pr), 1) >> 6

    def conv_silu(c0):
        cols = pl.ds(pl.multiple_of(c0, dk), dk)
        y = (cw_ref[0:1, cols] * ext_ref[pl.ds(5, ts), cols] + cw_ref[1:2, cols] * ext_ref[pl.ds(6, ts), cols]
             + cw_ref[2:3, cols] * ext_ref[pl.ds(7, ts), cols] + cw_ref[3:4, cols] * ext_ref[pl.ds(8, ts), cols])
        return y * _sigmoid(y)

    def one_head(h, slot):
        c0 = pl.multiple_of(h * dk, dk)
        pick_a = lane == (C_HEADS + h)
        gcol = jnp.sum(jnp.where(pick_a, gcum_cols, 0.0), axis=1, keepdims=True)
        glcol = jnp.sum(jnp.where(pick_a, gtot_cols, 0.0), axis=1, keepdims=True)
        bcol = jnp.sum(jnp.where(lane == h, beta_cols, 0.0), axis=1, keepdims=True)
        pick_r = sub == (C_HEADS + h)
        grow = jnp.sum(jnp.where(pick_r, gcum_rows, 0.0), axis=0, keepdims=True)
        glrow = jnp.sum(jnp.where(pick_r, gtot_rows, 0.0), axis=0, keepdims=True)

        q = conv_silu(c0)
        k = conv_silu(c0 + hd)
        v = conv_silu(c0 + 2 * hd)
        q = q * lax.rsqrt(jnp.sum(q * q, axis=-1, keepdims=True) + RMS_EPS) * (dk ** -0.5)
        k = k * lax.rsqrt(jnp.sum(k * k, axis=-1, keepdims=True) + RMS_EPS)
        eg = jnp.exp(gcol)
        kb = k * bcol
        kbf = k.astype(BF16)
        kbb = kb.astype(BF16)
        qbf = q.astype(BF16)
        rhs = jnp.concatenate([v * bcol, kb * eg], axis=1).astype(BF16)
        qe = q * eg
        ktil_t = (k * jnp.exp(glcol - gcol)).T.astype(BF16)

        intra, wv_, kc = [], [], []
        for p in range(ts // pr):
            rp = slice(p * pr, (p + 1) * pr)
            decay = jnp.exp(jnp.where(le, gcol[rp] - grow[:, rp], NEG))
            lower = jnp.where(strict, _dot_nt(kbb[rp], kbf[rp]) * decay, 0.0)
            intra.append((_dot_nt(qbf[rp], kbf[rp]) * decay).astype(BF16))
            mpow = -lower
            tinv = eye + mpow
            for _ in range(5):
                mb = mpow.astype(BF16)
                mpow = _dot(mb, mb)
                tinv = tinv + _dot(tinv.astype(BF16), mpow.astype(BF16))
            wk = _dot(tinv.astype(BF16), rhs[rp])
            wv_.append(wk[:, :dk])
            kc.append(wk[:, dk:])

        vnew_ref[slot] = jnp.zeros(vnew_ref.shape[1:], vnew_ref.dtype)
        s = state_ref[h]
        outs = []
        for j in range(nch):
            p, jj = j // 2, j % 2
            rows = slice(j * C_CHUNK, (j + 1) * C_CHUNK)
            lrows = slice(jj * C_CHUNK, (jj + 1) * C_CHUNK)
            rp = slice(p * pr, (p + 1) * pr)
            a1 = _dot(jnp.concatenate([kc[p][lrows], qe[rows]], axis=0).astype(BF16), s.astype(BF16))
            vnew_ref[slot, rows, :] = (wv_[p][lrows] - a1[:C_CHUNK]).astype(BF16)
            vn_pair = vnew_ref[slot, rp, :]
            outs.append(a1[C_CHUNK:] + _dot(intra[p][lrows, :], vn_pair))
            kt_j = jnp.where(csel == jj, ktil_t[:, rp], jnp.zeros((), BF16))
            dg = jnp.exp(jnp.sum(jnp.where(csel == jj, glrow[:, rp], 0.0), axis=1, keepdims=True) * (1.0 / C_CHUNK))
            s = s * dg + _dot(kt_j, vn_pair)
        state_ref[h] = s
        o = jnp.concatenate(outs, axis=0)
        o = o * lax.rsqrt(jnp.mean(o * o, axis=-1, keepdims=True) + RMS_EPS) * nw_ref[...]
        z = qkvz_ref[:, pl.ds(pl.multiple_of(c0 + 3 * hd, dk), dk)].astype(F32)
        o_ref[:, pl.ds(c0, dk)] = (o * (z * _sigmoid(z))).astype(o_ref.dtype)

    def head_pair(hh, carry):
        one_head(2 * hh, 0)
        one_head(2 * hh + 1, 1)
        return carry

    lax.fori_loop(0, C_HEADS // 2, head_pair, 0)
    ext_ref[0:8, :] = ext_ref[ts:ts + 8, :]


def _mixer_c(x2, w_in, conv_w, a_log, dt_bias, norm_w, w_out, g, b, bsz, seq):
    t, d = x2.shape
    hd = C_HEADS * C_HEAD_DIM
    qkvz = _proj(x2, w_in[:, :4 * hd].astype(BF16), BF16)
    wg = jnp.zeros((d, LANES), F32).at[:, :2 * C_HEADS].set(w_in[:, 4 * hd:])
    gates = _proj3(x2, wg)
    col = lambda v: jnp.zeros((LANES, 1), F32).at[C_HEADS:2 * C_HEADS, 0].set(v.astype(F32))
    nsteps = seq // C_SUPER
    o = pl.pallas_call(
        _delta_body,
        grid=(bsz, nsteps),
        in_specs=[pl.BlockSpec((C_SUPER, 4 * hd), lambda bi, i: (bi * nsteps + i, 0)),
                  pl.BlockSpec((C_SUPER, LANES), lambda bi, i: (bi * nsteps + i, 0)),
                  _const_spec((C_CONV, 3 * hd)), _const_spec((LANES, 1)), _const_spec((LANES, 1)),
                  _const_spec((1, C_HEAD_DIM))],
        out_specs=pl.BlockSpec((C_SUPER, hd), lambda bi, i: (bi * nsteps + i, 0)),
        out_shape=jax.ShapeDtypeStruct((t, hd), BF16),
        scratch_shapes=[pltpu.VMEM((C_SUPER + 8, 3 * hd), F32),
                        pltpu.VMEM((C_HEADS, C_HEAD_DIM, C_HEAD_DIM), F32),
                        pltpu.VMEM((2, C_SUPER, C_HEAD_DIM), BF16)],
        compiler_params=_cparams(2),
        name="deltanet",
    )(qkvz, gates, conv_w.astype(F32), col(a_log), col(dt_bias), norm_w.reshape(1, -1).astype(F32))
    return _mm_res_ln(o, w_out.astype(BF16), x2, g, b)


def _route_body(x_ref, wh_ref, wl_ref, o_ref, g1_ref, g2_ref, cnt_ref, run_ref, *, tm):
    @pl.when(pl.program_id(0) == 0)
    def _():
        run_ref[...] = jnp.zeros_like(run_ref)

    x = x_ref[...]
    xh = x.astype(BF16)
    xl = (x - xh.astype(F32)).astype(BF16)
    wh = wh_ref[...]
    logits = _dot(xh, wh) + _dot(xl, wh) + _dot(xh, wl_ref[...])
    lane = lax.broadcasted_iota(I32, (1, LANES), 1)
    lane_f = lane.astype(F32)

    def top1(vals):
        m = jnp.max(vals, axis=-1, keepdims=True)
        idx = jnp.min(jnp.where(vals == m, lane_f, 1e9), axis=-1, keepdims=True)
        return m, idx.astype(I32)

    lc = jnp.where(lane < MOE_GROUPS, logits, NEG)
    mc, grp = top1(lc)
    p_grp = 1.0 / jnp.sum(jnp.exp(lc - mc), axis=-1, keepdims=True)
    lo = MOE_GROUPS + MOE_PER_GROUP * grp
    lf = jnp.where(jnp.logical_and(lane >= lo, lane < lo + MOE_PER_GROUP), logits, NEG)
    m1, i1 = top1(lf)
    m2, i2 = top1(jnp.where(lane == i1, NEG, lf))
    e21 = jnp.exp(m2 - m1)
    g1 = p_grp / (1.0 + e21)
    g2 = p_grp * e21 / (1.0 + e21)
    ex1 = i1 - MOE_GROUPS
    ex2 = i2 - MOE_GROUPS

    oh1 = jnp.where(lane == ex1, 1.0, 0.0).astype(F32)
    oh2 = jnp.where(lane == ex2, 1.0, 0.0).astype(F32)
    ohs = oh1 + oh2
    r_i = lax.broadcasted_iota(I32, (tm, tm), 0)
    c_i = lax.broadcasted_iota(I32, (tm, tm), 1)
    before = _dot(jnp.where(c_i < r_i, 1.0, 0.0).astype(BF16), ohs.astype(BF16)) + run_ref[...]
    rank1 = jnp.sum(oh1 * before, axis=-1, keepdims=True)
    rank2 = jnp.sum(oh2 * before, axis=-1, keepdims=True)
    run_ref[...] = run_ref[...] + jnp.sum(ohs, axis=0, keepdims=True)

    out = jnp.where(lane == 0, ex1.astype(F32), 0.0)
    out = jnp.where(lane == 1, ex2.astype(F32), out)
    out = jnp.where(lane == 2, g1, out)
    out = jnp.where(lane == 3, g2, out)
    out = jnp.where(lane == 4, rank1, out)
    out = jnp.where(lane == 5, rank2, out)
    o_ref[...] = out
    g1_ref[...] = jnp.broadcast_to(g1, g1_ref.shape)
    g2_ref[...] = jnp.broadcast_to(g2, g2_ref.shape)
    cnt_ref[...] = jnp.broadcast_to(run_ref[...], cnt_ref.shape)


def _route(x2, w_coarse, w_fine, tm=512):
    t, d = x2.shape
    wr = jnp.zeros((d, LANES), F32)
    wr = wr.at[:, :MOE_GROUPS].set(w_coarse).at[:, MOE_GROUPS:MOE_GROUPS + MOE_EXPERTS].set(
        w_fine.reshape(d, MOE_EXPERTS))
    wh = wr.astype(BF16)
    wl = (wr - wh.astype(F32)).astype(BF16)
    return pl.pallas_call(
        functools.partial(_route_body, tm=tm),
        grid=(t // tm,),
        in_specs=[pl.BlockSpec((tm, d), lambda i: (i, 0)), _const_spec((d, LANES)), _const_spec((d, LANES))],
        out_specs=[pl.BlockSpec((tm, LANES), lambda i: (i, 0)), pl.BlockSpec((tm, LANES), lambda i: (i, 0)),
                   pl.BlockSpec((tm, LANES), lambda i: (i, 0)), pl.BlockSpec((8, LANES), lambda i: (0, 0))],
        out_shape=[jax.ShapeDtypeStruct((t, LANES), F32), jax.ShapeDtypeStruct((t, LANES), F32),
                   jax.ShapeDtypeStruct((t, LANES), F32), jax.ShapeDtypeStruct((8, LANES), F32)],
        scratch_shapes=[pltpu.VMEM((1, LANES), F32)],
        compiler_params=_cparams(1),
        name="moe_route",
    )(x2, wh, wl)


def _dispatch_body(p1_ref, p2_ref, x_ref, o_hbm, slab_ref, sem, *, tm):
    s = pl.program_id(0)
    i = pl.program_id(1)

    @pl.when(i == 0)
    def _():
        slab_ref[...] = jnp.zeros_like(slab_ref)

    def row(r, carry):
        v = x_ref[pl.ds(r, 1), :]
        slab_ref[pl.ds(p1_ref[0, r], 1), :] = v
        slab_ref[pl.ds(p2_ref[0, r], 1), :] = v
        return carry
    lax.fori_loop(0, tm, row, 0, unroll=8)

    @pl.when(i == pl.num_programs(1) - 1)
    def _():
        cp = pltpu.make_async_copy(slab_ref, o_hbm.at[s], sem)
        cp.start()
        cp.wait()


def _dispatch(xp, pos1, pos2, s_pad, tm=1024):
    t, w = xp.shape
    nslab = w // LANES
    nt = t // tm
    return pl.pallas_call(
        functools.partial(_dispatch_body, tm=tm),
        grid=(nslab, nt),
        in_specs=[pl.BlockSpec((None, 1, tm), lambda s, i: (i, 0, 0), memory_space=pltpu.SMEM),
                  pl.BlockSpec((None, 1, tm), lambda s, i: (i, 0, 0), memory_space=pltpu.SMEM),
                  pl.BlockSpec((tm, LANES), lambda s, i: (i, s))],
        out_specs=pl.BlockSpec(memory_space=pl.ANY),
        out_shape=jax.ShapeDtypeStruct((nslab, s_pad, LANES), U32),
        scratch_shapes=[pltpu.VMEM((s_pad, LANES), U32), pltpu.SemaphoreType.DMA(())],
        compiler_params=_cparams(2),
        name="moe_dispatch",
    )(pos1.reshape(nt, 1, tm), pos2.reshape(nt, 1, tm), xp)


def _expert_body(te_ref, nv_ref, xs_ref, wg_ref, wu_ref, wd_ref, y_ref, wgb_ref, wub_ref, wdb_ref):
    i = pl.program_id(0)
    half = D_MODEL // 2
    nslab = half // LANES

    @pl.when(i < nv_ref[0])
    def _():
        changed = jnp.logical_or(i == 0, te_ref[i] != te_ref[jnp.maximum(i - 1, 0)])

        @pl.when(changed)
        def _():
            wgb_ref[...] = wg_ref[...].astype(BF16)
            wub_ref[...] = wu_ref[...].astype(BF16)
            wdb_ref[...] = wd_ref[...].astype(BF16)

        xlo = jnp.concatenate([_unpack_lo(xs_ref[s]) for s in range(nslab)], axis=1).astype(BF16)
        xhi = jnp.concatenate([_unpack_hi(xs_ref[s]) for s in range(nslab)], axis=1).astype(BF16)
        hg =_dot(xlo, wgb_ref[0:half, :]) + _dot(xhi, wgb_ref[half:, :])
        hu = _dot(xlo, wub_ref[0:half, :]) + _dot(xhi, wub_ref[half:, :])
        h = (hg * _sigmoid(hg) * hu).astype(BF16)
        packed = _pack_pairs(_dot(h, wdb_ref[...]))
        for s in range(nslab):
            y_ref[s] = packed[:, s * LANES:(s + 1) * LANES]

    @pl.when(i >= nv_ref[0])
    def _():
        y_ref[...] = jnp.zeros_like(y_ref)


def _experts(xs, tile_expert, nvalid, w_gate, w_up, w_down, n_tiles):
    tm = MOE_TM
    d, hdn = D_MODEL, MOE_HIDDEN
    nslab = d // 2 // LANES
    return pl.pallas_call(
        _expert_body,
        grid_spec=pltpu.PrefetchScalarGridSpec(
            num_scalar_prefetch=2,
            grid=(n_tiles,),
            in_specs=[pl.BlockSpec((nslab, tm, LANES), lambda i, te, nv: (0, i, 0)),
                      pl.BlockSpec((None, d, hdn), lambda i, te, nv: (te[i], 0, 0)),
                      pl.BlockSpec((None, d, hdn), lambda i, te, nv: (te[i], 0, 0)),
                      pl.BlockSpec((None, hdn, d), lambda i, te, nv: (te[i], 0, 0))],
            out_specs=pl.BlockSpec((nslab, tm, LANES), lambda i, te, nv: (0, i, 0)),
            scratch_shapes=[pltpu.VMEM((d, hdn), BF16), pltpu.VMEM((d, hdn), BF16), pltpu.VMEM((hdn, d), BF16)],
        ),
        out_shape=jax.ShapeDtypeStruct((nslab, n_tiles * tm, LANES), U32),
        compiler_params=_cparams(1),
        name="moe_experts",
    )(tile_expert, nvalid, xs, w_gate, w_up, w_down)


def _combine_body(p1_ref, p2_ref, y_ref, g1_ref, g2_ref, lo_ref, hi_ref, b1_ref, b2_ref, *, tm):
    def row(r, carry):
        b1_ref[pl.ds(r, 1), :] = y_ref[pl.ds(p1_ref[0, r], 1), :]
        b2_ref[pl.ds(r, 1), :] = y_ref[pl.ds(p2_ref[0, r], 1), :]
        return carry
    lax.fori_loop(0, tm, row, 0, unroll=8)
    g1 = g1_ref[...]
    g2 = g2_ref[...]
    y1 = b1_ref[...]
    y2 = b2_ref[...]
    lo_ref[...] = g1 * _unpack_lo(y1) + g2 * _unpack_lo(y2)
    hi_ref[...] = g1 * _unpack_hi(y1) + g2 * _unpack_hi(y2)


def _combine(y, pos1, pos2, g1b, g2b, tm=1024):
    nslab, s_pad, _ = y.shape
    t = pos1.shape[0]
    nt = t // tm
    return pl.pallas_call(
        functools.partial(_combine_body, tm=tm),
        grid=(nslab, nt),
        in_specs=[pl.BlockSpec((None, 1, tm), lambda s, i: (i, 0, 0), memory_space=pltpu.SMEM),
                  pl.BlockSpec((None, 1, tm), lambda s, i: (i, 0, 0), memory_space=pltpu.SMEM),
                  pl.BlockSpec((None, s_pad, LANES), lambda s, i: (s, 0, 0), pipeline_mode=pl.Buffered(1)),
                  pl.BlockSpec((tm, LANES), lambda s, i: (i, 0)), pl.BlockSpec((tm, LANES), lambda s, i: (i, 0))],
        out_specs=[pl.BlockSpec((tm, LANES), lambda s, i: (i, s)), pl.BlockSpec((tm, LANES), lambda s, i: (i, s))],
        out_shape=[jax.ShapeDtypeStruct((t, nslab * LANES), F32), jax.ShapeDtypeStruct((t, nslab * LANES), F32)],
        scratch_shapes=[pltpu.VMEM((tm, LANES), U32), pltpu.VMEM((tm, LANES), U32)],
        compiler_params=_cparams(2),
        name="moe_combine",
    )(pos1.reshape(nt, 1, tm), pos2.reshape(nt, 1, tm), y, g1b, g2b)


def _res_ln_body(x_ref, lo_ref, hi_ref, g_ref, b_ref, o_ref):
    h = jnp.concatenate([lo_ref[...], hi_ref[...]], axis=1)
    o_ref[...] = _ln_rows(ALPHA * x_ref[...] + h, g_ref[...], b_ref[...])


def _res_ln(x2, lo, hi, g, b, tm=1024):
    t, d = x2.shape
    return pl.pallas_call(
        _res_ln_body,
        grid=(t // tm,),
        in_specs=[pl.BlockSpec((tm, d), lambda i: (i, 0)), pl.BlockSpec((tm, d // 2), lambda i: (i, 0)),
                  pl.BlockSpec((tm, d // 2), lambda i: (i, 0)), _const_spec((1, d)), _const_spec((1, d))],
        out_specs=pl.BlockSpec((tm, d), lambda i: (i, 0)),
        out_shape=jax.ShapeDtypeStruct((t, d), F32),
        compiler_params=_cparams(1),
        name="res_ln",
    )(x2, lo, hi, g.reshape(1, d), b.reshape(1, d))


def _moe(x2, xp, w_coarse, w_fine, w_gate, w_up, w_down, g, b):
    t = x2.shape[0]
    tm = MOE_TM
    n_tiles = (2 * t) // tm + MOE_EXPERTS
    route, g1b, g2b, cnt = _route(x2, w_coarse, w_fine)
    ex = route[:, 0:2].astype(I32)
    rank = route[:, 4:6].astype(I32)
    counts = cnt[0, :MOE_EXPERTS].astype(I32)
    ptiles = (counts + tm - 1) // tm
    tile_end = jnp.cumsum(ptiles)
    pstart = (tile_end - ptiles) * tm
    eids = jnp.arange(MOE_EXPERTS, dtype=I32)
    pos = jnp.sum(jnp.where(ex[:, :, None] == eids, pstart, 0), axis=-1) + rank
    nvalid = tile_end[-1:].astype(I32)
    tidx = jnp.arange(n_tiles, dtype=I32)
    te = jnp.sum((tile_end[None, :] <= jnp.minimum(tidx, nvalid[0] - 1)[:, None]).astype(I32), axis=1)
    xs = _dispatch(xp, pos[:, 0], pos[:, 1], n_tiles * tm)
    y = _experts(xs, te, nvalid, w_gate, w_up, w_down, n_tiles)
    lo, hi = _combine(y, pos[:, 0], pos[:, 1], g1b, g2b)
    return _res_ln(x2, lo, hi, g, b)


def kernel(x, rel_bias, a_w_in, a_w_out, b_w_in, b_norm_g, b_norm_b, b_w_s, b_b_s, b_w_out, c_w_in, c_conv,
           c_a_log, c_dt_bias, c_norm_w, c_w_out, ln_g, ln_b, moe_w_coarse, moe_w_fine, moe_w_gate, moe_w_up,
           moe_w_down):
    bsz, seq, d = x.shape
    x2 = x.reshape(bsz * seq, d)
    for i in range(DEPTH):
        kind, j = i % 3, i // 3
        g1, b1 = ln_g[i, 0], ln_b[i, 0]
        if kind == 0:
            x2, xp = _mixer_a(x2, a_w_in[j], a_w_out[j], rel_bias, g1, b1, bsz, seq)
        elif kind == 1:
            x2, xp = _mixer_b(x2, b_w_in[j], b_norm_g[j], b_norm_b[j], b_w_s[j], b_b_s[j], b_w_out[j], g1, b1)
        else:
            x2, xp = _mixer_c(x2, c_w_in[j], c_conv[j], c_a_log[j], c_dt_bias[j], c_norm_w[j], c_w_out[j],
                              g1, b1, bsz, seq)
        x2 = _moe(x2, xp, moe_w_coarse[i], moe_w_fine[i], moe_w_gate[i], moe_w_up[i], moe_w_down[i],
                  ln_g[i, 1], ln_b[i, 1])
    return x2.reshape(bsz, seq, d)
```

```python
import functools
import math

import numpy as np
import jax
import jax.numpy as jnp
from jax import lax
from jax.experimental import pallas as pl
from jax.experimental.pallas import tpu as pltpu

F32 = jnp.float32
BF16 = jnp.bfloat16
U32 = jnp.uint32
I32 = jnp.int32

D_MODEL = 1024
DEPTH = 4
A_GROUPS = ((128, 1), (512, 4), (2048, 16))
A_HEADS = 16
A_HEAD_DIM = 64
A_BLOCK = 128
NUM_BUCKETS = 32
MAX_DISTANCE = 2048
B_CHUNK = 128
B_WIDTH = 2 * D_MODEL
B_GROUPS = 16
C_HEADS = 8
C_HEAD_DIM = 128
C_CONV = 4
C_CHUNK = 64
MOE_GROUPS = 8
MOE_PER_GROUP = 8
MOE_EXPERTS = 64
MOE_HIDDEN = 512
LN_EPS = 1e-5
RMS_EPS = 1e-6
ALPHA = (2 * DEPTH) ** 0.25

LANES = 128
NEG = -1e30
VMEM_LIMIT = 56 * 1024 * 1024
MOE_TM = 256
C_SUPER = 256
C_GROUP = 4


def _cparams(n_axes, vmem=VMEM_LIMIT):
    return pltpu.CompilerParams(dimension_semantics=("arbitrary",) * n_axes, vmem_limit_bytes=vmem)


def _const_spec(shape):
    nd = len(shape)
    return pl.BlockSpec(shape, lambda *_: (0,) * nd, pipeline_mode=pl.Buffered(1))


def _ln_rows(y, g, b):
    mu = jnp.mean(y, axis=-1, keepdims=True)
    yc = y - mu
    var = jnp.mean(yc * yc, axis=-1, keepdims=True)
    return yc * lax.rsqrt(var + LN_EPS) * g + b


def _pack_pairs(y):
    w = y.shape[1] // 2
    lo = lax.bitcast_convert_type(y[:, :w].astype(BF16).astype(F32), U32)
    hi = lax.bitcast_convert_type(y[:, w:].astype(BF16).astype(F32), U32)
    return (lo >> 16) | (hi & jnp.uint32(0xFFFF0000))


def _unpack_lo(p):
    return lax.bitcast_convert_type(p << 16, F32)


def _unpack_hi(p):
    return lax.bitcast_convert_type(p & jnp.uint32(0xFFFF0000), F32)


def _split3(a):
    h = a.astype(BF16)
    r = a - h.astype(F32)
    m = r.astype(BF16)
    l = (r - m.astype(F32)).astype(BF16)
    return h, m, l


def _dot(a, b):
    return jnp.dot(a, b, preferred_element_type=F32)


def _dot_nt(a, b):
    return lax.dot_general(a, b, (((1,), (1,)), ((), ())), preferred_element_type=F32)


def _gelu_tanh(x):
    return 0.5 * x * (1.0 + jnp.tanh(0.7978845608028654 * (x + 0.044715 * (x * x * x))))


def _sigmoid(x):
    return 1.0 / (1.0 + jnp.exp(-x))


def _proj_body(x_ref, w_ref, o_ref, xb_ref):
    @pl.when(pl.program_id(1) == 0)
    def _():
        xb_ref[...] = x_ref[...].astype(BF16)

    o_ref[...] = _dot(xb_ref[...], w_ref[...]).astype(o_ref.dtype)


def _proj(x2, w, out_dtype, tm=1024, tn=1024):
    t, k = x2.shape
    n = w.shape[1]
    tn = min(tn, n)
    return pl.pallas_call(
        _proj_body,
        grid=(t // tm, n // tn),
        in_specs=[pl.BlockSpec((tm, k), lambda i, j: (i, 0)),
                  pl.BlockSpec((k, tn), lambda i, j: (0, j))],
        out_specs=pl.BlockSpec((tm, tn), lambda i, j: (i, j)),
        out_shape=jax.ShapeDtypeStruct((t, n), out_dtype),
        scratch_shapes=[pltpu.VMEM((tm, k), BF16)],
        compiler_params=_cparams(2),
        name="proj",
    )(x2, w)


def _proj3_body(x_ref, wh_ref, wl_ref, o_ref):
    x = x_ref[...]
    xh = x.astype(BF16)
    xl = (x - xh.astype(F32)).astype(BF16)
    wh = wh_ref[...]
    o_ref[...] = _dot(xh, wh) + _dot(xl, wh) + _dot(xh, wl_ref[...])


def _proj3(x2, w, tm=1024):
    t, k = x2.shape
    n = w.shape[1]
    wh = w.astype(BF16)
    wl = (w - wh.astype(F32)).astype(BF16)
    return pl.pallas_call(
        _proj3_body,
        grid=(t // tm,),
        in_specs=[pl.BlockSpec((tm, k), lambda i: (i, 0)), _const_spec((k, n)), _const_spec((k, n))],
        out_specs=pl.BlockSpec((tm, n), lambda i: (i, 0)),
        out_shape=jax.ShapeDtypeStruct((t, n), F32),
        compiler_params=_cparams(1),
        name="proj3",
    )(x2, wh, wl)


def _mm_res_ln_body(a_ref, w_ref, x_ref, g_ref, b_ref, o_ref, p_ref):
    y = _dot(a_ref[...], w_ref[...])
    xn = _ln_rows(ALPHA * x_ref[...] + y, g_ref[...], b_ref[...])
    o_ref[...] = xn
    p_ref[...] = _pack_pairs(xn)


def _mm_res_ln(a, w, x2, g, b, tm=512):
    t, k = a.shape
    d = w.shape[1]
    return pl.pallas_call(
        _mm_res_ln_body,
        grid=(t // tm,),
        in_specs=[pl.BlockSpec((tm, k), lambda i: (i, 0)), _const_spec((k, d)),
                  pl.BlockSpec((tm, d), lambda i: (i, 0)), _const_spec((1, d)), _const_spec((1, d))],
        out_specs=[pl.BlockSpec((tm, d), lambda i: (i, 0)), pl.BlockSpec((tm, d // 2), lambda i: (i, 0))],
        out_shape=[jax.ShapeDtypeStruct((t, d), F32), jax.ShapeDtypeStruct((t, d // 2), U32)],
        compiler_params=_cparams(1),
        name="mm_res_ln",
    )(a, w, x2, g.reshape(1, d), b.reshape(1, d))


def _t5_bucket(dist):
    max_exact = NUM_BUCKETS // 2
    d = jnp.maximum(dist, 1).astype(F32)
    large = max_exact + (jnp.log(d / max_exact) / math.log(MAX_DISTANCE / max_exact)
                         * (NUM_BUCKETS - max_exact)).astype(I32)
    return jnp.where(dist < max_exact, dist, jnp.minimum(large, NUM_BUCKETS - 1))


def _attn_bias(rel_bias, window, dil):
    steps = window // dil
    qi = jnp.arange(A_BLOCK)[:, None]
    ki = jnp.arange(2 * A_BLOCK)[None, :]
    rel = qi + A_BLOCK - ki
    valid = (rel >= 0) & (rel <= steps)
    bucket = _t5_bucket(jnp.maximum(rel, 0) * dil)
    bias = jnp.zeros((A_HEADS, A_BLOCK, 2 * A_BLOCK), F32)
    for bkt in range(NUM_BUCKETS):
        bias = jnp.where((bucket == bkt)[None], rel_bias[bkt].astype(F32)[:, None, None], bias)
    return jnp.where(valid[None], bias, NEG)


def _attn_body(q_ref, k_ref, v_ref, kp_ref, vp_ref, bias_ref, o_ref, lse_ref, *, nres, nblk):
    li = pl.program_id(2)
    lane = lax.broadcasted_iota(I32, (1, LANES), 1)
    col2 = lax.broadcasted_iota(I32, (1, 2 * A_BLOCK), 1)
    first_pen = jnp.where(col2 < A_BLOCK, jnp.where(li == 0, NEG, 0.0).astype(F32), 0.0)
    head0 = lane < A_HEAD_DIM
    lse_ref[...] = jnp.zeros_like(lse_ref)

    def hp_body(hp, carry):
        c0 = pl.multiple_of(hp * LANES, LANES)
        cols = pl.ds(c0, LANES)
        blocks = [(r, j) for r in range(nres) for j in range(nblk)]
        units = [(r, j, e) for (r, j) in blocks for e in range(2)]
        vbs, s_ = {}, {}
        for (r, j) in blocks:
            rows = pl.ds(j * A_BLOCK, A_BLOCK)
            q = q_ref[r, rows, cols]
            if j == 0:
                kb = jnp.concatenate([kp_ref[r, :, cols], k_ref[r, rows, cols]], axis=0)
                vbs[r, j] = jnp.concatenate([vp_ref[r, :, cols], v_ref[r, rows, cols]], axis=0)
            else:
                band = pl.ds((j - 1) * A_BLOCK, 2 * A_BLOCK)
                kb = k_ref[r, band, cols]
                vbs[r, j] = v_ref[r, band, cols]
            for e in range(2):
                qm = jnp.where(head0 if e == 0 else jnp.logical_not(head0), q, jnp.zeros_like(q))
                s = _dot_nt(qm, kb) + bias_ref[2 * hp + e]
                s_[r, j, e] = s + first_pen if j == 0 else s
        p_, l_, lse_ = {}, {}, {}
        for u in units:
            m = jnp.max(s_[u], axis=-1, keepdims=True)
            p = jnp.exp(s_[u] - m)
            l_[u] = jnp.sum(p, axis=-1, keepdims=True)
            p_[u] = p.astype(BF16)
            lse_[u] = m + jnp.log(l_[u])
        pv = {u: _dot(p_[u], vbs[u[0], u[1]]) for u in units}
        for (r, j) in blocks:
            rows = pl.ds(j * A_BLOCK, A_BLOCK)
            o_ref[r, rows, cols] = jnp.where(head0, pv[r, j, 0] / l_[r, j, 0],
                                             pv[r, j, 1] / l_[r, j, 1]).astype(o_ref.dtype)
            cur = lse_ref[r, rows, :]
            cur = jnp.where(lane == 2 * hp, lse_[r, j, 0], cur)
            cur = jnp.where(lane == 2 * hp + 1, lse_[r, j, 1], cur)
            lse_ref[r, rows, :] = cur
        return carry

    lax.fori_loop(0, A_HEADS // 2, hp_body, 0)


def _proj_perm_body(x_ref, w_ref, o_ref, xb_ref, y_ref, *, dil):
    tm, k = x_ref.shape
    n = tm // dil

    @pl.when(pl.program_id(1) == 0)
    def _():
        if dil == 1:
            xb_ref[...] = x_ref[...].astype(BF16)
        else:
            nc = k // LANES
            for c in range(nc):
                y_ref[c] = x_ref[:, c * LANES:(c + 1) * LANES]
            for r in range(dil):
                xb_ref[r * n:(r + 1) * n, :] = jnp.concatenate(
                    [y_ref[c, pl.ds(r, n, stride=dil), :] for c in range(nc)], axis=1).astype(BF16)

    y = _dot(xb_ref[...], w_ref[...])
    for r in range(dil):
        o_ref[r] = y[r * n:(r + 1) * n].astype(o_ref.dtype)


def _proj_perm(x2, w, g, dil, bsz, seq, tm=1024, tn=1024):
    t, k = x2.shape
    hd = A_HEADS * A_HEAD_DIM
    tpb = seq // tm
    return pl.pallas_call(
        functools.partial(_proj_perm_body, dil=dil),
        grid=(t // tm, 3 * hd // tn),
        in_specs=[pl.BlockSpec((tm, k), lambda i, j: (i, 0)),
                  pl.BlockSpec((k, tn), lambda i, j: (0, 3 * g * (hd // tn) + j))],
        out_specs=pl.BlockSpec((None, dil, tm // dil, tn), lambda i, j: (i // tpb, 0, i % tpb, j)),
        out_shape=jax.ShapeDtypeStruct((bsz, dil, seq // dil, 3 * hd), BF16),
        scratch_shapes=[pltpu.VMEM((tm, k), BF16), pltpu.VMEM((k // LANES, tm, LANES), F32)],
        compiler_params=_cparams(2),
        name=f"proj_d{dil}",
    )(x2, w)


def _dilated_group(qkv, dil, bias, bsz, seq):
    hd = A_HEADS * A_HEAD_DIM
    L = seq // dil
    rows_per_step = 512
    tl = min(rows_per_step, L)
    nblk = tl // A_BLOCK
    nres = rows_per_step // tl

    def blk(col):
        return pl.BlockSpec((None, nres, tl, hd), lambda b, r, li: (b, r, li, col))

    def prev(col):
        return pl.BlockSpec((None, nres, A_BLOCK, hd),
                            lambda b, r, li: (b, r, jnp.maximum(li * nblk - 1, 0), col))

    return pl.pallas_call(
        functools.partial(_attn_body, nres=nres, nblk=nblk),
        grid=(bsz, dil // nres, L // tl),
        in_specs=[blk(0), blk(1), blk(2), prev(1), prev(2), _const_spec((A_HEADS, A_BLOCK, 2 * A_BLOCK))],
        out_specs=[pl.BlockSpec((None, nres, tl, hd), lambda b, r, li: (b, r, li, 0)),
                   pl.BlockSpec((None, nres, tl, LANES), lambda b, r, li: (b, r, li, 0))],
        out_shape=[jax.ShapeDtypeStruct((bsz, dil, L, hd), BF16),
                   jax.ShapeDtypeStruct((bsz, dil, L, LANES), F32)],
        compiler_params=_cparams(3),
        name=f"dilated_attn_d{dil}",
    )(qkv, qkv, qkv, qkv, qkv, bias)


def _attn_out_body(o1_ref, o2_ref, o3_ref, l1_ref, l2_ref, l3_ref, e_ref, w_ref, x_ref, g_ref, b_ref,
                   xo_ref, p_ref, so_ref, sl2_ref, sl3_ref, *, dils):
    def natural(ref, scr, dil):
        if dil == 1:
            return ref[0].astype(F32)
        n = ref.shape[1]
        nc = ref.shape[2] // LANES
        for r in range(dil):
            blk = ref[r].astype(F32)
            for c in range(nc):
                scr[c, pl.ds(r, n, stride=dil), :] = blk[:, c * LANES:(c + 1) * LANES]
        return jnp.concatenate([scr[c] for c in range(nc)], axis=1) if nc > 1 else scr[0]

    l1 = natural(l1_ref, None, dils[0])
    l2 = natural(l2_ref, sl2_ref, dils[1])
    l3 = natural(l3_ref, sl3_ref, dils[2])
    m = jnp.maximum(jnp.maximum(l1, l2), l3)
    e1, e2, e3 = jnp.exp(l1 - m), jnp.exp(l2 - m), jnp.exp(l3 - m)
    inv = 1.0 / (e1 + e2 + e3)
    ex = e_ref[...]

    def expand(wt):
        h = wt.astype(BF16)
        lo = (wt - h.astype(F32)).astype(BF16)
        return _dot(h, ex) + _dot(lo, ex)

    comb = expand(e1 * inv) * natural(o1_ref, None, dils[0])
    comb = comb + expand(e2 * inv) * natural(o2_ref, so_ref, dils[1])
    comb = comb + expand(e3 * inv) * natural(o3_ref, so_ref, dils[2])
    y = _dot(comb.astype(BF16), w_ref[...])
    xn = _ln_rows(ALPHA * x_ref[...] + y, g_ref[...], b_ref[...])
    xo_ref[...] = xn
    p_ref[...] = _pack_pairs(xn)


def _attn_out(os, lses, dils, w_out, x2, g, b, seq, tm=512):
    t, d = x2.shape
    tpb = seq // tm
    expand = np.zeros((LANES, d), np.float32)
    for h in range(A_HEADS):
        expand[h, h * A_HEAD_DIM:(h + 1) * A_HEAD_DIM] = 1.0
    row = lambda n: pl.BlockSpec((tm, n), lambda i: (i, 0))
    res = lambda dil, n: pl.BlockSpec((None, dil, tm // dil, n), lambda i: (i // tpb, 0, i % tpb, 0))
    return pl.pallas_call(
        functools.partial(_attn_out_body, dils=dils),
        grid=(t // tm,),
        in_specs=[res(dils[0], d), res(dils[1], d), res(dils[2], d),
                  res(dils[0], LANES), res(dils[1], LANES), res(dils[2], LANES),
                  _const_spec((LANES, d)), _const_spec((d, d)), row(d), _const_spec((1, d)), _const_spec((1, d))],
        out_specs=[row(d), row(d // 2)],
        out_shape=[jax.ShapeDtypeStruct((t, d), F32), jax.ShapeDtypeStruct((t, d // 2), U32)],
        scratch_shapes=[pltpu.VMEM((d // LANES, tm, LANES), F32), pltpu.VMEM((1, tm, LANES), F32),
                        pltpu.VMEM((1, tm, LANES), F32)],
        compiler_params=_cparams(1),
        name="attn_out",
    )(*os, *lses, jnp.asarray(expand, BF16), w_out, x2, g.reshape(1, d), b.reshape(1, d))


def _mixer_a(x2, w_in, w_out, rel_bias, g, b, bsz, seq):
    hd = A_HEADS * A_HEAD_DIM
    scale = np.ones((9 * hd,), np.float32)
    for gi in range(len(A_GROUPS)):
        scale[3 * gi * hd:(3 * gi + 1) * hd] = A_HEAD_DIM ** -0.5
    wb = (w_in * scale).astype(BF16)
    os, lses = [], []
    for gi, (window, dil) in enumerate(A_GROUPS):
        qkv = _proj_perm(x2, wb, gi, dil, bsz, seq)
        o, lse = _dilated_group(qkv, dil, _attn_bias(rel_bias, window, dil), bsz, seq)
        os.append(o)
        lses.append(lse)
    return _attn_out(os, lses, tuple(dl for _, dl in A_GROUPS), w_out.astype(BF16), x2, g, b, seq)


def _sgu_body(x_ref, wu_ref, wv_ref, ng_ref, nb_ref, wc_ref, bs_ref, a_ref, vb_ref, *, tm):
    xb = x_ref[...].astype(BF16)
    v = _gelu_tanh(_dot(xb, wv_ref[...]))
    vb_ref[...] = _ln_rows(v, ng_ref[...], nb_ref[...]).astype(BF16)
    gw = B_WIDTH // B_GROUPS
    ucols = 512
    for j in range(B_WIDTH // ucols):
        u = _gelu_tanh(_dot(xb, wu_ref[:, j * ucols:(j + 1) * ucols]))
        for c in range(tm // B_CHUNK):
            rows = slice(c * B_CHUNK, (c + 1) * B_CHUNK)
            for gg in range(ucols // gw):
                gi = j * (ucols // gw) + gg
                cols = slice(gi * gw, (gi + 1) * gw)
                f = _dot(wc_ref[gi], vb_ref[rows, cols]) + bs_ref[:, cols]
                a_ref[rows, cols] = (u[rows, gg * gw:(gg + 1) * gw] * f).astype(a_ref.dtype)


def _mixer_b(x2, w_in, norm_g, norm_b, w_s, b_s, w_out, g, b, tm=256):
    t, d = x2.shape
    wu = w_in[:, :B_WIDTH].astype(BF16)
    wv = w_in[:, B_WIDTH:].astype(BF16)
    wc = (w_s * jnp.tril(jnp.ones((B_CHUNK, B_CHUNK), w_s.dtype))).astype(BF16)
    bs_full = jnp.repeat(b_s.T, B_WIDTH // B_GROUPS, axis=1)
    a = pl.pallas_call(
        functools.partial(_sgu_body, tm=tm),
        grid=(t // tm,),
        in_specs=[pl.BlockSpec((tm, d), lambda i: (i, 0)), _const_spec((d, B_WIDTH)), _const_spec((d, B_WIDTH)),
                  _const_spec((1, B_WIDTH)), _const_spec((1, B_WIDTH)),
                  _const_spec((B_GROUPS, B_CHUNK, B_CHUNK)), _const_spec((B_CHUNK, B_WIDTH))],
        out_specs=pl.BlockSpec((tm, B_WIDTH), lambda i: (i, 0)),
        out_shape=jax.ShapeDtypeStruct((t, B_WIDTH), BF16),
        scratch_shapes=[pltpu.VMEM((tm, B_WIDTH), BF16)],
        compiler_params=_cparams(1),
        name="sgu",
    )(x2, wu, wv, norm_g.reshape(1, -1), norm_b.reshape(1, -1), wc, bs_full)
    return _mm_res_ln(a, w_out.astype(BF16), x2, g, b)


def _delta_body(qkvz_ref, gates_ref, cw_ref, alog_ref, dtb_ref, nw_ref, o_ref, ext_ref, state_ref, vnew_ref):
    ts = C_SUPER
    hd = C_HEADS * C_HEAD_DIM
    dk = C_HEAD_DIM
    nch = ts // C_CHUNK
    pr = 2 * C_CHUNK

    @pl.when(pl.program_id(1) == 0)
    def _():
        ext_ref[0:8, :] = jnp.zeros((8, 3 * hd), F32)
        state_ref[...] = jnp.zeros_like(state_ref)

    ext_ref[8:8 + ts, :] = qkvz_ref[:, 0:3 * hd].astype(F32)

    r_i = lax.broadcasted_iota(I32, (ts, ts), 0)
    c_i = lax.broadcasted_iota(I32, (ts, ts), 1)
    same = (r_i >> 6) == (c_i >> 6)
    u_cum = jnp.where(jnp.logical_and(same, r_i <= c_i), 1.0, 0.0).astype(BF16)
    u_tot = jnp.where(same, 1.0, 0.0).astype(BF16)
    r_p = lax.broadcasted_iota(I32, (pr, pr), 0)
    c_p = lax.broadcasted_iota(I32, (pr, pr), 1)
    same_p = (r_p >> 6) == (c_p >> 6)
    le = jnp.logical_and(same_p, c_p <= r_p)
    strict = jnp.logical_and(same_p, c_p < r_p)
    eye = jnp.where(r_p == c_p, 1.0, 0.0).astype(F32)

    gt = gates_ref[...].T
    zt = gt + dtb_ref[...]
    g_rows = -jnp.exp(alog_ref[...]) * (jnp.maximum(zt, 0.0) + jnp.log(1.0 + jnp.exp(-jnp.abs(zt))))
    gh, gm, gl = _split3(g_rows)
    gcum_rows = _dot(gh, u_cum) + _dot(gm, u_cum) + _dot(gl, u_cum)
    gtot_rows = _dot(gh, u_tot) + _dot(gm, u_tot) + _dot(gl, u_tot)
    gcum_cols = gcum_rows.T
    gtot_cols = gtot_rows.T
    beta_cols = _sigmoid(gates_ref[...])
    lane = lax.broadcasted_iota(I32, (1, LANES), 1)
    sub = lax.broadcasted_iota(I32, (LANES, 1), 0)
    csel = lax.broadcasted_iota(I32, (1, pr), 1) >> 6

    def conv_silu(c0):
        cols = pl.ds(pl.multiple_of(c0, dk), dk)
        y = (cw_ref[0:1, cols] * ext_ref[pl.ds(5, ts), cols] + cw_ref[1:2, cols] * ext_ref[pl.ds(6, ts), cols]
             + cw_ref[2:3, cols] * ext_ref[pl.ds(7, ts), cols] + cw_ref[3:4, cols] * ext_ref[pl.ds(8, ts), cols])
        return y * _sigmoid(y)

    def head_group(hg, carry):
        heads = [hg * C_GROUP + i for i in range(C_GROUP)]
        hv = []
        for h in heads:
            c0 = pl.multiple_of(h * dk, dk)
            pick_a = lane == (C_HEADS + h)
            gcol = jnp.sum(jnp.where(pick_a, gcum_cols, 0.0), axis=1, keepdims=True)
            glcol = jnp.sum(jnp.where(pick_a, gtot_cols, 0.0), axis=1, keepdims=True)
            bcol = jnp.sum(jnp.where(lane == h, beta_cols, 0.0), axis=1, keepdims=True)
            pick_r = sub == (C_HEADS + h)
            grow = jnp.sum(jnp.where(pick_r, gcum_rows, 0.0), axis=0, keepdims=True)
            glrow = jnp.sum(jnp.where(pick_r, gtot_rows, 0.0), axis=0, keepdims=True)
            q = conv_silu(c0)
            k = conv_silu(c0 + hd)
            v = conv_silu(c0 + 2 * hd)
            q = q * lax.rsqrt(jnp.sum(q * q, axis=-1, keepdims=True) + RMS_EPS) * (dk ** -0.5)
            k = k * lax.rsqrt(jnp.sum(k * k, axis=-1, keepdims=True) + RMS_EPS)
            eg = jnp.exp(gcol)
            kb = k * bcol
            hv.append(dict(
                c0=c0, gcol=gcol, grow=grow, glrow=glrow, kbf=k.astype(BF16), kbb=kb.astype(BF16),
                qbf=q.astype(BF16), rhs=jnp.concatenate([v * bcol, kb * eg], axis=1).astype(BF16),
                qe=q * eg, ktil_t=(k * jnp.exp(glcol - gcol)).T.astype(BF16)))

        inst = [(i, p) for i in range(C_GROUP) for p in range(ts // pr)]
        mpow, tinv, intra = {}, {}, {}
        for (i, p) in inst:
            d_ = hv[i]
            rp = slice(p * pr, (p + 1) * pr)
            decay = jnp.exp(jnp.where(le, d_["gcol"][rp] - d_["grow"][:, rp], NEG))
            lower = jnp.where(strict, _dot_nt(d_["kbb"][rp], d_["kbf"][rp]) * decay, 0.0)
            intra[i, p] = (_dot_nt(d_["qbf"][rp], d_["kbf"][rp]) * decay).astype(BF16)
            mpow[i, p] = -lower
            tinv[i, p] = eye - lower
        for _ in range(5):
            for key in inst:
                mb = mpow[key].astype(BF16)
                mpow[key] = _dot(mb, mb)
            for key in inst:
                tinv[key] = tinv[key] + _dot(tinv[key].astype(BF16), mpow[key].astype(BF16))
        wv_, kc = {}, {}
        for (i, p) in inst:
            wk = _dot(tinv[i, p].astype(BF16), hv[i]["rhs"][p * pr:(p + 1) * pr])
            wv_[i, p] = wk[:, :dk]
            kc[i, p] = wk[:, dk:]

        st = []
        for i, h in enumerate(heads):
            vnew_ref[i] = jnp.zeros(vnew_ref.shape[1:], vnew_ref.dtype)
            st.append(state_ref[h])
        outs = [[] for _ in heads]
        for j in range(nch):
            p, jj = j // 2, j % 2
            rows = slice(j * C_CHUNK, (j + 1) * C_CHUNK)
            lrows = slice(jj * C_CHUNK, (jj + 1) * C_CHUNK)
            rp = slice(p * pr, (p + 1) * pr)
            a1 = [_dot(jnp.concatenate([kc[i, p][lrows], hv[i]["qe"][rows]], axis=0).astype(BF16),
                       st[i].astype(BF16)) for i in range(C_GROUP)]
            for i in range(C_GROUP):
                vnew_ref[i, rows, :] = (wv_[i, p][lrows] - a1[i][:C_CHUNK]).astype(BF16)
            for i in range(C_GROUP):
                vn_pair = vnew_ref[i, rp, :]
                outs[i].append(a1[i][C_CHUNK:] + _dot(intra[i, p][lrows, :], vn_pair))
                kt_j = jnp.where(csel == jj, hv[i]["ktil_t"][:, rp], jnp.zeros((), BF16))
                dg = jnp.exp(jnp.sum(jnp.where(csel == jj, hv[i]["glrow"][:, rp], 0.0), axis=1, keepdims=True)
                             * (1.0 / C_CHUNK))
                st[i] = st[i] * dg + _dot(kt_j, vn_pair)
        for i, h in enumerate(heads):
            state_ref[h] = st[i]
            o = jnp.concatenate(outs[i], axis=0)
            o = o * lax.rsqrt(jnp.mean(o * o, axis=-1, keepdims=True) + RMS_EPS) * nw_ref[...]
            c0 = hv[i]["c0"]
            z = qkvz_ref[:, pl.ds(pl.multiple_of(c0 + 3 * hd, dk), dk)].astype(F32)
            o_ref[:, pl.ds(c0, dk)] = (o * (z * _sigmoid(z))).astype(o_ref.dtype)
        return carry

    lax.fori_loop(0, C_HEADS // C_GROUP, head_group, 0)
    ext_ref[0:8, :] = ext_ref[ts:ts + 8, :]


def _mixer_c(x2, w_in, conv_w, a_log, dt_bias, norm_w, w_out, g, b, bsz, seq):
    t, d = x2.shape
    hd = C_HEADS * C_HEAD_DIM
    qkvz = _proj(x2, w_in[:, :4 * hd].astype(BF16), BF16)
    wg = jnp.zeros((d, LANES), F32).at[:, :2 * C_HEADS].set(w_in[:, 4 * hd:])
    gates = _proj3(x2, wg)
    col = lambda v: jnp.zeros((LANES, 1), F32).at[C_HEADS:2 * C_HEADS, 0].set(v.astype(F32))
    nsteps = seq // C_SUPER
    o = pl.pallas_call(
        _delta_body,
        grid=(bsz, nsteps),
        in_specs=[pl.BlockSpec((C_SUPER, 4 * hd), lambda bi, i: (bi * nsteps + i, 0)),
                  pl.BlockSpec((C_SUPER, LANES), lambda bi, i: (bi * nsteps + i, 0)),
                  _const_spec((C_CONV, 3 * hd)), _const_spec((LANES, 1)), _const_spec((LANES, 1)),
                  _const_spec((1, C_HEAD_DIM))],
        out_specs=pl.BlockSpec((C_SUPER, hd), lambda bi, i: (bi * nsteps + i, 0)),
        out_shape=jax.ShapeDtypeStruct((t, hd), BF16),
        scratch_shapes=[pltpu.VMEM((C_SUPER + 8, 3 * hd), F32),
                        pltpu.VMEM((C_HEADS, C_HEAD_DIM, C_HEAD_DIM), F32),
                        pltpu.VMEM((C_GROUP, C_SUPER, C_HEAD_DIM), BF16)],
        compiler_params=_cparams(2),
        name="deltanet",
    )(qkvz, gates, conv_w.astype(F32), col(a_log), col(dt_bias), norm_w.reshape(1, -1).astype(F32))
    return _mm_res_ln(o, w_out.astype(BF16), x2, g, b)


def _route_body(x_ref, wh_ref, wl_ref, o_ref, g1_ref, g2_ref, cnt_ref, run_ref, *, tm):
    @pl.when(pl.program_id(0) == 0)
    def _():
        run_ref[...] = jnp.zeros_like(run_ref)

    x = x_ref[...]
    xh = x.astype(BF16)
    xl = (x - xh.astype(F32)).astype(BF16)
    wh = wh_ref[...]
    logits = _dot(xh, wh) + _dot(xl, wh) + _dot(xh, wl_ref[...])
    lane = lax.broadcasted_iota(I32, (1, LANES), 1)
    lane_f = lane.astype(F32)

    def top1(vals):
        m = jnp.max(vals, axis=-1, keepdims=True)
        idx = jnp.min(jnp.where(vals == m, lane_f, 1e9), axis=-1, keepdims=True)
        return m, idx.astype(I32)

    lc = jnp.where(lane < MOE_GROUPS, logits, NEG)
    mc, grp = top1(lc)
    p_grp = 1.0 / jnp.sum(jnp.exp(lc - mc), axis=-1, keepdims=True)
    lo = MOE_GROUPS + MOE_PER_GROUP * grp
    lf = jnp.where(jnp.logical_and(lane >= lo, lane < lo + MOE_PER_GROUP), logits, NEG)
    m1, i1 = top1(lf)
    m2, i2 = top1(jnp.where(lane == i1, NEG, lf))
    e21 = jnp.exp(m2 - m1)
    g1 = p_grp / (1.0 + e21)
    g2 = p_grp * e21 / (1.0 + e21)
    ex1 = i1 - MOE_GROUPS
    ex2 = i2 - MOE_GROUPS

    oh1 = jnp.where(lane == ex1, 1.0, 0.0).astype(F32)
    oh2 = jnp.where(lane == ex2, 1.0, 0.0).astype(F32)
    ohs = oh1 + oh2
    r_i = lax.broadcasted_iota(I32, (tm, tm), 0)
    c_i = lax.broadcasted_iota(I32, (tm, tm), 1)
    before = _dot(jnp.where(c_i < r_i, 1.0, 0.0).astype(BF16), ohs.astype(BF16)) + run_ref[...]
    rank1 = jnp.sum(oh1 * before, axis=-1, keepdims=True)
    rank2 = jnp.sum(oh2 * before, axis=-1, keepdims=True)
    run_ref[...] = run_ref[...] + jnp.sum(ohs, axis=0, keepdims=True)

    out = jnp.where(lane == 0, ex1.astype(F32), 0.0)
    out = jnp.where(lane == 1, ex2.astype(F32), out)
    out = jnp.where(lane == 2, g1, out)
    out = jnp.where(lane == 3, g2, out)
    out = jnp.where(lane == 4, rank1, out)
    out = jnp.where(lane == 5, rank2, out)
    o_ref[...] = out
    g1_ref[...] = jnp.broadcast_to(g1, g1_ref.shape)
    g2_ref[...] = jnp.broadcast_to(g2, g2_ref.shape)
    cnt_ref[...] = jnp.broadcast_to(run_ref[...], cnt_ref.shape)


def _route(x2, w_coarse, w_fine, tm=512):
    t, d = x2.shape
    wr = jnp.zeros((d, LANES), F32)
    wr = wr.at[:, :MOE_GROUPS].set(w_coarse).at[:, MOE_GROUPS:MOE_GROUPS + MOE_EXPERTS].set(
        w_fine.reshape(d, MOE_EXPERTS))
    wh = wr.astype(BF16)
    wl = (wr - wh.astype(F32)).astype(BF16)
    return pl.pallas_call(
        functools.partial(_route_body, tm=tm),
        grid=(t // tm,),
        in_specs=[pl.BlockSpec((tm, d), lambda i: (i, 0)), _const_spec((d, LANES)), _const_spec((d, LANES))],
        out_specs=[pl.BlockSpec((tm, LANES), lambda i: (i, 0)), pl.BlockSpec((tm, LANES), lambda i: (i, 0)),
                   pl.BlockSpec((tm, LANES), lambda i: (i, 0)), pl.BlockSpec((8, LANES), lambda i: (0, 0))],
        out_shape=[jax.ShapeDtypeStruct((t, LANES), F32), jax.ShapeDtypeStruct((t, LANES), F32),
                   jax.ShapeDtypeStruct((t, LANES), F32), jax.ShapeDtypeStruct((8, LANES), F32)],
        scratch_shapes=[pltpu.VMEM((1, LANES), F32)],
        compiler_params=_cparams(1),
        name="moe_route",
    )(x2, wh, wl)


def _dispatch_body(p1_ref, p2_ref, x_ref, o_hbm, slab_ref, sem, *, tm):
    s = pl.program_id(0)
    i = pl.program_id(1)

    @pl.when(i == 0)
    def _():
        slab_ref[...] = jnp.zeros_like(slab_ref)

    def row(r, carry):
        v = x_ref[pl.ds(r, 1), :]
        slab_ref[pl.ds(p1_ref[0, r], 1), :] = v
        slab_ref[pl.ds(p2_ref[0, r], 1), :] = v
        return carry
    lax.fori_loop(0, tm, row, 0, unroll=8)

    @pl.when(i == pl.num_programs(1) - 1)
    def _():
        cp = pltpu.make_async_copy(slab_ref, o_hbm.at[s], sem)
        cp.start()
        cp.wait()


def _dispatch(xp, pos1, pos2, s_pad, tm=1024):
    t, w = xp.shape
    nslab = w // LANES
    nt = t // tm
    return pl.pallas_call(
        functools.partial(_dispatch_body, tm=tm),
        grid=(nslab, nt),
        in_specs=[pl.BlockSpec((None, 1, tm), lambda s, i: (i, 0, 0), memory_space=pltpu.SMEM),
                  pl.BlockSpec((None, 1, tm), lambda s, i: (i, 0, 0), memory_space=pltpu.SMEM),
                  pl.BlockSpec((tm, LANES), lambda s, i: (i, s))],
        out_specs=pl.BlockSpec(memory_space=pl.ANY),
        out_shape=jax.ShapeDtypeStruct((nslab, s_pad, LANES), U32),
        scratch_shapes=[pltpu.VMEM((s_pad, LANES), U32), pltpu.SemaphoreType.DMA(())],
        compiler_params=_cparams(2),
        name="moe_dispatch",
    )(pos1.reshape(nt, 1, tm), pos2.reshape(nt, 1, tm), xp)


def _expert_body(te_ref, nv_ref, xs_ref, wg_ref, wu_ref, wd_ref, y_ref, wgb_ref, wub_ref, wdb_ref):
    i = pl.program_id(0)
    half = D_MODEL // 2
    nslab = half // LANES

    @pl.when(i < nv_ref[0])
    def _():
        changed = jnp.logical_or(i == 0, te_ref[i] != te_ref[jnp.maximum(i - 1, 0)])

        @pl.when(changed)
        def _():
            wgb_ref[...] = wg_ref[...].astype(BF16)
            wub_ref[...] = wu_ref[...].astype(BF16)
            wdb_ref[...] = wd_ref[...].astype(BF16)

        xlo = jnp.concatenate([_unpack_lo(xs_ref[s]) for s in range(nslab)], axis=1).astype(BF16)
        xhi = jnp.concatenate([_unpack_hi(xs_ref[s]) for s in range(nslab)], axis=1).astype(BF16)
        hg =_dot(xlo, wgb_ref[0:half, :]) + _dot(xhi, wgb_ref[half:, :])
        hu = _dot(xlo, wub_ref[0:half, :]) + _dot(xhi, wub_ref[half:, :])
        h = (hg * _sigmoid(hg) * hu).astype(BF16)
        packed = _pack_pairs(_dot(h, wdb_ref[...]))
        for s in range(nslab):
            y_ref[s] = packed[:, s * LANES:(s + 1) * LANES]

    @pl.when(i >= nv_ref[0])
    def _():
        y_ref[...] = jnp.zeros_like(y_ref)


def _experts(xs, tile_expert, nvalid, w_gate, w_up, w_down, layer, n_tiles):
    tm = MOE_TM
    d, hdn = D_MODEL, MOE_HIDDEN
    nslab = d // 2 // LANES
    return pl.pallas_call(
        _expert_body,
        grid_spec=pltpu.PrefetchScalarGridSpec(
            num_scalar_prefetch=2,
            grid=(n_tiles,),
            in_specs=[pl.BlockSpec((nslab, tm, LANES), lambda i, te, nv: (0, i, 0)),
                      pl.BlockSpec((None, None, d, hdn), lambda i, te, nv: (layer, te[i], 0, 0)),
                      pl.BlockSpec((None, None, d, hdn), lambda i, te, nv: (layer, te[i], 0, 0)),
                      pl.BlockSpec((None, None, hdn, d), lambda i, te, nv: (layer, te[i], 0, 0))],
            out_specs=pl.BlockSpec((nslab, tm, LANES), lambda i, te, nv: (0, i, 0)),
            scratch_shapes=[pltpu.VMEM((d, hdn), BF16), pltpu.VMEM((d, hdn), BF16), pltpu.VMEM((hdn, d), BF16)],
        ),
        out_shape=jax.ShapeDtypeStruct((nslab, n_tiles * tm, LANES), U32),
        compiler_params=_cparams(1),
        name="moe_experts",
    )(tile_expert, nvalid, xs, w_gate, w_up, w_down)


def _combine_body(p1_ref, p2_ref, y_ref, g1_ref, g2_ref, lo_ref, hi_ref, b1_ref, b2_ref, *, tm):
    def row(r, carry):
        b1_ref[pl.ds(r, 1), :] = y_ref[pl.ds(p1_ref[0, r], 1), :]
        b2_ref[pl.ds(r, 1), :] = y_ref[pl.ds(p2_ref[0, r], 1), :]
        return carry
    lax.fori_loop(0, tm, row, 0, unroll=8)
    g1 = g1_ref[...]
    g2 = g2_ref[...]
    y1 = b1_ref[...]
    y2 = b2_ref[...]
    lo_ref[...] = g1 * _unpack_lo(y1) + g2 * _unpack_lo(y2)
    hi_ref[...] = g1 * _unpack_hi(y1) + g2 * _unpack_hi(y2)


def _combine(y, pos1, pos2, g1b, g2b, tm=1024):
    nslab, s_pad, _ = y.shape
    t = pos1.shape[0]
    nt = t // tm
    return pl.pallas_call(
        functools.partial(_combine_body, tm=tm),
        grid=(nslab, nt),
        in_specs=[pl.BlockSpec((None, 1, tm), lambda s, i: (i, 0, 0), memory_space=pltpu.SMEM),
                  pl.BlockSpec((None, 1, tm), lambda s, i: (i, 0, 0), memory_space=pltpu.SMEM),
                  pl.BlockSpec((None, s_pad, LANES), lambda s, i: (s, 0, 0), pipeline_mode=pl.Buffered(1)),
                  pl.BlockSpec((tm, LANES), lambda s, i: (i, 0)), pl.BlockSpec((tm, LANES), lambda s, i: (i, 0))],
        out_specs=[pl.BlockSpec((tm, LANES), lambda s, i: (i, s)), pl.BlockSpec((tm, LANES), lambda s, i: (i, s))],
        out_shape=[jax.ShapeDtypeStruct((t, nslab * LANES), F32), jax.ShapeDtypeStruct((t, nslab * LANES), F32)],
        scratch_shapes=[pltpu.VMEM((tm, LANES), U32), pltpu.VMEM((tm, LANES), U32)],
        compiler_params=_cparams(2),
        name="moe_combine",
    )(pos1.reshape(nt, 1, tm), pos2.reshape(nt, 1, tm), y, g1b, g2b)


def _res_ln_body(x_ref, lo_ref, hi_ref, g_ref, b_ref, o_ref):
    h = jnp.concatenate([lo_ref[...], hi_ref[...]], axis=1)
    o_ref[...] = _ln_rows(ALPHA * x_ref[...] + h, g_ref[...], b_ref[...])


def _res_ln(x2, lo, hi, g, b, tm=1024):
    t, d = x2.shape
    return pl.pallas_call(
        _res_ln_body,
        grid=(t // tm,),
        in_specs=[pl.BlockSpec((tm, d), lambda i: (i, 0)), pl.BlockSpec((tm, d // 2), lambda i: (i, 0)),
                  pl.BlockSpec((tm, d // 2), lambda i: (i, 0)), _const_spec((1, d)), _const_spec((1, d))],
        out_specs=pl.BlockSpec((tm, d), lambda i: (i, 0)),
        out_shape=jax.ShapeDtypeStruct((t, d), F32),
        compiler_params=_cparams(1),
        name="res_ln",
    )(x2, lo, hi, g.reshape(1, d), b.reshape(1, d))


def _moe(x2, xp, w_coarse, w_fine, w_gate, w_up, w_down, layer, g, b):
    t = x2.shape[0]
    tm = MOE_TM
    n_tiles = (2 * t) // tm + MOE_EXPERTS
    route, g1b, g2b, cnt = _route(x2, w_coarse, w_fine)
    ex = route[:, 0:2].astype(I32)
    rank = route[:, 4:6].astype(I32)
    counts = cnt[0, :MOE_EXPERTS].astype(I32)
    ptiles = (counts + tm - 1) // tm
    tile_end = jnp.cumsum(ptiles)
    pstart = (tile_end - ptiles) * tm
    eids = jnp.arange(MOE_EXPERTS, dtype=I32)
    pos = jnp.sum(jnp.where(ex[:, :, None] == eids, pstart, 0), axis=-1) + rank
    nvalid = tile_end[-1:].astype(I32)
    tidx = jnp.arange(n_tiles, dtype=I32)
    te = jnp.sum((tile_end[None, :] <= jnp.minimum(tidx, nvalid[0] - 1)[:, None]).astype(I32), axis=1)
    xs = _dispatch(xp, pos[:, 0], pos[:, 1], n_tiles * tm)
    y = _experts(xs, te, nvalid, w_gate, w_up, w_down, layer, n_tiles)
    lo, hi = _combine(y, pos[:, 0], pos[:, 1], g1b, g2b)
    return _res_ln(x2, lo, hi, g, b)


def kernel(x, rel_bias, a_w_in, a_w_out, b_w_in, b_norm_g, b_norm_b, b_w_s, b_b_s, b_w_out, c_w_in, c_conv,
           c_a_log, c_dt_bias, c_norm_w, c_w_out, ln_g, ln_b, moe_w_coarse, moe_w_fine, moe_w_gate, moe_w_up,
           moe_w_down):
    bsz, seq, d = x.shape
    x2 = x.reshape(bsz * seq, d)
    for i in range(DEPTH):
        kind, j = i % 3, i // 3
        g1, b1 = ln_g[i, 0], ln_b[i, 0]
        if kind == 0:
            x2, xp = _mixer_a(x2, a_w_in[j], a_w_out[j], rel_bias, g1, b1, bsz, seq)
        elif kind == 1:
            x2, xp = _mixer_b(x2, b_w_in[j], b_norm_g[j], b_norm_b[j], b_w_s[j], b_b_s[j], b_w_out[j], g1, b1)
        else:
            x2, xp = _mixer_c(x2, c_w_in[j], c_conv[j], c_a_log[j], c_dt_bias[j], c_norm_w[j], c_w_out[j],
                              g1, b1, bsz, seq)
        x2 = _moe(x2, xp, moe_w_coarse[i], moe_w_fine[i], moe_w_gate, moe_w_up, moe_w_down, i,
                  ln_g[i, 1], ln_b[i, 1])
    return x2.reshape(bsz, seq, d)
```

```python
import functools
import math

import numpy as np
import jax
import jax.numpy as jnp
from jax import lax
from jax.experimental import pallas as pl
from jax.experimental.pallas import tpu as pltpu

F32 = jnp.float32
BF16 = jnp.bfloat16
U32 = jnp.uint32
I32 = jnp.int32

D_MODEL = 1024
DEPTH = 4
A_GROUPS = ((128, 1), (512, 4), (2048, 16))
A_HEADS = 16
A_HEAD_DIM = 64
A_BLOCK = 128
NUM_BUCKETS = 32
MAX_DISTANCE = 2048
B_CHUNK = 128
B_WIDTH = 2 * D_MODEL
B_GROUPS = 16
C_HEADS = 8
C_HEAD_DIM = 128
C_CONV = 4
C_CHUNK = 64
MOE_GROUPS = 8
MOE_PER_GROUP = 8
MOE_EXPERTS = 64
MOE_HIDDEN = 512
LN_EPS = 1e-5
RMS_EPS = 1e-6
ALPHA = (2 * DEPTH) ** 0.25

LANES = 128
NEG = -1e30
VMEM_LIMIT = 56 * 1024 * 1024
MOE_TM = 512
ROW_UNROLL = 16
C_SUPER = 256
C_GROUP = 4


def _cparams(n_axes, vmem=VMEM_LIMIT):
    return pltpu.CompilerParams(dimension_semantics=("arbitrary",) * n_axes, vmem_limit_bytes=vmem)


def _const_spec(shape):
    nd = len(shape)
    return pl.BlockSpec(shape, lambda *_: (0,) * nd, pipeline_mode=pl.Buffered(1))


def _ln_rows(y, g, b):
    mu = jnp.mean(y, axis=-1, keepdims=True)
    yc = y - mu
    var = jnp.mean(yc * yc, axis=-1, keepdims=True)
    return yc * lax.rsqrt(var + LN_EPS) * g + b


def _pack_pairs(y):
    w = y.shape[1] // 2
    lo = lax.bitcast_convert_type(y[:, :w].astype(BF16).astype(F32), U32)
    hi = lax.bitcast_convert_type(y[:, w:].astype(BF16).astype(F32), U32)
    return (lo >> 16) | (hi & jnp.uint32(0xFFFF0000))


def _unpack_lo(p):
    return lax.bitcast_convert_type(p << 16, F32)


def _unpack_hi(p):
    return lax.bitcast_convert_type(p & jnp.uint32(0xFFFF0000), F32)


def _split3(a):
    h = a.astype(BF16)
    r = a - h.astype(F32)
    m = r.astype(BF16)
    l = (r - m.astype(F32)).astype(BF16)
    return h, m, l


def _dot(a, b):
    return jnp.dot(a, b, preferred_element_type=F32)


def _dot_nt(a, b):
    return lax.dot_general(a, b, (((1,), (1,)), ((), ())), preferred_element_type=F32)


def _gelu_tanh(x):
    return 0.5 * x * (1.0 + jnp.tanh(0.7978845608028654 * (x + 0.044715 * (x * x * x))))


def _sigmoid(x):
    return 1.0 / (1.0 + jnp.exp(-x))


def _proj_body(x_ref, w_ref, o_ref, xb_ref):
    @pl.when(pl.program_id(1) == 0)
    def _():
        xb_ref[...] = x_ref[...].astype(BF16)

    o_ref[...] = _dot(xb_ref[...], w_ref[...]).astype(o_ref.dtype)


def _proj(x2, w, out_dtype, tm=1024, tn=1024):
    t, k = x2.shape
    n = w.shape[1]
    tn = min(tn, n)
    return pl.pallas_call(
        _proj_body,
        grid=(t // tm, n // tn),
        in_specs=[pl.BlockSpec((tm, k), lambda i, j: (i, 0)),
                  pl.BlockSpec((k, tn), lambda i, j: (0, j))],
        out_specs=pl.BlockSpec((tm, tn), lambda i, j: (i, j)),
        out_shape=jax.ShapeDtypeStruct((t, n), out_dtype),
        scratch_shapes=[pltpu.VMEM((tm, k), BF16)],
        compiler_params=_cparams(2),
        name="proj",
    )(x2, w)


def _proj3_body(x_ref, wh_ref, wl_ref, o_ref):
    x = x_ref[...]
    xh = x.astype(BF16)
    xl = (x - xh.astype(F32)).astype(BF16)
    wh = wh_ref[...]
    o_ref[...] = _dot(xh, wh) + _dot(xl, wh) + _dot(xh, wl_ref[...])


def _proj3(x2, w, tm=1024):
    t, k = x2.shape
    n = w.shape[1]
    wh = w.astype(BF16)
    wl = (w - wh.astype(F32)).astype(BF16)
    return pl.pallas_call(
        _proj3_body,
        grid=(t // tm,),
        in_specs=[pl.BlockSpec((tm, k), lambda i: (i, 0)), _const_spec((k, n)), _const_spec((k, n))],
        out_specs=pl.BlockSpec((tm, n), lambda i: (i, 0)),
        out_shape=jax.ShapeDtypeStruct((t, n), F32),
        compiler_params=_cparams(1),
        name="proj3",
    )(x2, wh, wl)


def _mm_res_ln_body(a_ref, w_ref, x_ref, g_ref, b_ref, o_ref, p_ref):
    y = _dot(a_ref[...], w_ref[...])
    xn = _ln_rows(ALPHA * x_ref[...] + y, g_ref[...], b_ref[...])
    o_ref[...] = xn
    p_ref[...] = _pack_pairs(xn)


def _mm_res_ln(a, w, x2, g, b, tm=512):
    t, k = a.shape
    d = w.shape[1]
    return pl.pallas_call(
        _mm_res_ln_body,
        grid=(t // tm,),
        in_specs=[pl.BlockSpec((tm, k), lambda i: (i, 0)), _const_spec((k, d)),
                  pl.BlockSpec((tm, d), lambda i: (i, 0)), _const_spec((1, d)), _const_spec((1, d))],
        out_specs=[pl.BlockSpec((tm, d), lambda i: (i, 0)), pl.BlockSpec((tm, d // 2), lambda i: (i, 0))],
        out_shape=[jax.ShapeDtypeStruct((t, d), F32), jax.ShapeDtypeStruct((t, d // 2), U32)],
        compiler_params=_cparams(1),
        name="mm_res_ln",
    )(a, w, x2, g.reshape(1, d), b.reshape(1, d))


def _t5_bucket(dist):
    max_exact = NUM_BUCKETS // 2
    d = jnp.maximum(dist, 1).astype(F32)
    large = max_exact + (jnp.log(d / max_exact) / math.log(MAX_DISTANCE / max_exact)
                         * (NUM_BUCKETS - max_exact)).astype(I32)
    return jnp.where(dist < max_exact, dist, jnp.minimum(large, NUM_BUCKETS - 1))


def _attn_bias(rel_bias, window, dil):
    steps = window // dil
    qi = jnp.arange(A_BLOCK)[:, None]
    ki = jnp.arange(2 * A_BLOCK)[None, :]
    rel = qi + A_BLOCK - ki
    valid = (rel >= 0) & (rel <= steps)
    bucket = _t5_bucket(jnp.maximum(rel, 0) * dil)
    bias = jnp.zeros((A_HEADS, A_BLOCK, 2 * A_BLOCK), F32)
    for bkt in range(NUM_BUCKETS):
        bias = jnp.where((bucket == bkt)[None], rel_bias[bkt].astype(F32)[:, None, None], bias)
    return jnp.where(valid[None], bias, NEG)


def _attn_body(q_ref, k_ref, v_ref, kp_ref, vp_ref, bias_ref, o_ref, lse_ref, *, nres, nblk):
    li = pl.program_id(2)
    lane = lax.broadcasted_iota(I32, (1, LANES), 1)
    col2 = lax.broadcasted_iota(I32, (1, 2 * A_BLOCK), 1)
    first_pen = jnp.where(col2 < A_BLOCK, jnp.where(li == 0, NEG, 0.0).astype(F32), 0.0)
    head0 = lane < A_HEAD_DIM
    lse_ref[...] = jnp.zeros_like(lse_ref)

    def hp_body(hp, carry):
        c0 = pl.multiple_of(hp * LANES, LANES)
        cols = pl.ds(c0, LANES)
        blocks = [(r, j) for r in range(nres) for j in range(nblk)]
        units = [(r, j, e) for (r, j) in blocks for e in range(2)]
        vbs, s_ = {}, {}
        for (r, j) in blocks:
            rows = pl.ds(j * A_BLOCK, A_BLOCK)
            q = q_ref[r, rows, cols]
            if j == 0:
                kb = jnp.concatenate([kp_ref[r, :, cols], k_ref[r, rows, cols]], axis=0)
                vbs[r, j] = jnp.concatenate([vp_ref[r, :, cols], v_ref[r, rows, cols]], axis=0)
            else:
                band = pl.ds((j - 1) * A_BLOCK, 2 * A_BLOCK)
                kb = k_ref[r, band, cols]
                vbs[r, j] = v_ref[r, band, cols]
            for e in range(2):
                qm = jnp.where(head0 if e == 0 else jnp.logical_not(head0), q, jnp.zeros_like(q))
                s = _dot_nt(qm, kb) + bias_ref[2 * hp + e]
                s_[r, j, e] = s + first_pen if j == 0 else s
        p_, l_, lse_ = {}, {}, {}
        for u in units:
            m = jnp.max(s_[u], axis=-1, keepdims=True)
            p = jnp.exp(s_[u] - m)
            l_[u] = jnp.sum(p, axis=-1, keepdims=True)
            p_[u] = p.astype(BF16)
            lse_[u] = m + jnp.log(l_[u])
        pv = {u: _dot(p_[u], vbs[u[0], u[1]]) for u in units}
        for (r, j) in blocks:
            rows = pl.ds(j * A_BLOCK, A_BLOCK)
            o_ref[r, rows, cols] = jnp.where(head0, pv[r, j, 0] / l_[r, j, 0],
                                             pv[r, j, 1] / l_[r, j, 1]).astype(o_ref.dtype)
            cur = lse_ref[r, rows, :]
            cur = jnp.where(lane == 2 * hp, lse_[r, j, 0], cur)
            cur = jnp.where(lane == 2 * hp + 1, lse_[r, j, 1], cur)
            lse_ref[r, rows, :] = cur
        return carry

    lax.fori_loop(0, A_HEADS // 2, hp_body, 0)


def _proj_perm_body(x_ref, w_ref, o_ref, xb_ref, y_ref, *, dil):
    tm, k = x_ref.shape
    n = tm // dil

    @pl.when(pl.program_id(1) == 0)
    def _():
        if dil == 1:
            xb_ref[...] = x_ref[...].astype(BF16)
        else:
            nc = k // LANES
            for c in range(nc):
                y_ref[c] = x_ref[:, c * LANES:(c + 1) * LANES]
            for r in range(dil):
                xb_ref[r * n:(r + 1) * n, :] = jnp.concatenate(
                    [y_ref[c, pl.ds(r, n, stride=dil), :] for c in range(nc)], axis=1).astype(BF16)

    y = _dot(xb_ref[...], w_ref[...])
    for r in range(dil):
        o_ref[r] = y[r * n:(r + 1) * n].astype(o_ref.dtype)


def _proj_perm(x2, w, g, dil, bsz, seq, tm=1024, tn=1024):
    t, k = x2.shape
    hd = A_HEADS * A_HEAD_DIM
    tpb = seq // tm
    return pl.pallas_call(
        functools.partial(_proj_perm_body, dil=dil),
        grid=(t // tm, 3 * hd // tn),
        in_specs=[pl.BlockSpec((tm, k), lambda i, j: (i, 0)),
                  pl.BlockSpec((k, tn), lambda i, j: (0, 3 * g * (hd // tn) + j))],
        out_specs=pl.BlockSpec((None, dil, tm // dil, tn), lambda i, j: (i // tpb, 0, i % tpb, j)),
        out_shape=jax.ShapeDtypeStruct((bsz, dil, seq // dil, 3 * hd), BF16),
        scratch_shapes=[pltpu.VMEM((tm, k), BF16), pltpu.VMEM((k // LANES, tm, LANES), F32)],
        compiler_params=_cparams(2),
        name=f"proj_d{dil}",
    )(x2, w)


def _dilated_group(qkv, dil, bias, bsz, seq):
    hd = A_HEADS * A_HEAD_DIM
    L = seq // dil
    rows_per_step = 512
    tl = min(rows_per_step, L)
    nblk = tl // A_BLOCK
    nres = rows_per_step // tl

    def blk(col):
        return pl.BlockSpec((None, nres, tl, hd), lambda b, r, li: (b, r, li, col))

    def prev(col):
        return pl.BlockSpec((None, nres, A_BLOCK, hd),
                            lambda b, r, li: (b, r, jnp.maximum(li * nblk - 1, 0), col))

    return pl.pallas_call(
        functools.partial(_attn_body, nres=nres, nblk=nblk),
        grid=(bsz, dil // nres, L // tl),
        in_specs=[blk(0), blk(1), blk(2), prev(1), prev(2), _const_spec((A_HEADS, A_BLOCK, 2 * A_BLOCK))],
        out_specs=[pl.BlockSpec((None, nres, tl, hd), lambda b, r, li: (b, r, li, 0)),
                   pl.BlockSpec((None, nres, tl, LANES), lambda b, r, li: (b, r, li, 0))],
        out_shape=[jax.ShapeDtypeStruct((bsz, dil, L, hd), BF16),
                   jax.ShapeDtypeStruct((bsz, dil, L, LANES), F32)],
        compiler_params=_cparams(3),
        name=f"dilated_attn_d{dil}",
    )(qkv, qkv, qkv, qkv, qkv, bias)


def _attn_out_body(o1_ref, o2_ref, o3_ref, l1_ref, l2_ref, l3_ref, e_ref, w_ref, x_ref, g_ref, b_ref,
                   xo_ref, p_ref, so_ref, sl2_ref, sl3_ref, *, dils):
    def natural(ref, scr, dil):
        if dil == 1:
            return ref[0].astype(F32)
        n = ref.shape[1]
        nc = ref.shape[2] // LANES
        for r in range(dil):
            blk = ref[r].astype(F32)
            for c in range(nc):
                scr[c, pl.ds(r, n, stride=dil), :] = blk[:, c * LANES:(c + 1) * LANES]
        return jnp.concatenate([scr[c] for c in range(nc)], axis=1) if nc > 1 else scr[0]

    l1 = natural(l1_ref, None, dils[0])
    l2 = natural(l2_ref, sl2_ref, dils[1])
    l3 = natural(l3_ref, sl3_ref, dils[2])
    m = jnp.maximum(jnp.maximum(l1, l2), l3)
    e1, e2, e3 = jnp.exp(l1 - m), jnp.exp(l2 - m), jnp.exp(l3 - m)
    inv = 1.0 / (e1 + e2 + e3)
    ex = e_ref[...]

    def expand(wt):
        h = wt.astype(BF16)
        lo = (wt - h.astype(F32)).astype(BF16)
        return _dot(h, ex) + _dot(lo, ex)

    comb = expand(e1 * inv) * natural(o1_ref, None, dils[0])
    comb = comb + expand(e2 * inv) * natural(o2_ref, so_ref, dils[1])
    comb = comb + expand(e3 * inv) * natural(o3_ref, so_ref, dils[2])
    y = _dot(comb.astype(BF16), w_ref[...])
    xn = _ln_rows(ALPHA * x_ref[...] + y, g_ref[...], b_ref[...])
    xo_ref[...] = xn
    p_ref[...] = _pack_pairs(xn)


def _attn_out(os, lses, dils, w_out, x2, g, b, seq, tm=512):
    t, d = x2.shape
    tpb = seq // tm
    expand = np.zeros((LANES, d), np.float32)
    for h in range(A_HEADS):
        expand[h, h * A_HEAD_DIM:(h + 1) * A_HEAD_DIM] = 1.0
    row = lambda n: pl.BlockSpec((tm, n), lambda i: (i, 0))
    res = lambda dil, n: pl.BlockSpec((None, dil, tm // dil, n), lambda i: (i // tpb, 0, i % tpb, 0))
    return pl.pallas_call(
        functools.partial(_attn_out_body, dils=dils),
        grid=(t // tm,),
        in_specs=[res(dils[0], d), res(dils[1], d), res(dils[2], d),
                  res(dils[0], LANES), res(dils[1], LANES), res(dils[2], LANES),
                  _const_spec((LANES, d)), _const_spec((d, d)), row(d), _const_spec((1, d)), _const_spec((1, d))],
        out_specs=[row(d), row(d // 2)],
        out_shape=[jax.ShapeDtypeStruct((t, d), F32), jax.ShapeDtypeStruct((t, d // 2), U32)],
        scratch_shapes=[pltpu.VMEM((d // LANES, tm, LANES), F32), pltpu.VMEM((1, tm, LANES), F32),
                        pltpu.VMEM((1, tm, LANES), F32)],
        compiler_params=_cparams(1),
        name="attn_out",
    )(*os, *lses, jnp.asarray(expand, BF16), w_out, x2, g.reshape(1, d), b.reshape(1, d))


def _mixer_a(x2, w_in, w_out, rel_bias, g, b, bsz, seq):
    hd = A_HEADS * A_HEAD_DIM
    scale = np.ones((9 * hd,), np.float32)
    for gi in range(len(A_GROUPS)):
        scale[3 * gi * hd:(3 * gi + 1) * hd] = A_HEAD_DIM ** -0.5
    wb = (w_in * scale).astype(BF16)
    os, lses = [], []
    for gi, (window, dil) in enumerate(A_GROUPS):
        qkv = _proj_perm(x2, wb, gi, dil, bsz, seq)
        o, lse = _dilated_group(qkv, dil, _attn_bias(rel_bias, window, dil), bsz, seq)
        os.append(o)
        lses.append(lse)
    return _attn_out(os, lses, tuple(dl for _, dl in A_GROUPS), w_out.astype(BF16), x2, g, b, seq)


def _sgu_body(x_ref, wu_ref, wv_ref, ng_ref, nb_ref, wc_ref, bs_ref, a_ref, vb_ref, *, tm):
    xb = x_ref[...].astype(BF16)
    v = _gelu_tanh(_dot(xb, wv_ref[...]))
    vb_ref[...] = _ln_rows(v, ng_ref[...], nb_ref[...]).astype(BF16)
    gw = B_WIDTH // B_GROUPS
    ucols = 512
    for j in range(B_WIDTH // ucols):
        u = _gelu_tanh(_dot(xb, wu_ref[:, j * ucols:(j + 1) * ucols]))
        for c in range(tm // B_CHUNK):
            rows = slice(c * B_CHUNK, (c + 1) * B_CHUNK)
            for gg in range(ucols // gw):
                gi = j * (ucols // gw) + gg
                cols = slice(gi * gw, (gi + 1) * gw)
                f = _dot(wc_ref[gi], vb_ref[rows, cols]) + bs_ref[:, cols]
                a_ref[rows, cols] = (u[rows, gg * gw:(gg + 1) * gw] * f).astype(a_ref.dtype)


def _mixer_b(x2, w_in, norm_g, norm_b, w_s, b_s, w_out, g, b, tm=256):
    t, d = x2.shape
    wu = w_in[:, :B_WIDTH].astype(BF16)
    wv = w_in[:, B_WIDTH:].astype(BF16)
    wc = (w_s * jnp.tril(jnp.ones((B_CHUNK, B_CHUNK), w_s.dtype))).astype(BF16)
    bs_full = jnp.repeat(b_s.T, B_WIDTH // B_GROUPS, axis=1)
    a = pl.pallas_call(
        functools.partial(_sgu_body, tm=tm),
        grid=(t // tm,),
        in_specs=[pl.BlockSpec((tm, d), lambda i: (i, 0)), _const_spec((d, B_WIDTH)), _const_spec((d, B_WIDTH)),
                  _const_spec((1, B_WIDTH)), _const_spec((1, B_WIDTH)),
                  _const_spec((B_GROUPS, B_CHUNK, B_CHUNK)), _const_spec((B_CHUNK, B_WIDTH))],
        out_specs=pl.BlockSpec((tm, B_WIDTH), lambda i: (i, 0)),
        out_shape=jax.ShapeDtypeStruct((t, B_WIDTH), BF16),
        scratch_shapes=[pltpu.VMEM((tm, B_WIDTH), BF16)],
        compiler_params=_cparams(1),
        name="sgu",
    )(x2, wu, wv, norm_g.reshape(1, -1), norm_b.reshape(1, -1), wc, bs_full)
    return _mm_res_ln(a, w_out.astype(BF16), x2, g, b)


def _delta_body(qkvz_ref, gates_ref, cw_ref, alog_ref, dtb_ref, nw_ref, o_ref, ext_ref, state_ref, vnew_ref):
    ts = C_SUPER
    hd = C_HEADS * C_HEAD_DIM
    dk = C_HEAD_DIM
    nch = ts // C_CHUNK
    pr = 2 * C_CHUNK

    @pl.when(pl.program_id(1) == 0)
    def _():
        ext_ref[0:8, :] = jnp.zeros((8, 3 * hd), F32)
        state_ref[...] = jnp.zeros_like(state_ref)

    ext_ref[8:8 + ts, :] = qkvz_ref[:, 0:3 * hd].astype(F32)

    r_i = lax.broadcasted_iota(I32, (ts, ts), 0)
    c_i = lax.broadcasted_iota(I32, (ts, ts), 1)
    same = (r_i >> 6) == (c_i >> 6)
    u_cum = jnp.where(jnp.logical_and(same, r_i <= c_i), 1.0, 0.0).astype(BF16)
    u_tot = jnp.where(same, 1.0, 0.0).astype(BF16)
    r_p = lax.broadcasted_iota(I32, (pr, pr), 0)
    c_p = lax.broadcasted_iota(I32, (pr, pr), 1)
    same_p = (r_p >> 6) == (c_p >> 6)
    le = jnp.logical_and(same_p, c_p <= r_p)
    strict = jnp.logical_and(same_p, c_p < r_p)
    eye = jnp.where(r_p == c_p, 1.0, 0.0).astype(F32)

    gt = gates_ref[...].T
    zt = gt + dtb_ref[...]
    g_rows = -jnp.exp(alog_ref[...]) * (jnp.maximum(zt, 0.0) + jnp.log(1.0 + jnp.exp(-jnp.abs(zt))))
    gh, gm, gl = _split3(g_rows)
    gcum_rows = _dot(gh, u_cum) + _dot(gm, u_cum) + _dot(gl, u_cum)
    gtot_rows = _dot(gh, u_tot) + _dot(gm, u_tot) + _dot(gl, u_tot)
    gcum_cols = gcum_rows.T
    gtot_cols = gtot_rows.T
    beta_cols = _sigmoid(gates_ref[...])
    lane = lax.broadcasted_iota(I32, (1, LANES), 1)
    sub = lax.broadcasted_iota(I32, (LANES, 1), 0)
    csel = lax.broadcasted_iota(I32, (1, pr), 1) >> 6

    def conv_silu(c0):
        cols = pl.ds(pl.multiple_of(c0, dk), dk)
        y = (cw_ref[0:1, cols] * ext_ref[pl.ds(5, ts), cols] + cw_ref[1:2, cols] * ext_ref[pl.ds(6, ts), cols]
             + cw_ref[2:3, cols] * ext_ref[pl.ds(7, ts), cols] + cw_ref[3:4, cols] * ext_ref[pl.ds(8, ts), cols])
        return y * _sigmoid(y)

    def head_group(hg, carry):
        heads = [hg * C_GROUP + i for i in range(C_GROUP)]
        hv = []
        for h in heads:
            c0 = pl.multiple_of(h * dk, dk)
            pick_a = lane == (C_HEADS + h)
            gcol = jnp.sum(jnp.where(pick_a, gcum_cols, 0.0), axis=1, keepdims=True)
            glcol = jnp.sum(jnp.where(pick_a, gtot_cols, 0.0), axis=1, keepdims=True)
            bcol = jnp.sum(jnp.where(lane == h, beta_cols, 0.0), axis=1, keepdims=True)
            pick_r = sub == (C_HEADS + h)
            grow = jnp.sum(jnp.where(pick_r, gcum_rows, 0.0), axis=0, keepdims=True)
            glrow = jnp.sum(jnp.where(pick_r, gtot_rows, 0.0), axis=0, keepdims=True)
            q = conv_silu(c0)
            k = conv_silu(c0 + hd)
            v = conv_silu(c0 + 2 * hd)
            q = q * lax.rsqrt(jnp.sum(q * q, axis=-1, keepdims=True) + RMS_EPS) * (dk ** -0.5)
            k = k * lax.rsqrt(jnp.sum(k * k, axis=-1, keepdims=True) + RMS_EPS)
            eg = jnp.exp(gcol)
            kb = k * bcol
            hv.append(dict(
                c0=c0, gcol=gcol, grow=grow, glrow=glrow, kbf=k.astype(BF16), kbb=kb.astype(BF16),
                qbf=q.astype(BF16), rhs=jnp.concatenate([v * bcol, kb * eg], axis=1).astype(BF16),
                qe=q * eg, ktil_t=(k * jnp.exp(glcol - gcol)).T.astype(BF16)))

        inst = [(i, p) for i in range(C_GROUP) for p in range(ts // pr)]
        mpow, tinv, intra = {}, {}, {}
        for (i, p) in inst:
            d_ = hv[i]
            rp = slice(p * pr, (p + 1) * pr)
            decay = jnp.exp(jnp.where(le, d_["gcol"][rp] - d_["grow"][:, rp], NEG))
            lower = jnp.where(strict, _dot_nt(d_["kbb"][rp], d_["kbf"][rp]) * decay, 0.0)
            intra[i, p] = (_dot_nt(d_["qbf"][rp], d_["kbf"][rp]) * decay).astype(BF16)
            mpow[i, p] = -lower
            tinv[i, p] = eye - lower
        for _ in range(5):
            for key in inst:
                mb = mpow[key].astype(BF16)
                mpow[key] = _dot(mb, mb)
            for key in inst:
                tinv[key] = tinv[key] + _dot(tinv[key].astype(BF16), mpow[key].astype(BF16))
        wv_, kc = {}, {}
        for (i, p) in inst:
            wk = _dot(tinv[i, p].astype(BF16), hv[i]["rhs"][p * pr:(p + 1) * pr])
            wv_[i, p] = wk[:, :dk]
            kc[i, p] = wk[:, dk:]

        st = []
        for i, h in enumerate(heads):
            vnew_ref[i] = jnp.zeros(vnew_ref.shape[1:], vnew_ref.dtype)
            st.append(state_ref[h])
        outs = [[] for _ in heads]
        for j in range(nch):
            p, jj = j // 2, j % 2
            rows = slice(j * C_CHUNK, (j + 1) * C_CHUNK)
            lrows = slice(jj * C_CHUNK, (jj + 1) * C_CHUNK)
            rp = slice(p * pr, (p + 1) * pr)
            a1 = [_dot(jnp.concatenate([kc[i, p][lrows], hv[i]["qe"][rows]], axis=0).astype(BF16),
                       st[i].astype(BF16)) for i in range(C_GROUP)]
            for i in range(C_GROUP):
                vnew_ref[i, rows, :] = (wv_[i, p][lrows] - a1[i][:C_CHUNK]).astype(BF16)
            for i in range(C_GROUP):
                vn_pair = vnew_ref[i, rp, :]
                outs[i].append(a1[i][C_CHUNK:] + _dot(intra[i, p][lrows, :], vn_pair))
                kt_j = jnp.where(csel == jj, hv[i]["ktil_t"][:, rp], jnp.zeros((), BF16))
                dg = jnp.exp(jnp.sum(jnp.where(csel == jj, hv[i]["glrow"][:, rp], 0.0), axis=1, keepdims=True)
                             * (1.0 / C_CHUNK))
                st[i] = st[i] * dg + _dot(kt_j, vn_pair)
        for i, h in enumerate(heads):
            state_ref[h] = st[i]
            o = jnp.concatenate(outs[i], axis=0)
            o = o * lax.rsqrt(jnp.mean(o * o, axis=-1, keepdims=True) + RMS_EPS) * nw_ref[...]
            c0 = hv[i]["c0"]
            z = qkvz_ref[:, pl.ds(pl.multiple_of(c0 + 3 * hd, dk), dk)].astype(F32)
            o_ref[:, pl.ds(c0, dk)] = (o * (z * _sigmoid(z))).astype(o_ref.dtype)
        return carry

    lax.fori_loop(0, C_HEADS // C_GROUP, head_group, 0)
    ext_ref[0:8, :] = ext_ref[ts:ts + 8, :]


def _mixer_c(x2, w_in, conv_w, a_log, dt_bias, norm_w, w_out, g, b, bsz, seq):
    t, d = x2.shape
    hd = C_HEADS * C_HEAD_DIM
    qkvz = _proj(x2, w_in[:, :4 * hd].astype(BF16), BF16)
    wg = jnp.zeros((d, LANES), F32).at[:, :2 * C_HEADS].set(w_in[:, 4 * hd:])
    gates = _proj3(x2, wg)
    col = lambda v: jnp.zeros((LANES, 1), F32).at[C_HEADS:2 * C_HEADS, 0].set(v.astype(F32))
    nsteps = seq // C_SUPER
    o = pl.pallas_call(
        _delta_body,
        grid=(bsz, nsteps),
        in_specs=[pl.BlockSpec((C_SUPER, 4 * hd), lambda bi, i: (bi * nsteps + i, 0)),
                  pl.BlockSpec((C_SUPER, LANES), lambda bi, i: (bi * nsteps + i, 0)),
                  _const_spec((C_CONV, 3 * hd)), _const_spec((LANES, 1)), _const_spec((LANES, 1)),
                  _const_spec((1, C_HEAD_DIM))],
        out_specs=pl.BlockSpec((C_SUPER, hd), lambda bi, i: (bi * nsteps + i, 0)),
        out_shape=jax.ShapeDtypeStruct((t, hd), BF16),
        scratch_shapes=[pltpu.VMEM((C_SUPER + 8, 3 * hd), F32),
                        pltpu.VMEM((C_HEADS, C_HEAD_DIM, C_HEAD_DIM), F32),
                        pltpu.VMEM((C_GROUP, C_SUPER, C_HEAD_DIM), BF16)],
        compiler_params=_cparams(2),
        name="deltanet",
    )(qkvz, gates, conv_w.astype(F32), col(a_log), col(dt_bias), norm_w.reshape(1, -1).astype(F32))
    return _mm_res_ln(o, w_out.astype(BF16), x2, g, b)


def _route_body(x_ref, wh_ref, wl_ref, o_ref, g1_ref, g2_ref, cnt_ref, run_ref, *, tm):
    @pl.when(pl.program_id(0) == 0)
    def _():
        run_ref[...] = jnp.zeros_like(run_ref)

    x = x_ref[...]
    xh = x.astype(BF16)
    xl = (x - xh.astype(F32)).astype(BF16)
    wh = wh_ref[...]
    logits = _dot(xh, wh) + _dot(xl, wh) + _dot(xh, wl_ref[...])
    lane = lax.broadcasted_iota(I32, (1, LANES), 1)
    lane_f = lane.astype(F32)

    def top1(vals):
        m = jnp.max(vals, axis=-1, keepdims=True)
        idx = jnp.min(jnp.where(vals == m, lane_f, 1e9), axis=-1, keepdims=True)
        return m, idx.astype(I32)

    lc = jnp.where(lane < MOE_GROUPS, logits, NEG)
    mc, grp = top1(lc)
    p_grp = 1.0 / jnp.sum(jnp.exp(lc - mc), axis=-1, keepdims=True)
    lo = MOE_GROUPS + MOE_PER_GROUP * grp
    lf = jnp.where(jnp.logical_and(lane >= lo, lane < lo + MOE_PER_GROUP), logits, NEG)
    m1, i1 = top1(lf)
    m2, i2 = top1(jnp.where(lane == i1, NEG, lf))
    e21 = jnp.exp(m2 - m1)
    g1 = p_grp / (1.0 + e21)
    g2 = p_grp * e21 / (1.0 + e21)
    ex1 = i1 - MOE_GROUPS
    ex2 = i2 - MOE_GROUPS

    oh1 = jnp.where(lane == ex1, 1.0, 0.0).astype(F32)
    oh2 = jnp.where(lane == ex2, 1.0, 0.0).astype(F32)
    ohs = oh1 + oh2
    r_i = lax.broadcasted_iota(I32, (tm, tm), 0)
    c_i = lax.broadcasted_iota(I32, (tm, tm), 1)
    before = _dot(jnp.where(c_i < r_i, 1.0, 0.0).astype(BF16), ohs.astype(BF16)) + run_ref[...]
    rank1 = jnp.sum(oh1 * before, axis=-1, keepdims=True)
    rank2 = jnp.sum(oh2 * before, axis=-1, keepdims=True)
    run_ref[...] = run_ref[...] + jnp.sum(ohs, axis=0, keepdims=True)

    out = jnp.where(lane == 0, ex1.astype(F32), 0.0)
    out = jnp.where(lane == 1, ex2.astype(F32), out)
    out = jnp.where(lane == 2, g1, out)
    out = jnp.where(lane == 3, g2, out)
    out = jnp.where(lane == 4, rank1, out)
    out = jnp.where(lane == 5, rank2, out)
    o_ref[...] = out
    g1_ref[...] = jnp.broadcast_to(g1, g1_ref.shape)
    g2_ref[...] = jnp.broadcast_to(g2, g2_ref.shape)
    cnt_ref[...] = jnp.broadcast_to(run_ref[...], cnt_ref.shape)


def _route(x2, w_coarse, w_fine, tm=512):
    t, d = x2.shape
    wr = jnp.zeros((d, LANES), F32)
    wr = wr.at[:, :MOE_GROUPS].set(w_coarse).at[:, MOE_GROUPS:MOE_GROUPS + MOE_EXPERTS].set(
        w_fine.reshape(d, MOE_EXPERTS))
    wh = wr.astype(BF16)
    wl = (wr - wh.astype(F32)).astype(BF16)
    return pl.pallas_call(
        functools.partial(_route_body, tm=tm),
        grid=(t // tm,),
        in_specs=[pl.BlockSpec((tm, d), lambda i: (i, 0)), _const_spec((d, LANES)), _const_spec((d, LANES))],
        out_specs=[pl.BlockSpec((tm, LANES), lambda i: (i, 0)), pl.BlockSpec((tm, LANES), lambda i: (i, 0)),
                   pl.BlockSpec((tm, LANES), lambda i: (i, 0)), pl.BlockSpec((8, LANES), lambda i: (0, 0))],
        out_shape=[jax.ShapeDtypeStruct((t, LANES), F32), jax.ShapeDtypeStruct((t, LANES), F32),
                   jax.ShapeDtypeStruct((t, LANES), F32), jax.ShapeDtypeStruct((8, LANES), F32)],
        scratch_shapes=[pltpu.VMEM((1, LANES), F32)],
        compiler_params=_cparams(1),
        name="moe_route",
    )(x2, wh, wl)


def _dispatch_body(p1_ref, p2_ref, x_ref, o_hbm, slab_ref, sem, *, tm):
    s = pl.program_id(0)
    i = pl.program_id(1)

    @pl.when(i == 0)
    def _():
        slab_ref[...] = jnp.zeros_like(slab_ref)

    def row(r, carry):
        v = x_ref[pl.ds(r, 1), :]
        slab_ref[pl.ds(p1_ref[0, r], 1), :] = v
        slab_ref[pl.ds(p2_ref[0, r], 1), :] = v
        return carry
    lax.fori_loop(0, tm, row, 0, unroll=ROW_UNROLL)

    @pl.when(i == pl.num_programs(1) - 1)
    def _():
        cp = pltpu.make_async_copy(slab_ref, o_hbm.at[s], sem)
        cp.start()
        cp.wait()


def _dispatch(xp, pos1, pos2, s_pad, tm=1024):
    t, w = xp.shape
    nslab = w // LANES
    nt = t // tm
    return pl.pallas_call(
        functools.partial(_dispatch_body, tm=tm),
        grid=(nslab, nt),
        in_specs=[pl.BlockSpec((None, 1, tm), lambda s, i: (i, 0, 0), memory_space=pltpu.SMEM),
                  pl.BlockSpec((None, 1, tm), lambda s, i: (i, 0, 0), memory_space=pltpu.SMEM),
                  pl.BlockSpec((tm, LANES), lambda s, i: (i, s))],
        out_specs=pl.BlockSpec(memory_space=pl.ANY),
        out_shape=jax.ShapeDtypeStruct((nslab, s_pad, LANES), U32),
        scratch_shapes=[pltpu.VMEM((s_pad, LANES), U32), pltpu.SemaphoreType.DMA(())],
        compiler_params=_cparams(2),
        name="moe_dispatch",
    )(pos1.reshape(nt, 1, tm), pos2.reshape(nt, 1, tm), xp)


def _expert_body(te_ref, nv_ref, xs_ref, wg_ref, wu_ref, wd_ref, y_ref, wgb_ref, wub_ref, wdb_ref):
    i = pl.program_id(0)
    half = D_MODEL // 2
    nslab = half // LANES

    @pl.when(i < nv_ref[0])
    def _():
        changed = jnp.logical_or(i == 0, te_ref[i] != te_ref[jnp.maximum(i - 1, 0)])

        @pl.when(changed)
        def _():
            wgb_ref[...] = wg_ref[...].astype(BF16)
            wub_ref[...] = wu_ref[...].astype(BF16)
            wdb_ref[...] = wd_ref[...].astype(BF16)

        xlo = jnp.concatenate([_unpack_lo(xs_ref[s]) for s in range(nslab)], axis=1).astype(BF16)
        xhi = jnp.concatenate([_unpack_hi(xs_ref[s]) for s in range(nslab)], axis=1).astype(BF16)
        hg =_dot(xlo, wgb_ref[0:half, :]) + _dot(xhi, wgb_ref[half:, :])
        hu = _dot(xlo, wub_ref[0:half, :]) + _dot(xhi, wub_ref[half:, :])
        h = (hg * _sigmoid(hg) * hu).astype(BF16)
        packed = _pack_pairs(_dot(h, wdb_ref[...]))
        for s in range(nslab):
            y_ref[s] = packed[:, s * LANES:(s + 1) * LANES]

    @pl.when(i >= nv_ref[0])
    def _():
        y_ref[...] = jnp.zeros_like(y_ref)


def _experts(xs, tile_expert, nvalid, w_gate, w_up, w_down, layer, n_tiles):
    tm = MOE_TM
    d, hdn = D_MODEL, MOE_HIDDEN
    nslab = d // 2 // LANES
    return pl.pallas_call(
        _expert_body,
        grid_spec=pltpu.PrefetchScalarGridSpec(
            num_scalar_prefetch=2,
            grid=(n_tiles,),
            in_specs=[pl.BlockSpec((nslab, tm, LANES), lambda i, te, nv: (0, i, 0)),
                      pl.BlockSpec((None, None, d, hdn), lambda i, te, nv: (layer, te[i], 0, 0)),
                      pl.BlockSpec((None, None, d, hdn), lambda i, te, nv: (layer, te[i], 0, 0)),
                      pl.BlockSpec((None, None, hdn, d), lambda i, te, nv: (layer, te[i], 0, 0))],
            out_specs=pl.BlockSpec((nslab, tm, LANES), lambda i, te, nv: (0, i, 0)),
            scratch_shapes=[pltpu.VMEM((d, hdn), BF16), pltpu.VMEM((d, hdn), BF16), pltpu.VMEM((hdn, d), BF16)],
        ),
        out_shape=jax.ShapeDtypeStruct((nslab, n_tiles * tm, LANES), U32),
        compiler_params=_cparams(1),
        name="moe_experts",
    )(tile_expert, nvalid, xs, w_gate, w_up, w_down)


def _combine_body(p1_ref, p2_ref, y_ref, g1_ref, g2_ref, lo_ref, hi_ref, b1_ref, b2_ref, *, tm):
    def row(r, carry):
        b1_ref[pl.ds(r, 1), :] = y_ref[pl.ds(p1_ref[0, r], 1), :]
        b2_ref[pl.ds(r, 1), :] = y_ref[pl.ds(p2_ref[0, r], 1), :]
        return carry
    lax.fori_loop(0, tm, row, 0, unroll=ROW_UNROLL)
    g1 = g1_ref[...]
    g2 = g2_ref[...]
    y1 = b1_ref[...]
    y2 = b2_ref[...]
    lo_ref[...] = g1 * _unpack_lo(y1) + g2 * _unpack_lo(y2)
    hi_ref[...] = g1 * _unpack_hi(y1) + g2 * _unpack_hi(y2)


def _combine(y, pos1, pos2, g1b, g2b, tm=1024):
    nslab, s_pad, _ = y.shape
    t = pos1.shape[0]
    nt = t // tm
    return pl.pallas_call(
        functools.partial(_combine_body, tm=tm),
        grid=(nslab, nt),
        in_specs=[pl.BlockSpec((None, 1, tm), lambda s, i: (i, 0, 0), memory_space=pltpu.SMEM),
                  pl.BlockSpec((None, 1, tm), lambda s, i: (i, 0, 0), memory_space=pltpu.SMEM),
                  pl.BlockSpec((None, s_pad, LANES), lambda s, i: (s, 0, 0), pipeline_mode=pl.Buffered(1)),
                  pl.BlockSpec((tm, LANES), lambda s, i: (i, 0)), pl.BlockSpec((tm, LANES), lambda s, i: (i, 0))],
        out_specs=[pl.BlockSpec((tm, LANES), lambda s, i: (i, s)), pl.BlockSpec((tm, LANES), lambda s, i: (i, s))],
        out_shape=[jax.ShapeDtypeStruct((t, nslab * LANES), F32), jax.ShapeDtypeStruct((t, nslab * LANES), F32)],
        scratch_shapes=[pltpu.VMEM((tm, LANES), U32), pltpu.VMEM((tm, LANES), U32)],
        compiler_params=_cparams(2),
        name="moe_combine",
    )(pos1.reshape(nt, 1, tm), pos2.reshape(nt, 1, tm), y, g1b, g2b)


def _res_ln_body(x_ref, lo_ref, hi_ref, g_ref, b_ref, o_ref):
    h = jnp.concatenate([lo_ref[...], hi_ref[...]], axis=1)
    o_ref[...] = _ln_rows(ALPHA * x_ref[...] + h, g_ref[...], b_ref[...])


def _res_ln(x2, lo, hi, g, b, tm=1024):
    t, d = x2.shape
    return pl.pallas_call(
        _res_ln_body,
        grid=(t // tm,),
        in_specs=[pl.BlockSpec((tm, d), lambda i: (i, 0)), pl.BlockSpec((tm, d // 2), lambda i: (i, 0)),
                  pl.BlockSpec((tm, d // 2), lambda i: (i, 0)), _const_spec((1, d)), _const_spec((1, d))],
        out_specs=pl.BlockSpec((tm, d), lambda i: (i, 0)),
        out_shape=jax.ShapeDtypeStruct((t, d), F32),
        compiler_params=_cparams(1),
        name="res_ln",
    )(x2, lo, hi, g.reshape(1, d), b.reshape(1, d))


def _moe(x2, xp, w_coarse, w_fine, w_gate, w_up, w_down, layer, g, b):
    t = x2.shape[0]
    tm = MOE_TM
    n_tiles = (2 * t) // tm + MOE_EXPERTS
    route, g1b, g2b, cnt = _route(x2, w_coarse, w_fine)
    ex = route[:, 0:2].astype(I32)
    rank = route[:, 4:6].astype(I32)
    counts = cnt[0, :MOE_EXPERTS].astype(I32)
    ptiles = (counts + tm - 1) // tm
    tile_end = jnp.cumsum(ptiles)
    pstart = (tile_end - ptiles) * tm
    eids = jnp.arange(MOE_EXPERTS, dtype=I32)
    pos = jnp.sum(jnp.where(ex[:, :, None] == eids, pstart, 0), axis=-1) + rank
    nvalid = tile_end[-1:].astype(I32)
    tidx = jnp.arange(n_tiles, dtype=I32)
    te = jnp.sum((tile_end[None, :] <= jnp.minimum(tidx, nvalid[0] - 1)[:, None]).astype(I32), axis=1)
    xs = _dispatch(xp, pos[:, 0], pos[:, 1], n_tiles * tm)
    y = _experts(xs, te, nvalid, w_gate, w_up, w_down, layer, n_tiles)
    lo, hi = _combine(y, pos[:, 0], pos[:, 1], g1b, g2b)
    return _res_ln(x2, lo, hi, g, b)


def kernel(x, rel_bias, a_w_in, a_w_out, b_w_in, b_norm_g, b_norm_b, b_w_s, b_b_s, b_w_out, c_w_in, c_conv,
           c_a_log, c_dt_bias, c_norm_w, c_w_out, ln_g, ln_b, moe_w_coarse, moe_w_fine, moe_w_gate, moe_w_up,
           moe_w_down):
    bsz, seq, d = x.shape
    x2 = x.reshape(bsz * seq, d)
    for i in range(DEPTH):
        kind, j = i % 3, i // 3
        g1, b1 = ln_g[i, 0], ln_b[i, 0]
        if kind == 0:
            x2, xp = _mixer_a(x2, a_w_in[j], a_w_out[j], rel_bias, g1, b1, bsz, seq)
        elif kind == 1:
            x2, xp = _mixer_b(x2, b_w_in[j], b_norm_g[j], b_norm_b[j], b_w_s[j], b_b_s[j], b_w_out[j], g1, b1)
        else:
            x2, xp = _mixer_c(x2, c_w_in[j], c_conv[j], c_a_log[j], c_dt_bias[j], c_norm_w[j], c_w_out[j],
                              g1, b1, bsz, seq)
        x2 = _moe(x2, xp, moe_w_coarse[i], moe_w_fine[i], moe_w_gate, moe_w_up, moe_w_down, i,
                  ln_g[i, 1], ln_b[i, 1])
    return x2.reshape(bsz, seq, d)
```

```python
import functools
import math

import numpy as np
import jax
import jax.numpy as jnp
from jax import lax
from jax.experimental import pallas as pl
from jax.experimental.pallas import tpu as pltpu

F32 = jnp.float32
BF16 = jnp.bfloat16
U32 = jnp.uint32
I32 = jnp.int32

D_MODEL = 1024
DEPTH = 4
A_GROUPS = ((128, 1), (512, 4), (2048, 16))
A_HEADS = 16
A_HEAD_DIM = 64
A_BLOCK = 128
NUM_BUCKETS = 32
MAX_DISTANCE = 2048
B_CHUNK = 128
B_WIDTH = 2 * D_MODEL
B_GROUPS = 16
C_HEADS = 8
C_HEAD_DIM = 128
C_CONV = 4
C_CHUNK = 64
MOE_GROUPS = 8
MOE_PER_GROUP = 8
MOE_EXPERTS = 64
MOE_HIDDEN = 512
LN_EPS = 1e-5
RMS_EPS = 1e-6
ALPHA = (2 * DEPTH) ** 0.25

LANES = 128
NEG = -1e30
VMEM_LIMIT = 56 * 1024 * 1024
MOE_TM = 512
ROW_UNROLL = 16
C_SUPER = 256
C_GROUP = 4


def _cparams(n_axes, vmem=VMEM_LIMIT):
    return pltpu.CompilerParams(dimension_semantics=("arbitrary",) * n_axes, vmem_limit_bytes=vmem)


def _const_spec(shape):
    nd = len(shape)
    return pl.BlockSpec(shape, lambda *_: (0,) * nd, pipeline_mode=pl.Buffered(1))


def _ln_rows(y, g, b):
    mu = jnp.mean(y, axis=-1, keepdims=True)
    yc = y - mu
    var = jnp.mean(yc * yc, axis=-1, keepdims=True)
    return yc * lax.rsqrt(var + LN_EPS) * g + b


def _pack_pairs(y):
    w = y.shape[1] // 2
    lo = lax.bitcast_convert_type(y[:, :w].astype(BF16).astype(F32), U32)
    hi = lax.bitcast_convert_type(y[:, w:].astype(BF16).astype(F32), U32)
    return (lo >> 16) | (hi & jnp.uint32(0xFFFF0000))


def _unpack_lo(p):
    return lax.bitcast_convert_type(p << 16, F32)


def _unpack_hi(p):
    return lax.bitcast_convert_type(p & jnp.uint32(0xFFFF0000), F32)


def _split3(a):
    h = a.astype(BF16)
    r = a - h.astype(F32)
    m = r.astype(BF16)
    l = (r - m.astype(F32)).astype(BF16)
    return h, m, l


def _dot(a, b):
    return jnp.dot(a, b, preferred_element_type=F32)


def _dot_nt(a, b):
    return lax.dot_general(a, b, (((1,), (1,)), ((), ())), preferred_element_type=F32)


def _gelu_tanh(x):
    return 0.5 * x * (1.0 + jnp.tanh(0.7978845608028654 * (x + 0.044715 * (x * x * x))))


def _sigmoid(x):
    return 0.5 * jnp.tanh(0.5 * x) + 0.5


def _proj_body(x_ref, w_ref, o_ref, xb_ref):
    @pl.when(pl.program_id(1) == 0)
    def _():
        xb_ref[...] = x_ref[...].astype(BF16)

    o_ref[...] = _dot(xb_ref[...], w_ref[...]).astype(o_ref.dtype)


def _proj(x2, w, out_dtype, tm=1024, tn=1024):
    t, k = x2.shape
    n = w.shape[1]
    tn = min(tn, n)
    return pl.pallas_call(
        _proj_body,
        grid=(t // tm, n // tn),
        in_specs=[pl.BlockSpec((tm, k), lambda i, j: (i, 0)),
                  pl.BlockSpec((k, tn), lambda i, j: (0, j))],
        out_specs=pl.BlockSpec((tm, tn), lambda i, j: (i, j)),
        out_shape=jax.ShapeDtypeStruct((t, n), out_dtype),
        scratch_shapes=[pltpu.VMEM((tm, k), BF16)],
        compiler_params=_cparams(2),
        name="proj",
    )(x2, w)


def _proj3_body(x_ref, wh_ref, wl_ref, o_ref):
    x = x_ref[...]
    xh = x.astype(BF16)
    xl = (x - xh.astype(F32)).astype(BF16)
    wh = wh_ref[...]
    o_ref[...] = _dot(xh, wh) + _dot(xl, wh) + _dot(xh, wl_ref[...])


def _proj3(x2, w, tm=1024):
    t, k = x2.shape
    n = w.shape[1]
    wh = w.astype(BF16)
    wl = (w - wh.astype(F32)).astype(BF16)
    return pl.pallas_call(
        _proj3_body,
        grid=(t // tm,),
        in_specs=[pl.BlockSpec((tm, k), lambda i: (i, 0)), _const_spec((k, n)), _const_spec((k, n))],
        out_specs=pl.BlockSpec((tm, n), lambda i: (i, 0)),
        out_shape=jax.ShapeDtypeStruct((t, n), F32),
        compiler_params=_cparams(1),
        name="proj3",
    )(x2, wh, wl)


def _mm_res_ln_body(a_ref, w_ref, x_ref, g_ref, b_ref, o_ref, p_ref):
    y = _dot(a_ref[...], w_ref[...])
    xn = _ln_rows(ALPHA * x_ref[...] + y, g_ref[...], b_ref[...])
    o_ref[...] = xn
    p_ref[...] = _pack_pairs(xn)


def _mm_res_ln(a, w, x2, g, b, tm=512):
    t, k = a.shape
    d = w.shape[1]
    return pl.pallas_call(
        _mm_res_ln_body,
        grid=(t // tm,),
        in_specs=[pl.BlockSpec((tm, k), lambda i: (i, 0)), _const_spec((k, d)),
                  pl.BlockSpec((tm, d), lambda i: (i, 0)), _const_spec((1, d)), _const_spec((1, d))],
        out_specs=[pl.BlockSpec((tm, d), lambda i: (i, 0)), pl.BlockSpec((tm, d // 2), lambda i: (i, 0))],
        out_shape=[jax.ShapeDtypeStruct((t, d), F32), jax.ShapeDtypeStruct((t, d // 2), U32)],
        compiler_params=_cparams(1),
        name="mm_res_ln",
    )(a, w, x2, g.reshape(1, d), b.reshape(1, d))


def _t5_bucket(dist):
    max_exact = NUM_BUCKETS // 2
    d = jnp.maximum(dist, 1).astype(F32)
    large = max_exact + (jnp.log(d / max_exact) / math.log(MAX_DISTANCE / max_exact)
                         * (NUM_BUCKETS - max_exact)).astype(I32)
    return jnp.where(dist < max_exact, dist, jnp.minimum(large, NUM_BUCKETS - 1))


def _attn_bias(rel_bias, window, dil):
    steps = window // dil
    qi = jnp.arange(A_BLOCK)[:, None]
    ki = jnp.arange(2 * A_BLOCK)[None, :]
    rel = qi + A_BLOCK - ki
    valid = (rel >= 0) & (rel <= steps)
    bucket = _t5_bucket(jnp.maximum(rel, 0) * dil)
    bias = jnp.zeros((A_HEADS, A_BLOCK, 2 * A_BLOCK), F32)
    for bkt in range(NUM_BUCKETS):
        bias = jnp.where((bucket == bkt)[None], rel_bias[bkt].astype(F32)[:, None, None], bias)
    return jnp.where(valid[None], bias, NEG)


def _attn_body(q_ref, k_ref, v_ref, kp_ref, vp_ref, bias_ref, o_ref, lse_ref, *, nres, nblk):
    li = pl.program_id(2)
    lane = lax.broadcasted_iota(I32, (1, LANES), 1)
    col2 = lax.broadcasted_iota(I32, (1, 2 * A_BLOCK), 1)
    first_pen = jnp.where(col2 < A_BLOCK, jnp.where(li == 0, NEG, 0.0).astype(F32), 0.0)
    head0 = lane < A_HEAD_DIM
    lse_ref[...] = jnp.zeros_like(lse_ref)

    def hp_body(hp, carry):
        c0 = pl.multiple_of(hp * LANES, LANES)
        cols = pl.ds(c0, LANES)
        blocks = [(r, j) for r in range(nres) for j in range(nblk)]
        bias2 = jnp.concatenate([bias_ref[2 * hp], bias_ref[2 * hp + 1]], axis=0)
        vbs, s_ = {}, {}
        for (r, j) in blocks:
            rows = pl.ds(j * A_BLOCK, A_BLOCK)
            q = q_ref[r, rows, cols]
            if j == 0:
                kb = jnp.concatenate([kp_ref[r, :, cols], k_ref[r, rows, cols]], axis=0)
                vbs[r, j] = jnp.concatenate([vp_ref[r, :, cols], v_ref[r, rows, cols]], axis=0)
            else:
                band = pl.ds((j - 1) * A_BLOCK, 2 * A_BLOCK)
                kb = k_ref[r, band, cols]
                vbs[r, j] = v_ref[r, band, cols]
            zero = jnp.zeros_like(q)
            q2 = jnp.concatenate([jnp.where(head0, q, zero), jnp.where(head0, zero, q)], axis=0)
            s = _dot_nt(q2, kb) + bias2
            s_[r, j] = s + first_pen if j == 0 else s
        p_, l_, lse_ = {}, {}, {}
        for u in blocks:
            m = jnp.max(s_[u], axis=-1, keepdims=True)
            p = jnp.exp(s_[u] - m)
            l_[u] = jnp.sum(p, axis=-1, keepdims=True)
            p_[u] = p.astype(BF16)
            lse_[u] = m + jnp.log(l_[u])
        pv = {u: _dot(p_[u], vbs[u]) / l_[u] for u in blocks}
        for (r, j) in blocks:
            rows = pl.ds(j * A_BLOCK, A_BLOCK)
            o_ref[r, rows, cols] = jnp.where(head0, pv[r, j][:A_BLOCK], pv[r, j][A_BLOCK:]).astype(o_ref.dtype)
            cur = lse_ref[r, rows, :]
            cur = jnp.where(lane == 2 * hp, lse_[r, j][:A_BLOCK], cur)
            cur = jnp.where(lane == 2 * hp + 1, lse_[r, j][A_BLOCK:], cur)
            lse_ref[r, rows, :] = cur
        return carry

    lax.fori_loop(0, A_HEADS // 2, hp_body, 0)


def _proj_perm_body(x_ref, w_ref, o_ref, xb_ref, y_ref, *, dil):
    tm, k = x_ref.shape
    n = tm // dil

    @pl.when(pl.program_id(1) == 0)
    def _():
        if dil == 1:
            xb_ref[...] = x_ref[...].astype(BF16)
        else:
            nc = k // LANES
            for c in range(nc):
                y_ref[c] = x_ref[:, c * LANES:(c + 1) * LANES]
            for r in range(dil):
                xb_ref[r * n:(r + 1) * n, :] = jnp.concatenate(
                    [y_ref[c, pl.ds(r, n, stride=dil), :] for c in range(nc)], axis=1).astype(BF16)

    cw = 1024
    for c in range(w_ref.shape[1] // cw):
        y = _dot(xb_ref[...], w_ref[:, c * cw:(c + 1) * cw])
        for r in range(dil):
            o_ref[r, :, c * cw:(c + 1) * cw] = y[r * n:(r + 1) * n].astype(o_ref.dtype)


def _proj_perm(x2, w, g, dil, bsz, seq, tm=1024, tn=3072):
    t, k = x2.shape
    hd = A_HEADS * A_HEAD_DIM
    tpb = seq // tm
    return pl.pallas_call(
        functools.partial(_proj_perm_body, dil=dil),
        grid=(t // tm, 3 * hd // tn),
        in_specs=[pl.BlockSpec((tm, k), lambda i, j: (i, 0)),
                  pl.BlockSpec((k, tn), lambda i, j: (0, g * (3 * hd // tn) + j))],
        out_specs=pl.BlockSpec((None, dil, tm // dil, tn), lambda i, j: (i // tpb, 0, i % tpb, j)),
        out_shape=jax.ShapeDtypeStruct((bsz, dil, seq // dil, 3 * hd), BF16),
        scratch_shapes=[pltpu.VMEM((tm, k), BF16), pltpu.VMEM((k // LANES, tm, LANES), F32)],
        compiler_params=_cparams(2),
        name=f"proj_d{dil}",
    )(x2, w)


def _dilated_group(qkv, dil, bias, bsz, seq):
    hd = A_HEADS * A_HEAD_DIM
    L = seq // dil
    rows_per_step = 512
    tl = min(rows_per_step, L)
    nblk = tl // A_BLOCK
    nres = rows_per_step // tl

    def blk(col):
        return pl.BlockSpec((None, nres, tl, hd), lambda b, r, li: (b, r, li, col))

    def prev(col):
        return pl.BlockSpec((None, nres, A_BLOCK, hd),
                            lambda b, r, li: (b, r, jnp.maximum(li * nblk - 1, 0), col))

    return pl.pallas_call(
        functools.partial(_attn_body, nres=nres, nblk=nblk),
        grid=(bsz, dil // nres, L // tl),
        in_specs=[blk(0), blk(1), blk(2), prev(1), prev(2), _const_spec((A_HEADS, A_BLOCK, 2 * A_BLOCK))],
        out_specs=[pl.BlockSpec((None, nres, tl, hd), lambda b, r, li: (b, r, li, 0)),
                   pl.BlockSpec((None, nres, tl, LANES), lambda b, r, li: (b, r, li, 0))],
        out_shape=[jax.ShapeDtypeStruct((bsz, dil, L, hd), BF16),
                   jax.ShapeDtypeStruct((bsz, dil, L, LANES), F32)],
        compiler_params=_cparams(3),
        name=f"dilated_attn_d{dil}",
    )(qkv, qkv, qkv, qkv, qkv, bias)


def _attn_out_body(o1_ref, o2_ref, o3_ref, l1_ref, l2_ref, l3_ref, e_ref, w_ref, x_ref, g_ref, b_ref,
                   xo_ref, p_ref, so_ref, sl2_ref, sl3_ref, *, dils):
    def natural(ref, scr, dil):
        if dil == 1:
            return ref[0].astype(F32)
        n = ref.shape[1]
        nc = ref.shape[2] // LANES
        for r in range(dil):
            blk = ref[r].astype(F32)
            for c in range(nc):
                scr[c, pl.ds(r, n, stride=dil), :] = blk[:, c * LANES:(c + 1) * LANES]
        return jnp.concatenate([scr[c] for c in range(nc)], axis=1) if nc > 1 else scr[0]

    l1 = natural(l1_ref, None, dils[0])
    l2 = natural(l2_ref, sl2_ref, dils[1])
    l3 = natural(l3_ref, sl3_ref, dils[2])
    m = jnp.maximum(jnp.maximum(l1, l2), l3)
    e1, e2, e3 = jnp.exp(l1 - m), jnp.exp(l2 - m), jnp.exp(l3 - m)
    inv = 1.0 / (e1 + e2 + e3)
    ex = e_ref[...]

    def expand(wt):
        h = wt.astype(BF16)
        lo = (wt - h.astype(F32)).astype(BF16)
        return _dot(h, ex) + _dot(lo, ex)

    comb = expand(e1 * inv) * natural(o1_ref, None, dils[0])
    comb = comb + expand(e2 * inv) * natural(o2_ref, so_ref, dils[1])
    comb = comb + expand(e3 * inv) * natural(o3_ref, so_ref, dils[2])
    y = _dot(comb.astype(BF16), w_ref[...])
    xn = _ln_rows(ALPHA * x_ref[...] + y, g_ref[...], b_ref[...])
    xo_ref[...] = xn
    p_ref[...] = _pack_pairs(xn)


def _attn_out(os, lses, dils, w_out, x2, g, b, seq, tm=512):
    t, d = x2.shape
    tpb = seq // tm
    expand = np.zeros((LANES, d), np.float32)
    for h in range(A_HEADS):
        expand[h, h * A_HEAD_DIM:(h + 1) * A_HEAD_DIM] = 1.0
    row = lambda n: pl.BlockSpec((tm, n), lambda i: (i, 0))
    res = lambda dil, n: pl.BlockSpec((None, dil, tm // dil, n), lambda i: (i // tpb, 0, i % tpb, 0))
    return pl.pallas_call(
        functools.partial(_attn_out_body, dils=dils),
        grid=(t // tm,),
        in_specs=[res(dils[0], d), res(dils[1], d), res(dils[2], d),
                  res(dils[0], LANES), res(dils[1], LANES), res(dils[2], LANES),
                  _const_spec((LANES, d)), _const_spec((d, d)), row(d), _const_spec((1, d)), _const_spec((1, d))],
        out_specs=[row(d), row(d // 2)],
        out_shape=[jax.ShapeDtypeStruct((t, d), F32), jax.ShapeDtypeStruct((t, d // 2), U32)],
        scratch_shapes=[pltpu.VMEM((d // LANES, tm, LANES), F32), pltpu.VMEM((1, tm, LANES), F32),
                        pltpu.VMEM((1, tm, LANES), F32)],
        compiler_params=_cparams(1),
        name="attn_out",
    )(*os, *lses, jnp.asarray(expand, BF16), w_out, x2, g.reshape(1, d), b.reshape(1, d))


def _mixer_a(x2, w_in, w_out, rel_bias, g, b, bsz, seq):
    hd = A_HEADS * A_HEAD_DIM
    scale = np.ones((9 * hd,), np.float32)
    for gi in range(len(A_GROUPS)):
        scale[3 * gi * hd:(3 * gi + 1) * hd] = A_HEAD_DIM ** -0.5
    wb = (w_in * scale).astype(BF16)
    os, lses = [], []
    for gi, (window, dil) in enumerate(A_GROUPS):
        qkv = _proj_perm(x2, wb, gi, dil, bsz, seq)
        o, lse = _dilated_group(qkv, dil, _attn_bias(rel_bias, window, dil), bsz, seq)
        os.append(o)
        lses.append(lse)
    return _attn_out(os, lses, tuple(dl for _, dl in A_GROUPS), w_out.astype(BF16), x2, g, b, seq)


def _sgu_body(x_ref, wu_ref, wv_ref, ng_ref, nb_ref, wc_ref, bs_ref, a_ref, vb_ref, *, tm):
    xb = x_ref[...].astype(BF16)
    v = _gelu_tanh(_dot(xb, wv_ref[...]))
    vb_ref[...] = _ln_rows(v, ng_ref[...], nb_ref[...]).astype(BF16)
    gw = B_WIDTH // B_GROUPS
    ucols = 512
    for j in range(B_WIDTH // ucols):
        u = _gelu_tanh(_dot(xb, wu_ref[:, j * ucols:(j + 1) * ucols]))
        nchunk = tm // B_CHUNK
        for gg in range(ucols // gw):
            gi = j * (ucols // gw) + gg
            cols = slice(gi * gw, (gi + 1) * gw)
            vcat = jnp.concatenate([vb_ref[c * B_CHUNK:(c + 1) * B_CHUNK, cols] for c in range(nchunk)], axis=1)
            fcat = _dot(wc_ref[gi], vcat)
            for c in range(nchunk):
                rows = slice(c * B_CHUNK, (c + 1) * B_CHUNK)
                f = fcat[:, c * gw:(c + 1) * gw] + bs_ref[:, cols]
                a_ref[rows, cols] = (u[rows, gg * gw:(gg + 1) * gw] * f).astype(a_ref.dtype)


def _mixer_b(x2, w_in, norm_g, norm_b, w_s, b_s, w_out, g, b, tm=256):
    t, d = x2.shape
    wu = w_in[:, :B_WIDTH].astype(BF16)
    wv = w_in[:, B_WIDTH:].astype(BF16)
    wc = (w_s * jnp.tril(jnp.ones((B_CHUNK, B_CHUNK), w_s.dtype))).astype(BF16)
    bs_full = jnp.repeat(b_s.T, B_WIDTH // B_GROUPS, axis=1)
    a = pl.pallas_call(
        functools.partial(_sgu_body, tm=tm),
        grid=(t // tm,),
        in_specs=[pl.BlockSpec((tm, d), lambda i: (i, 0)), _const_spec((d, B_WIDTH)), _const_spec((d, B_WIDTH)),
                  _const_spec((1, B_WIDTH)), _const_spec((1, B_WIDTH)),
                  _const_spec((B_GROUPS, B_CHUNK, B_CHUNK)), _const_spec((B_CHUNK, B_WIDTH))],
        out_specs=pl.BlockSpec((tm, B_WIDTH), lambda i: (i, 0)),
        out_shape=jax.ShapeDtypeStruct((t, B_WIDTH), BF16),
        scratch_shapes=[pltpu.VMEM((tm, B_WIDTH), BF16)],
        compiler_params=_cparams(1),
        name="sgu",
    )(x2, wu, wv, norm_g.reshape(1, -1), norm_b.reshape(1, -1), wc, bs_full)
    return _mm_res_ln(a, w_out.astype(BF16), x2, g, b)


def _delta_body(qkvz_ref, gates_ref, cw_ref, alog_ref, dtb_ref, nw_ref, o_ref, ext_ref, state_ref, vnew_ref):
    ts = C_SUPER
    hd = C_HEADS * C_HEAD_DIM
    dk = C_HEAD_DIM
    nch = ts // C_CHUNK
    pr = 2 * C_CHUNK

    @pl.when(pl.program_id(1) == 0)
    def _():
        ext_ref[0:8, :] = jnp.zeros((8, 3 * hd), F32)
        state_ref[...] = jnp.zeros_like(state_ref)

    ext_ref[8:8 + ts, :] = qkvz_ref[:, 0:3 * hd].astype(F32)

    r_i = lax.broadcasted_iota(I32, (ts, ts), 0)
    c_i = lax.broadcasted_iota(I32, (ts, ts), 1)
    same = (r_i >> 6) == (c_i >> 6)
    u_cum = jnp.where(jnp.logical_and(same, r_i <= c_i), 1.0, 0.0).astype(BF16)
    u_tot = jnp.where(same, 1.0, 0.0).astype(BF16)
    r_p = lax.broadcasted_iota(I32, (pr, pr), 0)
    c_p = lax.broadcasted_iota(I32, (pr, pr), 1)
    same_p = (r_p >> 6) == (c_p >> 6)
    le = jnp.logical_and(same_p, c_p <= r_p)
    strict = jnp.logical_and(same_p, c_p < r_p)
    eye = jnp.where(r_p == c_p, 1.0, 0.0).astype(F32)

    gt = gates_ref[...].T
    zt = gt + dtb_ref[...]
    g_rows = -jnp.exp(alog_ref[...]) * (jnp.maximum(zt, 0.0) + jnp.log(1.0 + jnp.exp(-jnp.abs(zt))))
    gh, gm, gl = _split3(g_rows)
    gcum_rows = _dot(gh, u_cum) + _dot(gm, u_cum) + _dot(gl, u_cum)
    gtot_rows = _dot(gh, u_tot) + _dot(gm, u_tot) + _dot(gl, u_tot)
    gcum_cols = gcum_rows.T
    gtot_cols = gtot_rows.T
    beta_cols = _sigmoid(gates_ref[...])
    lane = lax.broadcasted_iota(I32, (1, LANES), 1)
    sub = lax.broadcasted_iota(I32, (LANES, 1), 0)
    csel = lax.broadcasted_iota(I32, (1, pr), 1) >> 6

    def conv_silu(c0):
        cols = pl.ds(pl.multiple_of(c0, dk), dk)
        y = (cw_ref[0:1, cols] * ext_ref[pl.ds(5, ts), cols] + cw_ref[1:2, cols] * ext_ref[pl.ds(6, ts), cols]
             + cw_ref[2:3, cols] * ext_ref[pl.ds(7, ts), cols] + cw_ref[3:4, cols] * ext_ref[pl.ds(8, ts), cols])
        return y * _sigmoid(y)

    def head_group(hg, carry):
        heads = [hg * C_GROUP + i for i in range(C_GROUP)]
        hv = []
        for h in heads:
            c0 = pl.multiple_of(h * dk, dk)
            pick_a = lane == (C_HEADS + h)
            gcol = jnp.sum(jnp.where(pick_a, gcum_cols, 0.0), axis=1, keepdims=True)
            glcol = jnp.sum(jnp.where(pick_a, gtot_cols, 0.0), axis=1, keepdims=True)
            bcol = jnp.sum(jnp.where(lane == h, beta_cols, 0.0), axis=1, keepdims=True)
            pick_r = sub == (C_HEADS + h)
            grow = jnp.sum(jnp.where(pick_r, gcum_rows, 0.0), axis=0, keepdims=True)
            glrow = jnp.sum(jnp.where(pick_r, gtot_rows, 0.0), axis=0, keepdims=True)
            q = conv_silu(c0)
            k = conv_silu(c0 + hd)
            v = conv_silu(c0 + 2 * hd)
            q = q * lax.rsqrt(jnp.sum(q * q, axis=-1, keepdims=True) + RMS_EPS) * (dk ** -0.5)
            k = k * lax.rsqrt(jnp.sum(k * k, axis=-1, keepdims=True) + RMS_EPS)
            eg = jnp.exp(gcol)
            kb = k * bcol
            hv.append(dict(
                c0=c0, gcol=gcol, grow=grow, glrow=glrow, kbf=k.astype(BF16), kbb=kb.astype(BF16),
                qbf=q.astype(BF16), rhs=jnp.concatenate([v * bcol, kb * eg], axis=1).astype(BF16),
                qe=q * eg, ktil_t=(k * jnp.exp(glcol - gcol)).T.astype(BF16)))

        inst = [(i, p) for i in range(C_GROUP) for p in range(ts // pr)]
        mpow, tinv, intra = {}, {}, {}
        for (i, p) in inst:
            d_ = hv[i]
            rp = slice(p * pr, (p + 1) * pr)
            decay = jnp.exp(jnp.where(le, d_["gcol"][rp] - d_["grow"][:, rp], NEG))
            lower = jnp.where(strict, _dot_nt(d_["kbb"][rp], d_["kbf"][rp]) * decay, 0.0)
            intra[i, p] = (_dot_nt(d_["qbf"][rp], d_["kbf"][rp]) * decay).astype(BF16)
            mpow[i, p] = -lower
            tinv[i, p] = eye - lower
        for _ in range(5):
            for key in inst:
                mb = mpow[key].astype(BF16)
                mpow[key] = _dot(mb, mb)
            for key in inst:
                tinv[key] = tinv[key] + _dot(tinv[key].astype(BF16), mpow[key].astype(BF16))
        wv_, kc = {}, {}
        for (i, p) in inst:
            wk = _dot(tinv[i, p].astype(BF16), hv[i]["rhs"][p * pr:(p + 1) * pr])
            wv_[i, p] = wk[:, :dk]
            kc[i, p] = wk[:, dk:]

        st = []
        for i, h in enumerate(heads):
            vnew_ref[i] = jnp.zeros(vnew_ref.shape[1:], vnew_ref.dtype)
            st.append(state_ref[h])
        outs = [[] for _ in heads]
        for j in range(nch):
            p, jj = j // 2, j % 2
            rows = slice(j * C_CHUNK, (j + 1) * C_CHUNK)
            lrows = slice(jj * C_CHUNK, (jj + 1) * C_CHUNK)
            rp = slice(p * pr, (p + 1) * pr)
            a1 = [_dot(jnp.concatenate([kc[i, p][lrows], hv[i]["qe"][rows]], axis=0).astype(BF16),
                       st[i].astype(BF16)) for i in range(C_GROUP)]
            for i in range(C_GROUP):
                vnew_ref[i, rows, :] = (wv_[i, p][lrows] - a1[i][:C_CHUNK]).astype(BF16)
            for i in range(C_GROUP):
                vn_pair = vnew_ref[i, rp, :]
                outs[i].append(a1[i][C_CHUNK:] + _dot(intra[i, p][lrows, :], vn_pair))
                kt_j = jnp.where(csel == jj, hv[i]["ktil_t"][:, rp], jnp.zeros((), BF16))
                dg = jnp.exp(jnp.sum(jnp.where(csel == jj, hv[i]["glrow"][:, rp], 0.0), axis=1, keepdims=True)
                             * (1.0 / C_CHUNK))
                st[i] = st[i] * dg + _dot(kt_j, vn_pair)
        for i, h in enumerate(heads):
            state_ref[h] = st[i]
            o = jnp.concatenate(outs[i], axis=0)
            o = o * lax.rsqrt(jnp.mean(o * o, axis=-1, keepdims=True) + RMS_EPS) * nw_ref[...]
            c0 = hv[i]["c0"]
            z = qkvz_ref[:, pl.ds(pl.multiple_of(c0 + 3 * hd, dk), dk)].astype(F32)
            o_ref[:, pl.ds(c0, dk)] = (o * (z * _sigmoid(z))).astype(o_ref.dtype)
        return carry

    lax.fori_loop(0, C_HEADS // C_GROUP, head_group, 0)
    ext_ref[0:8, :] = ext_ref[ts:ts + 8, :]


def _mixer_c(x2, w_in, conv_w, a_log, dt_bias, norm_w, w_out, g, b, bsz, seq):
    t, d = x2.shape
    hd = C_HEADS * C_HEAD_DIM
    qkvz = _proj(x2, w_in[:, :4 * hd].astype(BF16), BF16)
    wg = jnp.zeros((d, LANES), F32).at[:, :2 * C_HEADS].set(w_in[:, 4 * hd:])
    gates = _proj3(x2, wg)
    col = lambda v: jnp.zeros((LANES, 1), F32).at[C_HEADS:2 * C_HEADS, 0].set(v.astype(F32))
    nsteps = seq // C_SUPER
    o = pl.pallas_call(
        _delta_body,
        grid=(bsz, nsteps),
        in_specs=[pl.BlockSpec((C_SUPER, 4 * hd), lambda bi, i: (bi * nsteps + i, 0)),
                  pl.BlockSpec((C_SUPER, LANES), lambda bi, i: (bi * nsteps + i, 0)),
                  _const_spec((C_CONV, 3 * hd)), _const_spec((LANES, 1)), _const_spec((LANES, 1)),
                  _const_spec((1, C_HEAD_DIM))],
        out_specs=pl.BlockSpec((C_SUPER, hd), lambda bi, i: (bi * nsteps + i, 0)),
        out_shape=jax.ShapeDtypeStruct((t, hd), BF16),
        scratch_shapes=[pltpu.VMEM((C_SUPER + 8, 3 * hd), F32),
                        pltpu.VMEM((C_HEADS, C_HEAD_DIM, C_HEAD_DIM), F32),
                        pltpu.VMEM((C_GROUP, C_SUPER, C_HEAD_DIM), BF16)],
        compiler_params=_cparams(2),
        name="deltanet",
    )(qkvz, gates, conv_w.astype(F32), col(a_log), col(dt_bias), norm_w.reshape(1, -1).astype(F32))
    return _mm_res_ln(o, w_out.astype(BF16), x2, g, b)


def _route_rows(x, wh_ref, wl_ref, o_ref, g1_ref, g2_ref, cnt_ref, run_ref, tri_ref):
    tm = x.shape[0]

    @pl.when(pl.program_id(0) == 0)
    def _():
        run_ref[...] = jnp.zeros_like(run_ref)
        r_i = lax.broadcasted_iota(I32, (tm, tm), 0)
        c_i = lax.broadcasted_iota(I32, (tm, tm), 1)
        tri_ref[...] = jnp.where(c_i < r_i, 1.0, 0.0).astype(BF16)

    xh = x.astype(BF16)
    xl = (x - xh.astype(F32)).astype(BF16)
    wh = wh_ref[...]
    logits = _dot(xh, wh) + _dot(xl, wh) + _dot(xh, wl_ref[...])
    lane = lax.broadcasted_iota(I32, (1, LANES), 1)
    lane_f = lane.astype(F32)

    def top1(vals):
        m = jnp.max(vals, axis=-1, keepdims=True)
        idx = jnp.min(jnp.where(vals == m, lane_f, 1e9), axis=-1, keepdims=True)
        return m, idx.astype(I32)

    lc = jnp.where(lane < MOE_GROUPS, logits, NEG)
    mc, grp = top1(lc)
    p_grp = 1.0 / jnp.sum(jnp.exp(lc - mc), axis=-1, keepdims=True)
    lo = MOE_GROUPS + MOE_PER_GROUP * grp
    lf = jnp.where(jnp.logical_and(lane >= lo, lane < lo + MOE_PER_GROUP), logits, NEG)
    m1, i1 = top1(lf)
    m2, i2 = top1(jnp.where(lane == i1, NEG, lf))
    e21 = jnp.exp(m2 - m1)
    g1 = p_grp / (1.0 + e21)
    g2 = p_grp * e21 / (1.0 + e21)
    ex1 = i1 - MOE_GROUPS
    ex2 = i2 - MOE_GROUPS

    oh1 = jnp.where(lane == ex1, 1.0, 0.0).astype(F32)
    oh2 = jnp.where(lane == ex2, 1.0, 0.0).astype(F32)
    ohs = oh1 + oh2
    before = _dot(tri_ref[...], ohs.astype(BF16)) + run_ref[...]
    rank1 = jnp.sum(oh1 * before, axis=-1, keepdims=True)
    rank2 = jnp.sum(oh2 * before, axis=-1, keepdims=True)
    run_ref[...] = run_ref[...] + jnp.sum(ohs, axis=0, keepdims=True)

    out = jnp.where(lane == 0, ex1.astype(F32), 0.0)
    out = jnp.where(lane == 1, ex2.astype(F32), out)
    out = jnp.where(lane == 2, g1, out)
    out = jnp.where(lane == 3, g2, out)
    out = jnp.where(lane == 4, rank1, out)
    out = jnp.where(lane == 5, rank2, out)
    o_ref[...] = out
    g1_ref[...] = jnp.broadcast_to(g1, g1_ref.shape)
    g2_ref[...] = jnp.broadcast_to(g2, g2_ref.shape)
    cnt_ref[...] = jnp.broadcast_to(run_ref[...], cnt_ref.shape)


def _route_weights(w_coarse, w_fine):
    d = w_coarse.shape[0]
    wr = jnp.zeros((d, LANES), F32)
    wr = wr.at[:, :MOE_GROUPS].set(w_coarse).at[:, MOE_GROUPS:MOE_GROUPS + MOE_EXPERTS].set(
        w_fine.reshape(d, MOE_EXPERTS))
    wh = wr.astype(BF16)
    wl = (wr - wh.astype(F32)).astype(BF16)
    return wh, wl


def _route_body(x_ref, wh_ref, wl_ref, o_ref, g1_ref, g2_ref, cnt_ref, run_ref, tri_ref):
    _route_rows(x_ref[...], wh_ref, wl_ref, o_ref, g1_ref, g2_ref, cnt_ref, run_ref, tri_ref)


def _route(x2, w_coarse, w_fine, tm=512):
    t, d = x2.shape
    wh, wl = _route_weights(w_coarse, w_fine)
    row = pl.BlockSpec((tm, LANES), lambda i: (i, 0))
    return pl.pallas_call(
        _route_body,
        grid=(t // tm,),
        in_specs=[pl.BlockSpec((tm, d), lambda i: (i, 0)), _const_spec((d, LANES)), _const_spec((d, LANES))],
        out_specs=[row, row, row, pl.BlockSpec((8, LANES), lambda i: (0, 0))],
        out_shape=[jax.ShapeDtypeStruct((t, LANES), F32), jax.ShapeDtypeStruct((t, LANES), F32),
                   jax.ShapeDtypeStruct((t, LANES), F32), jax.ShapeDtypeStruct((8, LANES), F32)],
        scratch_shapes=[pltpu.VMEM((1, LANES), F32), pltpu.VMEM((tm, tm), BF16)],
        compiler_params=_cparams(1),
        name="moe_route",
    )(x2, wh, wl)


def _dispatch_body(p1_ref, p2_ref, x_ref, o_hbm, slab_ref, sem, *, tm):
    s = pl.program_id(0)
    i = pl.program_id(1)

    @pl.when(jnp.logical_and(s == 0, i == 0))
    def _():
        slab_ref[...] = jnp.zeros_like(slab_ref)

    def row(r, carry):
        v = x_ref[pl.ds(r, 1), :]
        slab_ref[pl.ds(p1_ref[0, r], 1), :] = v
        slab_ref[pl.ds(p2_ref[0, r], 1), :] = v
        return carry
    lax.fori_loop(0, tm, row, 0, unroll=ROW_UNROLL)

    @pl.when(i == pl.num_programs(1) - 1)
    def _():
        cp = pltpu.make_async_copy(slab_ref, o_hbm.at[s], sem)
        cp.start()
        cp.wait()


def _dispatch(xp, pos1, pos2, s_pad, tm=1024):
    t, w = xp.shape
    nslab = w // LANES
    nt = t // tm
    return pl.pallas_call(
        functools.partial(_dispatch_body, tm=tm),
        grid=(nslab, nt),
        in_specs=[pl.BlockSpec((None, 1, tm), lambda s, i: (i, 0, 0), memory_space=pltpu.SMEM),
                  pl.BlockSpec((None, 1, tm), lambda s, i: (i, 0, 0), memory_space=pltpu.SMEM),
                  pl.BlockSpec((tm, LANES), lambda s, i: (i, s))],
        out_specs=pl.BlockSpec(memory_space=pl.ANY),
        out_shape=jax.ShapeDtypeStruct((nslab, s_pad, LANES), U32),
        scratch_shapes=[pltpu.VMEM((s_pad, LANES), U32), pltpu.SemaphoreType.DMA(())],
        compiler_params=_cparams(2),
        name="moe_dispatch",
    )(pos1.reshape(nt, 1, tm), pos2.reshape(nt, 1, tm), xp)


def _expert_body(te_ref, nv_ref, xs_ref, wg_ref, wu_ref, wd_ref, y_ref, wgb_ref, wub_ref, wdb_ref):
    i = pl.program_id(0)
    half = D_MODEL // 2
    nslab = half // LANES

    @pl.when(i < nv_ref[0])
    def _():
        changed = jnp.logical_or(i == 0, te_ref[i] != te_ref[jnp.maximum(i - 1, 0)])

        @pl.when(changed)
        def _():
            wgb_ref[...] = wg_ref[...].astype(BF16)
            wub_ref[...] = wu_ref[...].astype(BF16)
            wdb_ref[...] = wd_ref[...].astype(BF16)

        xlo = jnp.concatenate([_unpack_lo(xs_ref[s]) for s in range(nslab)], axis=1).astype(BF16)
        xhi = jnp.concatenate([_unpack_hi(xs_ref[s]) for s in range(nslab)], axis=1).astype(BF16)
        hg =_dot(xlo, wgb_ref[0:half, :]) + _dot(xhi, wgb_ref[half:, :])
        hu = _dot(xlo, wub_ref[0:half, :]) + _dot(xhi, wub_ref[half:, :])
        h = (hg * _sigmoid(hg) * hu).astype(BF16)
        packed = _pack_pairs(_dot(h, wdb_ref[...]))
        for s in range(nslab):
            y_ref[s] = packed[:, s * LANES:(s + 1) * LANES]

    @pl.when(i >= nv_ref[0])
    def _():
        y_ref[...] = jnp.zeros_like(y_ref)


def _experts(xs, tile_expert, nvalid, w_gate, w_up, w_down, layer, n_tiles):
    tm = MOE_TM
    d, hdn = D_MODEL, MOE_HIDDEN
    nslab = d // 2 // LANES
    return pl.pallas_call(
        _expert_body,
        grid_spec=pltpu.PrefetchScalarGridSpec(
            num_scalar_prefetch=2,
            grid=(n_tiles,),
            in_specs=[pl.BlockSpec((nslab, tm, LANES), lambda i, te, nv: (0, jnp.minimum(i, nv[0] - 1), 0)),
                      pl.BlockSpec((None, None, d, hdn), lambda i, te, nv: (layer, te[i], 0, 0)),
                      pl.BlockSpec((None, None, d, hdn), lambda i, te, nv: (layer, te[i], 0, 0)),
                      pl.BlockSpec((None, None, hdn, d), lambda i, te, nv: (layer, te[i], 0, 0))],
            out_specs=pl.BlockSpec((nslab, tm, LANES), lambda i, te, nv: (0, i, 0)),
            scratch_shapes=[pltpu.VMEM((d, hdn), BF16), pltpu.VMEM((d, hdn), BF16), pltpu.VMEM((hdn, d), BF16)],
        ),
        out_shape=jax.ShapeDtypeStruct((nslab, n_tiles * tm, LANES), U32),
        compiler_params=_cparams(1),
        name="moe_experts",
    )(tile_expert, nvalid, xs, w_gate, w_up, w_down)


def _combine_body(p1_ref, p2_ref, y_ref, g1_ref, g2_ref, lo_ref, hi_ref, b1_ref, b2_ref, *, tm):
    def row(r, carry):
        b1_ref[pl.ds(r, 1), :] = y_ref[pl.ds(p1_ref[0, r], 1), :]
        b2_ref[pl.ds(r, 1), :] = y_ref[pl.ds(p2_ref[0, r], 1), :]
        return carry
    lax.fori_loop(0, tm, row, 0, unroll=ROW_UNROLL)
    g1 = g1_ref[...]
    g2 = g2_ref[...]
    y1 = b1_ref[...]
    y2 = b2_ref[...]
    lo_ref[...] = g1 * _unpack_lo(y1) + g2 * _unpack_lo(y2)
    hi_ref[...] = g1 * _unpack_hi(y1) + g2 * _unpack_hi(y2)


def _combine(y, pos1, pos2, g1b, g2b, tm=1024):
    nslab, s_pad, _ = y.shape
    t = pos1.shape[0]
    nt = t // tm
    return pl.pallas_call(
        functools.partial(_combine_body, tm=tm),
        grid=(nslab, nt),
        in_specs=[pl.BlockSpec((None, 1, tm), lambda s, i: (i, 0, 0), memory_space=pltpu.SMEM),
                  pl.BlockSpec((None, 1, tm), lambda s, i: (i, 0, 0), memory_space=pltpu.SMEM),
                  pl.BlockSpec((None, s_pad, LANES), lambda s, i: (s, 0, 0), pipeline_mode=pl.Buffered(1)),
                  pl.BlockSpec((tm, LANES), lambda s, i: (i, 0)), pl.BlockSpec((tm, LANES), lambda s, i: (i, 0))],
        out_specs=[pl.BlockSpec((tm, LANES), lambda s, i: (i, s)), pl.BlockSpec((tm, LANES), lambda s, i: (i, s))],
        out_shape=[jax.ShapeDtypeStruct((t, nslab * LANES), F32), jax.ShapeDtypeStruct((t, nslab * LANES), F32)],
        scratch_shapes=[pltpu.VMEM((tm, LANES), U32), pltpu.VMEM((tm, LANES), U32)],
        compiler_params=_cparams(2),
        name="moe_combine",
    )(pos1.reshape(nt, 1, tm), pos2.reshape(nt, 1, tm), y, g1b, g2b)


def _res_ln_body(x_ref, lo_ref, hi_ref, g_ref, b_ref, o_ref):
    h = jnp.concatenate([lo_ref[...], hi_ref[...]], axis=1)
    o_ref[...] = _ln_rows(ALPHA * x_ref[...] + h, g_ref[...], b_ref[...])


def _res_ln(x2, lo, hi, g, b, tm=1024):
    t, d = x2.shape
    return pl.pallas_call(
        _res_ln_body,
        grid=(t // tm,),
        in_specs=[pl.BlockSpec((tm, d), lambda i: (i, 0)), pl.BlockSpec((tm, d // 2), lambda i: (i, 0)),
                  pl.BlockSpec((tm, d // 2), lambda i: (i, 0)), _const_spec((1, d)), _const_spec((1, d))],
        out_specs=pl.BlockSpec((tm, d), lambda i: (i, 0)),
        out_shape=jax.ShapeDtypeStruct((t, d), F32),
        compiler_params=_cparams(1),
        name="res_ln",
    )(x2, lo, hi, g.reshape(1, d), b.reshape(1, d))


def _moe(x2, xp, w_coarse, w_fine, w_gate, w_up, w_down, layer, g, b):
    t = x2.shape[0]
    tm = MOE_TM
    n_tiles = (2 * t) // tm + MOE_EXPERTS
    route, g1b, g2b, cnt = _route(x2, w_coarse, w_fine)
    ex = route[:, 0:2].astype(I32)
    rank = route[:, 4:6].astype(I32)
    counts = cnt[0, :MOE_EXPERTS].astype(I32)
    ptiles = (counts + tm - 1) // tm
    tile_end = jnp.cumsum(ptiles)
    pstart = (tile_end - ptiles) * tm
    eids = jnp.arange(MOE_EXPERTS, dtype=I32)
    pos = jnp.sum(jnp.where(ex[:, :, None] == eids, pstart, 0), axis=-1) + rank
    nvalid = tile_end[-1:].astype(I32)
    tidx = jnp.arange(n_tiles, dtype=I32)
    te = jnp.sum((tile_end[None, :] <= jnp.minimum(tidx, nvalid[0] - 1)[:, None]).astype(I32), axis=1)
    xs = _dispatch(xp, pos[:, 0], pos[:, 1], n_tiles * tm)
    y = _experts(xs, te, nvalid, w_gate, w_up, w_down, layer, n_tiles)
    lo, hi = _combine(y, pos[:, 0], pos[:, 1], g1b, g2b)
    return _res_ln(x2, lo, hi, g, b)


def kernel(x, rel_bias, a_w_in, a_w_out, b_w_in, b_norm_g, b_norm_b, b_w_s, b_b_s, b_w_out, c_w_in, c_conv,
           c_a_log, c_dt_bias, c_norm_w, c_w_out, ln_g, ln_b, moe_w_coarse, moe_w_fine, moe_w_gate, moe_w_up,
           moe_w_down):
    bsz, seq, d = x.shape
    x2 = x.reshape(bsz * seq, d)
    for i in range(DEPTH):
        kind, j = i % 3, i // 3
        g1, b1 = ln_g[i, 0], ln_b[i, 0]
        if kind == 0:
            x2, xp = _mixer_a(x2, a_w_in[j], a_w_out[j], rel_bias, g1, b1, bsz, seq)
        elif kind == 1:
            x2, xp = _mixer_b(x2, b_w_in[j], b_norm_g[j], b_norm_b[j], b_w_s[j], b_b_s[j], b_w_out[j], g1, b1)
        else:
            x2, xp = _mixer_c(x2, c_w_in[j], c_conv[j], c_a_log[j], c_dt_bias[j], c_norm_w[j], c_w_out[j],
                              g1, b1, bsz, seq)
        x2 = _moe(x2, xp, moe_w_coarse[i], moe_w_fine[i], moe_w_gate, moe_w_up, moe_w_down, i,
                  ln_g[i, 1], ln_b[i, 1])
    return x2.reshape(bsz, seq, d)
```

```python
import functools
import math

import numpy as np
import jax
import jax.numpy as jnp
from jax import lax
from jax.experimental import pallas as pl
from jax.experimental.pallas import tpu as pltpu

F32 = jnp.float32
BF16 = jnp.bfloat16
U32 = jnp.uint32
I32 = jnp.int32

D_MODEL = 1024
DEPTH = 4
A_GROUPS = ((128, 1), (512, 4), (2048, 16))
A_HEADS = 16
A_HEAD_DIM = 64
A_BLOCK = 128
NUM_BUCKETS = 32
MAX_DISTANCE = 2048
B_CHUNK = 128
B_WIDTH = 2 * D_MODEL
B_GROUPS = 16
C_HEADS = 8
C_HEAD_DIM = 128
C_CONV = 4
C_CHUNK = 64
MOE_GROUPS = 8
MOE_PER_GROUP = 8
MOE_EXPERTS = 64
MOE_HIDDEN = 512
LN_EPS = 1e-5
RMS_EPS = 1e-6
ALPHA = (2 * DEPTH) ** 0.25

LANES = 128
NEG = -1e30
LOG2E = 1.4426950408889634
LN2 = 0.6931471805599453
VMEM_LIMIT = 56 * 1024 * 1024
MOE_TM = 512
ROW_UNROLL = 16
C_SUPER = 256
C_GROUP = 4


def _cparams(n_axes, vmem=VMEM_LIMIT):
    return pltpu.CompilerParams(dimension_semantics=("arbitrary",) * n_axes, vmem_limit_bytes=vmem)


def _const_spec(shape):
    nd = len(shape)
    return pl.BlockSpec(shape, lambda *_: (0,) * nd, pipeline_mode=pl.Buffered(1))


def _ln_rows(y, g, b):
    mu = jnp.mean(y, axis=-1, keepdims=True)
    yc = y - mu
    var = jnp.mean(yc * yc, axis=-1, keepdims=True)
    return yc * lax.rsqrt(var + LN_EPS) * g + b


def _pack_pairs(y):
    w = y.shape[1] // 2
    lo = lax.bitcast_convert_type(y[:, :w].astype(BF16).astype(F32), U32)
    hi = lax.bitcast_convert_type(y[:, w:].astype(BF16).astype(F32), U32)
    return (lo >> 16) | (hi & jnp.uint32(0xFFFF0000))


def _unpack_lo(p):
    return lax.bitcast_convert_type(p << 16, F32)


def _unpack_hi(p):
    return lax.bitcast_convert_type(p & jnp.uint32(0xFFFF0000), F32)


def _split3(a):
    h = a.astype(BF16)
    r = a - h.astype(F32)
    m = r.astype(BF16)
    l = (r - m.astype(F32)).astype(BF16)
    return h, m, l


def _dot(a, b):
    return jnp.dot(a, b, preferred_element_type=F32)


def _dot_nt(a, b):
    return lax.dot_general(a, b, (((1,), (1,)), ((), ())), preferred_element_type=F32)


def _gelu_tanh(x):
    return 0.5 * x * (1.0 + jnp.tanh(0.7978845608028654 * (x + 0.044715 * (x * x * x))))


def _sigmoid(x):
    return 0.5 * jnp.tanh(0.5 * x) + 0.5


def _proj_body(x_ref, w_ref, o_ref):
    xb = x_ref[...].astype(BF16)
    cw = 1024
    for c in range(w_ref.shape[1] // cw):
        o_ref[:, c * cw:(c + 1) * cw] = _dot(xb, w_ref[:, c * cw:(c + 1) * cw]).astype(o_ref.dtype)


def _proj(x2, w, out_dtype, tm=1024):
    t, k = x2.shape
    n = w.shape[1]
    return pl.pallas_call(
        _proj_body,
        grid=(t // tm,),
        in_specs=[pl.BlockSpec((tm, k), lambda i: (i, 0)), _const_spec((k, n))],
        out_specs=pl.BlockSpec((tm, n), lambda i: (i, 0)),
        out_shape=jax.ShapeDtypeStruct((t, n), out_dtype),
        compiler_params=_cparams(1),
        name="proj",
    )(x2, w)


def _proj3_body(x_ref, wh_ref, wl_ref, o_ref):
    x = x_ref[...]
    xh = x.astype(BF16)
    xl = (x - xh.astype(F32)).astype(BF16)
    wh = wh_ref[...]
    o_ref[...] = _dot(xh, wh) + _dot(xl, wh) + _dot(xh, wl_ref[...])


def _proj3(x2, w, tm=1024):
    t, k = x2.shape
    n = w.shape[1]
    wh = w.astype(BF16)
    wl = (w - wh.astype(F32)).astype(BF16)
    return pl.pallas_call(
        _proj3_body,
        grid=(t // tm,),
        in_specs=[pl.BlockSpec((tm, k), lambda i: (i, 0)), _const_spec((k, n)), _const_spec((k, n))],
        out_specs=pl.BlockSpec((tm, n), lambda i: (i, 0)),
        out_shape=jax.ShapeDtypeStruct((t, n), F32),
        compiler_params=_cparams(1),
        name="proj3",
    )(x2, wh, wl)


def _mm_res_ln_body(a_ref, w_ref, x_ref, g_ref, b_ref, o_ref, p_ref):
    y = _dot(a_ref[...], w_ref[...])
    xn = _ln_rows(ALPHA * x_ref[...] + y, g_ref[...], b_ref[...])
    o_ref[...] = xn
    p_ref[...] = _pack_pairs(xn)


def _mm_res_ln(a, w, x2, g, b, tm=512):
    t, k = a.shape
    d = w.shape[1]
    return pl.pallas_call(
        _mm_res_ln_body,
        grid=(t // tm,),
        in_specs=[pl.BlockSpec((tm, k), lambda i: (i, 0)), _const_spec((k, d)),
                  pl.BlockSpec((tm, d), lambda i: (i, 0)), _const_spec((1, d)), _const_spec((1, d))],
        out_specs=[pl.BlockSpec((tm, d), lambda i: (i, 0)), pl.BlockSpec((tm, d // 2), lambda i: (i, 0))],
        out_shape=[jax.ShapeDtypeStruct((t, d), F32), jax.ShapeDtypeStruct((t, d // 2), U32)],
        compiler_params=_cparams(1),
        name="mm_res_ln",
    )(a, w, x2, g.reshape(1, d), b.reshape(1, d))


def _t5_bucket(dist):
    max_exact = NUM_BUCKETS // 2
    d = jnp.maximum(dist, 1).astype(F32)
    large = max_exact + (jnp.log(d / max_exact) / math.log(MAX_DISTANCE / max_exact)
                         * (NUM_BUCKETS - max_exact)).astype(I32)
    return jnp.where(dist < max_exact, dist, jnp.minimum(large, NUM_BUCKETS - 1))


def _attn_bias(rel_bias, window, dil):
    steps = window // dil
    qi = jnp.arange(A_BLOCK)[:, None]
    ki = jnp.arange(2 * A_BLOCK)[None, :]
    rel = qi + A_BLOCK - ki
    valid = (rel >= 0) & (rel <= steps)
    bucket = _t5_bucket(jnp.maximum(rel, 0) * dil)
    bias = jnp.zeros((A_HEADS, A_BLOCK, 2 * A_BLOCK), F32)
    for bkt in range(NUM_BUCKETS):
        bias = jnp.where((bucket == bkt)[None], rel_bias[bkt].astype(F32)[:, None, None], bias)
    return jnp.where(valid[None], bias * LOG2E, NEG)


def _attn_body(q_ref, k_ref, v_ref, kp_ref, vp_ref, bias_ref, o_ref, lse_ref, *, nres, nblk):
    li = pl.program_id(2)
    lane = lax.broadcasted_iota(I32, (1, LANES), 1)
    col2 = lax.broadcasted_iota(I32, (1, 2 * A_BLOCK), 1)
    first_pen = jnp.where(col2 < A_BLOCK, jnp.where(li == 0, NEG, 0.0).astype(F32), 0.0)
    head0 = lane < A_HEAD_DIM
    lse_ref[...] = jnp.zeros_like(lse_ref)

    def hp_body(hp, carry):
        c0 = pl.multiple_of(hp * LANES, LANES)
        cols = pl.ds(c0, LANES)
        blocks = [(r, j) for r in range(nres) for j in range(nblk)]
        bias2 = jnp.concatenate([bias_ref[2 * hp], bias_ref[2 * hp + 1]], axis=0)
        vbs, s_ = {}, {}
        for (r, j) in blocks:
            rows = pl.ds(j * A_BLOCK, A_BLOCK)
            q = q_ref[r, rows, cols]
            if j == 0:
                kb = jnp.concatenate([kp_ref[r, :, cols], k_ref[r, rows, cols]], axis=0)
                vbs[r, j] = jnp.concatenate([vp_ref[r, :, cols], v_ref[r, rows, cols]], axis=0)
            else:
                band = pl.ds((j - 1) * A_BLOCK, 2 * A_BLOCK)
                kb = k_ref[r, band, cols]
                vbs[r, j] = v_ref[r, band, cols]
            zero = jnp.zeros_like(q)
            q2 = jnp.concatenate([jnp.where(head0, q, zero), jnp.where(head0, zero, q)], axis=0)
            s = _dot_nt(q2, kb) + bias2
            s_[r, j] = s + first_pen if j == 0 else s
        p_, l_, lse_ = {}, {}, {}
        for u in blocks:
            m = jnp.max(s_[u], axis=-1, keepdims=True)
            p = jnp.exp2(s_[u] - m)
            l_[u] = jnp.sum(p, axis=-1, keepdims=True)
            p_[u] = p.astype(BF16)
            lse_[u] = m * LN2 + jnp.log(l_[u])
        pv = {u: _dot(p_[u], vbs[u]) / l_[u] for u in blocks}
        for (r, j) in blocks:
            rows = pl.ds(j * A_BLOCK, A_BLOCK)
            o_ref[r, rows, cols] = jnp.where(head0, pv[r, j][:A_BLOCK], pv[r, j][A_BLOCK:]).astype(o_ref.dtype)
            cur = lse_ref[r, rows, :]
            cur = jnp.where(lane == 2 * hp, lse_[r, j][:A_BLOCK], cur)
            cur = jnp.where(lane == 2 * hp + 1, lse_[r, j][A_BLOCK:], cur)
            lse_ref[r, rows, :] = cur
        return carry

    lax.fori_loop(0, A_HEADS // 2, hp_body, 0)


def _proj_perm_body(x_ref, w_ref, o_ref, xb_ref, y_ref, *, dil):
    tm, k = x_ref.shape
    n = tm // dil

    @pl.when(pl.program_id(1) == 0)
    def _():
        if dil == 1:
            xb_ref[...] = x_ref[...].astype(BF16)
        else:
            nc = k // LANES
            for c in range(nc):
                y_ref[c] = x_ref[:, c * LANES:(c + 1) * LANES]
            for r in range(dil):
                xb_ref[r * n:(r + 1) * n, :] = jnp.concatenate(
                    [y_ref[c, pl.ds(r, n, stride=dil), :] for c in range(nc)], axis=1).astype(BF16)

    cw = 1024
    for c in range(w_ref.shape[1] // cw):
        y = _dot(xb_ref[...], w_ref[:, c * cw:(c + 1) * cw])
        for r in range(dil):
            o_ref[r, :, c * cw:(c + 1) * cw] = y[r * n:(r + 1) * n].astype(o_ref.dtype)


def _proj_perm(x2, w, g, dil, bsz, seq, tm=1024, tn=3072):
    t, k = x2.shape
    hd = A_HEADS * A_HEAD_DIM
    tpb = seq // tm
    return pl.pallas_call(
        functools.partial(_proj_perm_body, dil=dil),
        grid=(t // tm, 3 * hd // tn),
        in_specs=[pl.BlockSpec((tm, k), lambda i, j: (i, 0)),
                  pl.BlockSpec((k, tn), lambda i, j: (0, g * (3 * hd // tn) + j))],
        out_specs=pl.BlockSpec((None, dil, tm // dil, tn), lambda i, j: (i // tpb, 0, i % tpb, j)),
        out_shape=jax.ShapeDtypeStruct((bsz, dil, seq // dil, 3 * hd), BF16),
        scratch_shapes=[pltpu.VMEM((tm, k), BF16), pltpu.VMEM((k // LANES, tm, LANES), F32)],
        compiler_params=_cparams(2),
        name=f"proj_d{dil}",
    )(x2, w)


def _dilated_group(qkv, dil, bias, bsz, seq):
    hd = A_HEADS * A_HEAD_DIM
    L = seq // dil
    rows_per_step = 512
    tl = min(rows_per_step, L)
    nblk = tl // A_BLOCK
    nres = rows_per_step // tl

    def blk(col):
        return pl.BlockSpec((None, nres, tl, hd), lambda b, r, li: (b, r, li, col))

    def prev(col):
        return pl.BlockSpec((None, nres, A_BLOCK, hd),
                            lambda b, r, li: (b, r, jnp.maximum(li * nblk - 1, 0), col))

    return pl.pallas_call(
        functools.partial(_attn_body, nres=nres, nblk=nblk),
        grid=(bsz, dil // nres, L // tl),
        in_specs=[blk(0), blk(1), blk(2), prev(1), prev(2), _const_spec((A_HEADS, A_BLOCK, 2 * A_BLOCK))],
        out_specs=[pl.BlockSpec((None, nres, tl, hd), lambda b, r, li: (b, r, li, 0)),
                   pl.BlockSpec((None, nres, tl, LANES), lambda b, r, li: (b, r, li, 0))],
        out_shape=[jax.ShapeDtypeStruct((bsz, dil, L, hd), BF16),
                   jax.ShapeDtypeStruct((bsz, dil, L, LANES), F32)],
        compiler_params=_cparams(3),
        name=f"dilated_attn_d{dil}",
    )(qkv, qkv, qkv, qkv, qkv, bias)


def _attn_out_body(o1_ref, o2_ref, o3_ref, l1_ref, l2_ref, l3_ref, e_ref, w_ref, x_ref, g_ref, b_ref,
                   xo_ref, p_ref, so_ref, sl2_ref, sl3_ref, *, dils):
    def natural(ref, scr, dil):
        if dil == 1:
            return ref[0].astype(F32)
        n = ref.shape[1]
        nc = ref.shape[2] // LANES
        for r in range(dil):
            blk = ref[r].astype(F32)
            for c in range(nc):
                scr[c, pl.ds(r, n, stride=dil), :] = blk[:, c * LANES:(c + 1) * LANES]
        return jnp.concatenate([scr[c] for c in range(nc)], axis=1) if nc > 1 else scr[0]

    l1 = natural(l1_ref, None, dils[0])
    l2 = natural(l2_ref, sl2_ref, dils[1])
    l3 = natural(l3_ref, sl3_ref, dils[2])
    m = jnp.maximum(jnp.maximum(l1, l2), l3)
    e1, e2, e3 = jnp.exp(l1 - m), jnp.exp(l2 - m), jnp.exp(l3 - m)
    inv = 1.0 / (e1 + e2 + e3)
    ex = e_ref[...]

    def expand(wt):
        h = wt.astype(BF16)
        lo = (wt - h.astype(F32)).astype(BF16)
        return _dot(h, ex) + _dot(lo, ex)

    comb = expand(e1 * inv) * natural(o1_ref, None, dils[0])
    comb = comb + expand(e2 * inv) * natural(o2_ref, so_ref, dils[1])
    comb = comb + expand(e3 * inv) * natural(o3_ref, so_ref, dils[2])
    y = _dot(comb.astype(BF16), w_ref[...])
    xn = _ln_rows(ALPHA * x_ref[...] + y, g_ref[...], b_ref[...])
    xo_ref[...] = xn
    p_ref[...] = _pack_pairs(xn)


def _attn_out(os, lses, dils, w_out, x2, g, b, seq, tm=512):
    t, d = x2.shape
    tpb = seq // tm
    expand = np.zeros((LANES, d), np.float32)
    for h in range(A_HEADS):
        expand[h, h * A_HEAD_DIM:(h + 1) * A_HEAD_DIM] = 1.0
    row = lambda n: pl.BlockSpec((tm, n), lambda i: (i, 0))
    res = lambda dil, n: pl.BlockSpec((None, dil, tm // dil, n), lambda i: (i // tpb, 0, i % tpb, 0))
    return pl.pallas_call(
        functools.partial(_attn_out_body, dils=dils),
        grid=(t // tm,),
        in_specs=[res(dils[0], d), res(dils[1], d), res(dils[2], d),
                  res(dils[0], LANES), res(dils[1], LANES), res(dils[2], LANES),
                  _const_spec((LANES, d)), _const_spec((d, d)), row(d), _const_spec((1, d)), _const_spec((1, d))],
        out_specs=[row(d), row(d // 2)],
        out_shape=[jax.ShapeDtypeStruct((t, d), F32), jax.ShapeDtypeStruct((t, d // 2), U32)],
        scratch_shapes=[pltpu.VMEM((d // LANES, tm, LANES), F32), pltpu.VMEM((1, tm, LANES), F32),
                        pltpu.VMEM((1, tm, LANES), F32)],
        compiler_params=_cparams(1),
        name="attn_out",
    )(*os, *lses, jnp.asarray(expand, BF16), w_out, x2, g.reshape(1, d), b.reshape(1, d))


def _mixer_a(x2, w_in, w_out, rel_bias, g, b, bsz, seq):
    hd = A_HEADS * A_HEAD_DIM
    scale = np.ones((9 * hd,), np.float32)
    for gi in range(len(A_GROUPS)):
        scale[3 * gi * hd:(3 * gi + 1) * hd] = A_HEAD_DIM ** -0.5 * LOG2E
    wb = (w_in * scale).astype(BF16)
    os, lses = [], []
    for gi, (window, dil) in enumerate(A_GROUPS):
        qkv = _proj_perm(x2, wb, gi, dil, bsz, seq)
        o, lse = _dilated_group(qkv, dil, _attn_bias(rel_bias, window, dil), bsz, seq)
        os.append(o)
        lses.append(lse)
    return _attn_out(os, lses, tuple(dl for _, dl in A_GROUPS), w_out.astype(BF16), x2, g, b, seq)


def _sgu_body(x_ref, wu_ref, wv_ref, ng_ref, nb_ref, wc_ref, bs_ref, a_ref, vb_ref, *, tm):
    xb = x_ref[...].astype(BF16)
    v = _gelu_tanh(_dot(xb, wv_ref[...]))
    vb_ref[...] = _ln_rows(v, ng_ref[...], nb_ref[...]).astype(BF16)
    gw = B_WIDTH // B_GROUPS
    ucols = 512
    for j in range(B_WIDTH // ucols):
        u = _gelu_tanh(_dot(xb, wu_ref[:, j * ucols:(j + 1) * ucols]))
        nchunk = tm // B_CHUNK
        for gg in range(ucols // gw):
            gi = j * (ucols // gw) + gg
            cols = slice(gi * gw, (gi + 1) * gw)
            vcat = jnp.concatenate([vb_ref[c * B_CHUNK:(c + 1) * B_CHUNK, cols] for c in range(nchunk)], axis=1)
            fcat = _dot(wc_ref[gi], vcat)
            for c in range(nchunk):
                rows = slice(c * B_CHUNK, (c + 1) * B_CHUNK)
                f = fcat[:, c * gw:(c + 1) * gw] + bs_ref[:, cols]
                a_ref[rows, cols] = (u[rows, gg * gw:(gg + 1) * gw] * f).astype(a_ref.dtype)


def _mixer_b(x2, w_in, norm_g, norm_b, w_s, b_s, w_out, g, b, tm=512):
    t, d = x2.shape
    wu = w_in[:, :B_WIDTH].astype(BF16)
    wv = w_in[:, B_WIDTH:].astype(BF16)
    wc = (w_s * jnp.tril(jnp.ones((B_CHUNK, B_CHUNK), w_s.dtype))).astype(BF16)
    bs_full = jnp.repeat(b_s.T, B_WIDTH // B_GROUPS, axis=1)
    a = pl.pallas_call(
        functools.partial(_sgu_body, tm=tm),
        grid=(t // tm,),
        in_specs=[pl.BlockSpec((tm, d), lambda i: (i, 0)), _const_spec((d, B_WIDTH)), _const_spec((d, B_WIDTH)),
                  _const_spec((1, B_WIDTH)), _const_spec((1, B_WIDTH)),
                  _const_spec((B_GROUPS, B_CHUNK, B_CHUNK)), _const_spec((B_CHUNK, B_WIDTH))],
        out_specs=pl.BlockSpec((tm, B_WIDTH), lambda i: (i, 0)),
        out_shape=jax.ShapeDtypeStruct((t, B_WIDTH), BF16),
        scratch_shapes=[pltpu.VMEM((tm, B_WIDTH), BF16)],
        compiler_params=_cparams(1),
        name="sgu",
    )(x2, wu, wv, norm_g.reshape(1, -1), norm_b.reshape(1, -1), wc, bs_full)
    return _mm_res_ln(a, w_out.astype(BF16), x2, g, b)


def _delta_body(qkvz_ref, gates_ref, cw_ref, alog_ref, dtb_ref, nw_ref, o_ref, ext_ref, state_ref, vnew_ref):
    ts = C_SUPER
    hd = C_HEADS * C_HEAD_DIM
    dk = C_HEAD_DIM
    nch = ts // C_CHUNK
    pr = 2 * C_CHUNK

    @pl.when(pl.program_id(1) == 0)
    def _():
        ext_ref[0:8, :] = jnp.zeros((8, 3 * hd), F32)
        state_ref[...] = jnp.zeros_like(state_ref)

    ext_ref[8:8 + ts, :] = qkvz_ref[:, 0:3 * hd].astype(F32)

    r_i = lax.broadcasted_iota(I32, (ts, ts), 0)
    c_i = lax.broadcasted_iota(I32, (ts, ts), 1)
    same = (r_i >> 6) == (c_i >> 6)
    u_cum = jnp.where(jnp.logical_and(same, r_i <= c_i), 1.0, 0.0).astype(BF16)
    u_tot = jnp.where(same, 1.0, 0.0).astype(BF16)
    r_p = lax.broadcasted_iota(I32, (pr, pr), 0)
    c_p = lax.broadcasted_iota(I32, (pr, pr), 1)
    same_p = (r_p >> 6) == (c_p >> 6)
    le = jnp.logical_and(same_p, c_p <= r_p)
    strict = jnp.logical_and(same_p, c_p < r_p)
    eye = jnp.where(r_p == c_p, 1.0, 0.0).astype(F32)

    gt = gates_ref[...].T
    zt = gt + dtb_ref[...]
    g_rows = -jnp.exp(alog_ref[...]) * (jnp.maximum(zt, 0.0) + jnp.log(1.0 + jnp.exp(-jnp.abs(zt))))
    gh, gm, gl = _split3(g_rows)
    gcum_rows = _dot(gh, u_cum) + _dot(gm, u_cum) + _dot(gl, u_cum)
    gtot_rows = _dot(gh, u_tot) + _dot(gm, u_tot) + _dot(gl, u_tot)
    gcum_cols = gcum_rows.T
    gtot_cols = gtot_rows.T
    beta_cols = _sigmoid(gates_ref[...])
    lane = lax.broadcasted_iota(I32, (1, LANES), 1)
    sub = lax.broadcasted_iota(I32, (LANES, 1), 0)
    csel = lax.broadcasted_iota(I32, (1, pr), 1) >> 6

    def conv_silu(c0):
        cols = pl.ds(pl.multiple_of(c0, dk), dk)
        y = (cw_ref[0:1, cols] * ext_ref[pl.ds(5, ts), cols] + cw_ref[1:2, cols] * ext_ref[pl.ds(6, ts), cols]
             + cw_ref[2:3, cols] * ext_ref[pl.ds(7, ts), cols] + cw_ref[3:4, cols] * ext_ref[pl.ds(8, ts), cols])
        return y * _sigmoid(y)

    def head_group(hg, carry):
        heads = [hg * C_GROUP + i for i in range(C_GROUP)]
        hv = []
        for h in heads:
            c0 = pl.multiple_of(h * dk, dk)
            pick_a = lane == (C_HEADS + h)
            gcol = jnp.sum(jnp.where(pick_a, gcum_cols, 0.0), axis=1, keepdims=True)
            glcol = jnp.sum(jnp.where(pick_a, gtot_cols, 0.0), axis=1, keepdims=True)
            bcol = jnp.sum(jnp.where(lane == h, beta_cols, 0.0), axis=1, keepdims=True)
            pick_r = sub == (C_HEADS + h)
            grow = jnp.sum(jnp.where(pick_r, gcum_rows, 0.0), axis=0, keepdims=True)
            glrow = jnp.sum(jnp.where(pick_r, gtot_rows, 0.0), axis=0, keepdims=True)
            q = conv_silu(c0)
            k = conv_silu(c0 + hd)
            v = conv_silu(c0 + 2 * hd)
            q = q * lax.rsqrt(jnp.sum(q * q, axis=-1, keepdims=True) + RMS_EPS) * (dk ** -0.5)
            k = k * lax.rsqrt(jnp.sum(k * k, axis=-1, keepdims=True) + RMS_EPS)
            eg = jnp.exp(gcol)
            kb = k * bcol
            hv.append(dict(
                c0=c0, gcol=gcol, grow=grow, glrow=glrow, kbf=k.astype(BF16), kbb=kb.astype(BF16),
                qbf=q.astype(BF16), rhs=jnp.concatenate([v * bcol, kb * eg], axis=1).astype(BF16),
                qe=q * eg, ktil_t=(k * jnp.exp(glcol - gcol)).T.astype(BF16)))

        inst = [(i, p) for i in range(C_GROUP) for p in range(ts // pr)]
        mpow, tinv, intra = {}, {}, {}
        for (i, p) in inst:
            d_ = hv[i]
            rp = slice(p * pr, (p + 1) * pr)
            decay = jnp.exp(jnp.where(le, d_["gcol"][rp] - d_["grow"][:, rp], NEG))
            lower = jnp.where(strict, _dot_nt(d_["kbb"][rp], d_["kbf"][rp]) * decay, 0.0)
            intra[i, p] = (_dot_nt(d_["qbf"][rp], d_["kbf"][rp]) * decay).astype(BF16)
            mpow[i, p] = -lower
            tinv[i, p] = eye - lower
        for _ in range(5):
            for key in inst:
                mb = mpow[key].astype(BF16)
                mpow[key] = _dot(mb, mb)
            for key in inst:
                tinv[key] = tinv[key] + _dot(tinv[key].astype(BF16), mpow[key].astype(BF16))
        wv_, kc = {}, {}
        for (i, p) in inst:
            wk = _dot(tinv[i, p].astype(BF16), hv[i]["rhs"][p * pr:(p + 1) * pr])
            wv_[i, p] = wk[:, :dk]
            kc[i, p] = wk[:, dk:]

        st = []
        for i, h in enumerate(heads):
            vnew_ref[i] = jnp.zeros(vnew_ref.shape[1:], vnew_ref.dtype)
            st.append(state_ref[h])
        outs = [[] for _ in heads]
        for j in range(nch):
            p, jj = j // 2, j % 2
            rows = slice(j * C_CHUNK, (j + 1) * C_CHUNK)
            lrows = slice(jj * C_CHUNK, (jj + 1) * C_CHUNK)
            rp = slice(p * pr, (p + 1) * pr)
            a1 = [_dot(jnp.concatenate([kc[i, p][lrows], hv[i]["qe"][rows]], axis=0).astype(BF16),
                       st[i].astype(BF16)) for i in range(C_GROUP)]
            for i in range(C_GROUP):
                vnew_ref[i, rows, :] = (wv_[i, p][lrows] - a1[i][:C_CHUNK]).astype(BF16)
            for i in range(C_GROUP):
                vn_pair = vnew_ref[i, rp, :]
                outs[i].append(a1[i][C_CHUNK:] + _dot(intra[i, p][lrows, :], vn_pair))
                kt_j = jnp.where(csel == jj, hv[i]["ktil_t"][:, rp], jnp.zeros((), BF16))
                dg = jnp.exp(jnp.sum(jnp.where(csel == jj, hv[i]["glrow"][:, rp], 0.0), axis=1, keepdims=True)
                             * (1.0 / C_CHUNK))
                st[i] = st[i] * dg + _dot(kt_j, vn_pair)
        for i, h in enumerate(heads):
            state_ref[h] = st[i]
            o = jnp.concatenate(outs[i], axis=0)
            o = o * lax.rsqrt(jnp.mean(o * o, axis=-1, keepdims=True) + RMS_EPS) * nw_ref[...]
            c0 = hv[i]["c0"]
            z = qkvz_ref[:, pl.ds(pl.multiple_of(c0 + 3 * hd, dk), dk)].astype(F32)
            o_ref[:, pl.ds(c0, dk)] = (o * (z * _sigmoid(z))).astype(o_ref.dtype)
        return carry

    lax.fori_loop(0, C_HEADS // C_GROUP, head_group, 0)
    ext_ref[0:8, :] = ext_ref[ts:ts + 8, :]


def _mixer_c(x2, w_in, conv_w, a_log, dt_bias, norm_w, w_out, g, b, bsz, seq):
    t, d = x2.shape
    hd = C_HEADS * C_HEAD_DIM
    qkvz = _proj(x2, w_in[:, :4 * hd].astype(BF16), BF16)
    wg = jnp.zeros((d, LANES), F32).at[:, :2 * C_HEADS].set(w_in[:, 4 * hd:])
    gates = _proj3(x2, wg)
    col = lambda v: jnp.zeros((LANES, 1), F32).at[C_HEADS:2 * C_HEADS, 0].set(v.astype(F32))
    nsteps = seq // C_SUPER
    o = pl.pallas_call(
        _delta_body,
        grid=(bsz, nsteps),
        in_specs=[pl.BlockSpec((C_SUPER, 4 * hd), lambda bi, i: (bi * nsteps + i, 0)),
                  pl.BlockSpec((C_SUPER, LANES), lambda bi, i: (bi * nsteps + i, 0)),
                  _const_spec((C_CONV, 3 * hd)), _const_spec((LANES, 1)), _const_spec((LANES, 1)),
                  _const_spec((1, C_HEAD_DIM))],
        out_specs=pl.BlockSpec((C_SUPER, hd), lambda bi, i: (bi * nsteps + i, 0)),
        out_shape=jax.ShapeDtypeStruct((t, hd), BF16),
        scratch_shapes=[pltpu.VMEM((C_SUPER + 8, 3 * hd), F32),
                        pltpu.VMEM((C_HEADS, C_HEAD_DIM, C_HEAD_DIM), F32),
                        pltpu.VMEM((C_GROUP, C_SUPER, C_HEAD_DIM), BF16)],
        compiler_params=_cparams(2),
        name="deltanet",
    )(qkvz, gates, conv_w.astype(F32), col(a_log), col(dt_bias), norm_w.reshape(1, -1).astype(F32))
    return _mm_res_ln(o, w_out.astype(BF16), x2, g, b)


def _route_rows(x, wh_ref, wl_ref, o_ref, g1_ref, g2_ref, cnt_ref, run_ref, tri_ref):
    tm = x.shape[0]

    @pl.when(pl.program_id(0) == 0)
    def _():
        run_ref[...] = jnp.zeros_like(run_ref)
        r_i = lax.broadcasted_iota(I32, (tm, tm), 0)
        c_i = lax.broadcasted_iota(I32, (tm, tm), 1)
        tri_ref[...] = jnp.where(c_i < r_i, 1.0, 0.0).astype(BF16)

    xh = x.astype(BF16)
    xl = (x - xh.astype(F32)).astype(BF16)
    wh = wh_ref[...]
    logits = _dot(xh, wh) + _dot(xl, wh) + _dot(xh, wl_ref[...])
    lane = lax.broadcasted_iota(I32, (1, LANES), 1)
    lane_f = lane.astype(F32)

    def top1(vals):
        m = jnp.max(vals, axis=-1, keepdims=True)
        idx = jnp.min(jnp.where(vals == m, lane_f, 1e9), axis=-1, keepdims=True)
        return m, idx.astype(I32)

    lc = jnp.where(lane < MOE_GROUPS, logits, NEG)
    mc, grp = top1(lc)
    p_grp = 1.0 / jnp.sum(jnp.exp(lc - mc), axis=-1, keepdims=True)
    lo = MOE_GROUPS + MOE_PER_GROUP * grp
    lf = jnp.where(jnp.logical_and(lane >= lo, lane < lo + MOE_PER_GROUP), logits, NEG)
    m1, i1 = top1(lf)
    m2, i2 = top1(jnp.where(lane == i1, NEG, lf))
    e21 = jnp.exp(m2 - m1)
    g1 = p_grp / (1.0 + e21)
    g2 = p_grp * e21 / (1.0 + e21)
    ex1 = i1 - MOE_GROUPS
    ex2 = i2 - MOE_GROUPS

    oh1 = jnp.where(lane == ex1, 1.0, 0.0).astype(F32)
    oh2 = jnp.where(lane == ex2, 1.0, 0.0).astype(F32)
    ohs = oh1 + oh2
    before = _dot(tri_ref[...], ohs.astype(BF16)) + run_ref[...]
    rank1 = jnp.sum(oh1 * before, axis=-1, keepdims=True)
    rank2 = jnp.sum(oh2 * before, axis=-1, keepdims=True)
    run_ref[...] = run_ref[...] + jnp.sum(ohs, axis=0, keepdims=True)

    out = jnp.where(lane == 0, ex1.astype(F32), 0.0)
    out = jnp.where(lane == 1, ex2.astype(F32), out)
    out = jnp.where(lane == 2, g1, out)
    out = jnp.where(lane == 3, g2, out)
    out = jnp.where(lane == 4, rank1, out)
    out = jnp.where(lane == 5, rank2, out)
    o_ref[...] = out
    g1_ref[...] = jnp.broadcast_to(g1, g1_ref.shape)
    g2_ref[...] = jnp.broadcast_to(g2, g2_ref.shape)
    cnt_ref[...] = jnp.broadcast_to(run_ref[...], cnt_ref.shape)


def _route_weights(w_coarse, w_fine):
    d = w_coarse.shape[0]
    wr = jnp.zeros((d, LANES), F32)
    wr = wr.at[:, :MOE_GROUPS].set(w_coarse).at[:, MOE_GROUPS:MOE_GROUPS + MOE_EXPERTS].set(
        w_fine.reshape(d, MOE_EXPERTS))
    wh = wr.astype(BF16)
    wl = (wr - wh.astype(F32)).astype(BF16)
    return wh, wl


def _route_body(x_ref, wh_ref, wl_ref, o_ref, g1_ref, g2_ref, cnt_ref, run_ref, tri_ref):
    _route_rows(x_ref[...], wh_ref, wl_ref, o_ref, g1_ref, g2_ref, cnt_ref, run_ref, tri_ref)


def _route(x2, w_coarse, w_fine, tm=512):
    t, d = x2.shape
    wh, wl = _route_weights(w_coarse, w_fine)
    row = pl.BlockSpec((tm, LANES), lambda i: (i, 0))
    return pl.pallas_call(
        _route_body,
        grid=(t // tm,),
        in_specs=[pl.BlockSpec((tm, d), lambda i: (i, 0)), _const_spec((d, LANES)), _const_spec((d, LANES))],
        out_specs=[row, row, row, pl.BlockSpec((8, LANES), lambda i: (0, 0))],
        out_shape=[jax.ShapeDtypeStruct((t, LANES), F32), jax.ShapeDtypeStruct((t, LANES), F32),
                   jax.ShapeDtypeStruct((t, LANES), F32), jax.ShapeDtypeStruct((8, LANES), F32)],
        scratch_shapes=[pltpu.VMEM((1, LANES), F32), pltpu.VMEM((tm, tm), BF16)],
        compiler_params=_cparams(1),
        name="moe_route",
    )(x2, wh, wl)


def _dispatch_body(p1_ref, p2_ref, x_ref, o_hbm, slab_ref, sem, *, tm):
    s = pl.program_id(0)
    i = pl.program_id(1)

    @pl.when(jnp.logical_and(s == 0, i == 0))
    def _():
        slab_ref[...] = jnp.zeros_like(slab_ref)

    def row(r, carry):
        v = x_ref[pl.ds(r, 1), :]
        slab_ref[pl.ds(p1_ref[0, r], 1), :] = v
        slab_ref[pl.ds(p2_ref[0, r], 1), :] = v
        return carry
    lax.fori_loop(0, tm, row, 0, unroll=ROW_UNROLL)

    @pl.when(i == pl.num_programs(1) - 1)
    def _():
        cp = pltpu.make_async_copy(slab_ref, o_hbm.at[s], sem)
        cp.start()
        cp.wait()


def _dispatch(xp, pos1, pos2, s_pad, tm=1024):
    t, w = xp.shape
    nslab = w // LANES
    nt = t // tm
    return pl.pallas_call(
        functools.partial(_dispatch_body, tm=tm),
        grid=(nslab, nt),
        in_specs=[pl.BlockSpec((None, 1, tm), lambda s, i: (i, 0, 0), memory_space=pltpu.SMEM),
                  pl.BlockSpec((None, 1, tm), lambda s, i: (i, 0, 0), memory_space=pltpu.SMEM),
                  pl.BlockSpec((tm, LANES), lambda s, i: (i, s))],
        out_specs=pl.BlockSpec(memory_space=pl.ANY),
        out_shape=jax.ShapeDtypeStruct((nslab, s_pad, LANES), U32),
        scratch_shapes=[pltpu.VMEM((s_pad, LANES), U32), pltpu.SemaphoreType.DMA(())],
        compiler_params=_cparams(2),
        name="moe_dispatch",
    )(pos1.reshape(nt, 1, tm), pos2.reshape(nt, 1, tm), xp)


def _expert_body(te_ref, nv_ref, first_ref, nxt_ref, slot_ref, xs_ref, wg_hbm, wu_hbm, wd_hbm, y_ref,
                 wgb_ref, wub_ref, wdb_ref, wg_buf, wu_buf, wd_buf, sem, *, layer):
    i = pl.program_id(0)
    half = D_MODEL // 2
    nslab = half // LANES

    def copies(e, sl):
        return (pltpu.make_async_copy(wg_hbm.at[layer, e], wg_buf.at[sl], sem.at[sl, 0]),
                pltpu.make_async_copy(wu_hbm.at[layer, e], wu_buf.at[sl], sem.at[sl, 1]),
                pltpu.make_async_copy(wd_hbm.at[layer, e], wd_buf.at[sl], sem.at[sl, 2]))

    @pl.when(i < nv_ref[0])
    def _():
        @pl.when(first_ref[i] == 1)
        def _():
            sl = slot_ref[i]

            @pl.when(i == 0)
            def _():
                for cp in copies(te_ref[0], sl):
                    cp.start()

            for cp in copies(te_ref[i], sl):
                cp.wait()

            @pl.when(nxt_ref[i] >= 0)
            def _():
                for cp in copies(nxt_ref[i], 1 - sl):
                    cp.start()

            wgb_ref[...] = wg_buf[sl].astype(BF16)
            wub_ref[...] = wu_buf[sl].astype(BF16)
            wdb_ref[...] = wd_buf[sl].astype(BF16)

        xlo = jnp.concatenate([_unpack_lo(xs_ref[s]) for s in range(nslab)], axis=1).astype(BF16)
        xhi = jnp.concatenate([_unpack_hi(xs_ref[s]) for s in range(nslab)], axis=1).astype(BF16)
        hg =_dot(xlo, wgb_ref[0:half, :]) + _dot(xhi, wgb_ref[half:, :])
        hu = _dot(xlo, wub_ref[0:half, :]) + _dot(xhi, wub_ref[half:, :])
        h = (hg * _sigmoid(hg) * hu).astype(BF16)
        packed = _pack_pairs(_dot(h, wdb_ref[...]))
        for s in range(nslab):
            y_ref[s] = packed[:, s * LANES:(s + 1) * LANES]

    @pl.when(i >= nv_ref[0])
    def _():
        y_ref[...] = jnp.zeros_like(y_ref)


def _experts(xs, tile_expert, nvalid, first, nxt, slot, w_gate, w_up, w_down, layer, n_tiles):
    tm = MOE_TM
    d, hdn = D_MODEL, MOE_HIDDEN
    nslab = d // 2 // LANES
    any_spec = pl.BlockSpec(memory_space=pl.ANY)
    return pl.pallas_call(
        functools.partial(_expert_body, layer=layer),
        grid_spec=pltpu.PrefetchScalarGridSpec(
            num_scalar_prefetch=5,
            grid=(n_tiles,),
            in_specs=[pl.BlockSpec((nslab, tm, LANES), lambda i, te, nv, *_: (0, jnp.minimum(i, nv[0] - 1), 0)),
                      any_spec, any_spec, any_spec],
            out_specs=pl.BlockSpec((nslab, tm, LANES), lambda i, *_: (0, i, 0)),
            scratch_shapes=[pltpu.VMEM((d, hdn), BF16), pltpu.VMEM((d, hdn), BF16), pltpu.VMEM((hdn, d), BF16),
                            pltpu.VMEM((2, d, hdn), F32), pltpu.VMEM((2, d, hdn), F32),
                            pltpu.VMEM((2, hdn, d), F32), pltpu.SemaphoreType.DMA((2, 3))],
        ),
        out_shape=jax.ShapeDtypeStruct((nslab, n_tiles * tm, LANES), U32),
        compiler_params=_cparams(1),
        name="moe_experts",
    )(tile_expert, nvalid, first, nxt, slot, xs, w_gate, w_up, w_down)


def _combine_body(p1_ref, p2_ref, y_ref, g1_ref, g2_ref, lo_ref, hi_ref, b1_ref, b2_ref, *, tm):
    def row(r, carry):
        b1_ref[pl.ds(r, 1), :] = y_ref[pl.ds(p1_ref[0, r], 1), :]
        b2_ref[pl.ds(r, 1), :] = y_ref[pl.ds(p2_ref[0, r], 1), :]
        return carry
    lax.fori_loop(0, tm, row, 0, unroll=ROW_UNROLL)
    g1 = g1_ref[...]
    g2 = g2_ref[...]
    y1 = b1_ref[...]
    y2 = b2_ref[...]
    lo_ref[...] = g1 * _unpack_lo(y1) + g2 * _unpack_lo(y2)
    hi_ref[...] = g1 * _unpack_hi(y1) + g2 * _unpack_hi(y2)


def _combine(y, pos1, pos2, g1b, g2b, tm=1024):
    nslab, s_pad, _ = y.shape
    t = pos1.shape[0]
    nt = t // tm
    return pl.pallas_call(
        functools.partial(_combine_body, tm=tm),
        grid=(nslab, nt),
        in_specs=[pl.BlockSpec((None, 1, tm), lambda s, i: (i, 0, 0), memory_space=pltpu.SMEM),
                  pl.BlockSpec((None, 1, tm), lambda s, i: (i, 0, 0), memory_space=pltpu.SMEM),
                  pl.BlockSpec((None, s_pad, LANES), lambda s, i: (s, 0, 0), pipeline_mode=pl.Buffered(1)),
                  pl.BlockSpec((tm, LANES), lambda s, i: (i, 0)), pl.BlockSpec((tm, LANES), lambda s, i: (i, 0))],
        out_specs=[pl.BlockSpec((tm, LANES), lambda s, i: (i, s)), pl.BlockSpec((tm, LANES), lambda s, i: (i, s))],
        out_shape=[jax.ShapeDtypeStruct((t, nslab * LANES), F32), jax.ShapeDtypeStruct((t, nslab * LANES), F32)],
        scratch_shapes=[pltpu.VMEM((tm, LANES), U32), pltpu.VMEM((tm, LANES), U32)],
        compiler_params=_cparams(2),
        name="moe_combine",
    )(pos1.reshape(nt, 1, tm), pos2.reshape(nt, 1, tm), y, g1b, g2b)


def _res_ln_body(x_ref, lo_ref, hi_ref, g_ref, b_ref, o_ref):
    h = jnp.concatenate([lo_ref[...], hi_ref[...]], axis=1)
    o_ref[...] = _ln_rows(ALPHA * x_ref[...] + h, g_ref[...], b_ref[...])


def _res_ln(x2, lo, hi, g, b, tm=1024):
    t, d = x2.shape
    return pl.pallas_call(
        _res_ln_body,
        grid=(t // tm,),
        in_specs=[pl.BlockSpec((tm, d), lambda i: (i, 0)), pl.BlockSpec((tm, d // 2), lambda i: (i, 0)),
                  pl.BlockSpec((tm, d // 2), lambda i: (i, 0)), _const_spec((1, d)), _const_spec((1, d))],
        out_specs=pl.BlockSpec((tm, d), lambda i: (i, 0)),
        out_shape=jax.ShapeDtypeStruct((t, d), F32),
        compiler_params=_cparams(1),
        name="res_ln",
    )(x2, lo, hi, g.reshape(1, d), b.reshape(1, d))


def _moe(x2, xp, w_coarse, w_fine, w_gate, w_up, w_down, layer, g, b):
    t = x2.shape[0]
    tm = MOE_TM
    n_tiles = (2 * t) // tm + MOE_EXPERTS
    route, g1b, g2b, cnt = _route(x2, w_coarse, w_fine)
    ex = route[:, 0:2].astype(I32)
    rank = route[:, 4:6].astype(I32)
    counts = cnt[0, :MOE_EXPERTS].astype(I32)
    ptiles = (counts + tm - 1) // tm
    tile_end = jnp.cumsum(ptiles)
    pstart = (tile_end - ptiles) * tm
    eids = jnp.arange(MOE_EXPERTS, dtype=I32)
    pos = jnp.sum(jnp.where(ex[:, :, None] == eids, pstart, 0), axis=-1) + rank
    nvalid = tile_end[-1:].astype(I32)
    tidx = jnp.arange(n_tiles, dtype=I32)
    te = jnp.sum((tile_end[None, :] <= jnp.minimum(tidx, nvalid[0] - 1)[:, None]).astype(I32), axis=1)
    nonempty = ptiles > 0
    slot_e = (jnp.cumsum(nonempty.astype(I32)) - 1) & 1
    later = jnp.where(nonempty, eids, MOE_EXPERTS)
    nxt_incl = lax.cummin(later, axis=0, reverse=True)
    nxt_e = jnp.concatenate([nxt_incl[1:], jnp.full((1,), MOE_EXPERTS, I32)])
    nxt_e = jnp.where(nxt_e >= MOE_EXPERTS, -1, nxt_e)
    first = jnp.logical_and(tidx < nvalid[0],
                            jnp.logical_or(tidx == 0, te != jnp.concatenate([te[:1], te[:-1]]))).astype(I32)
    xs = _dispatch(xp, pos[:, 0], pos[:, 1], n_tiles * tm)
    y = _experts(xs, te, nvalid, first, nxt_e[te], slot_e[te], w_gate, w_up, w_down, layer, n_tiles)
    lo, hi = _combine(y, pos[:, 0], pos[:, 1], g1b, g2b)
    return _res_ln(x2, lo, hi, g, b)


def kernel(x, rel_bias, a_w_in, a_w_out, b_w_in, b_norm_g, b_norm_b, b_w_s, b_b_s, b_w_out, c_w_in, c_conv,
           c_a_log, c_dt_bias, c_norm_w, c_w_out, ln_g, ln_b, moe_w_coarse, moe_w_fine, moe_w_gate, moe_w_up,
           moe_w_down):
    bsz, seq, d = x.shape
    x2 = x.reshape(bsz * seq, d)
    for i in range(DEPTH):
        kind, j = i % 3, i // 3
        g1, b1 = ln_g[i, 0], ln_b[i, 0]
        if kind == 0:
            x2, xp = _mixer_a(x2, a_w_in[j], a_w_out[j], rel_bias, g1, b1, bsz, seq)
        elif kind == 1:
            x2, xp = _mixer_b(x2, b_w_in[j], b_norm_g[j], b_norm_b[j], b_w_s[j], b_b_s[j], b_w_out[j], g1, b1)
        else:
            x2, xp = _mixer_c(x2, c_w_in[j], c_conv[j], c_a_log[j], c_dt_bias[j], c_norm_w[j], c_w_out[j],
                              g1, b1, bsz, seq)
        x2 = _moe(x2, xp, moe_w_coarse[i], moe_w_fine[i], moe_w_gate, moe_w_up, moe_w_down, i,
                  ln_g[i, 1], ln_b[i, 1])
    return x2.reshape(bsz, seq, d)
```

```python
import functools
import math

import numpy as np
import jax
import jax.numpy as jnp
from jax import lax
from jax.experimental import pallas as pl
from jax.experimental.pallas import tpu as pltpu

F32 = jnp.float32
BF16 = jnp.bfloat16
U32 = jnp.uint32
I32 = jnp.int32

D_MODEL = 1024
DEPTH = 4
A_GROUPS = ((128, 1), (512, 4), (2048, 16))
A_HEADS = 16
A_HEAD_DIM = 64
A_BLOCK = 128
NUM_BUCKETS = 32
MAX_DISTANCE = 2048
B_CHUNK = 128
B_WIDTH = 2 * D_MODEL
B_GROUPS = 16
C_HEADS = 8
C_HEAD_DIM = 128
C_CONV = 4
C_CHUNK = 64
MOE_GROUPS = 8
MOE_PER_GROUP = 8
MOE_EXPERTS = 64
MOE_HIDDEN = 512
LN_EPS = 1e-5
RMS_EPS = 1e-6
ALPHA = (2 * DEPTH) ** 0.25

LANES = 128
NEG = -1e30
LOG2E = 1.4426950408889634
LN2 = 0.6931471805599453
VMEM_LIMIT = 56 * 1024 * 1024
MOE_TM = 512
ROW_UNROLL = 16
MOE_PARTS = 2
C_SUPER = 256
C_GROUP = 8
A_HP_GROUP = 2


def _cparams(n_axes, vmem=VMEM_LIMIT):
    return pltpu.CompilerParams(dimension_semantics=("arbitrary",) * n_axes, vmem_limit_bytes=vmem)


def _const_spec(shape):
    nd = len(shape)
    return pl.BlockSpec(shape, lambda *_: (0,) * nd, pipeline_mode=pl.Buffered(1))


def _ln_rows(y, g, b):
    mu = jnp.mean(y, axis=-1, keepdims=True)
    yc = y - mu
    var = jnp.mean(yc * yc, axis=-1, keepdims=True)
    return yc * lax.rsqrt(var + LN_EPS) * g + b


def _pack_pairs(y):
    w = y.shape[1] // 2
    lo = lax.bitcast_convert_type(y[:, :w].astype(BF16).astype(F32), U32)
    hi = lax.bitcast_convert_type(y[:, w:].astype(BF16).astype(F32), U32)
    return (lo >> 16) | (hi & jnp.uint32(0xFFFF0000))


def _unpack_lo(p):
    return lax.bitcast_convert_type(p << 16, F32)


def _unpack_hi(p):
    return lax.bitcast_convert_type(p & jnp.uint32(0xFFFF0000), F32)


def _split3(a):
    h = a.astype(BF16)
    r = a - h.astype(F32)
    m = r.astype(BF16)
    l = (r - m.astype(F32)).astype(BF16)
    return h, m, l


def _dot(a, b):
    return jnp.dot(a, b, preferred_element_type=F32)


def _dot_nt(a, b):
    return lax.dot_general(a, b, (((1,), (1,)), ((), ())), preferred_element_type=F32)


def _gelu_tanh(x):
    return 0.5 * x * (1.0 + jnp.tanh(0.7978845608028654 * (x + 0.044715 * (x * x * x))))


def _sigmoid(x):
    return 0.5 * jnp.tanh(0.5 * x) + 0.5


def _proj_body(x_ref, w_ref, o_ref):
    xb = x_ref[...].astype(BF16)
    cw = 1024
    for c in range(w_ref.shape[1] // cw):
        o_ref[:, c * cw:(c + 1) * cw] = _dot(xb, w_ref[:, c * cw:(c + 1) * cw]).astype(o_ref.dtype)


def _proj(x2, w, out_dtype, tm=1024):
    t, k = x2.shape
    n = w.shape[1]
    return pl.pallas_call(
        _proj_body,
        grid=(t // tm,),
        in_specs=[pl.BlockSpec((tm, k), lambda i: (i, 0)), _const_spec((k, n))],
        out_specs=pl.BlockSpec((tm, n), lambda i: (i, 0)),
        out_shape=jax.ShapeDtypeStruct((t, n), out_dtype),
        compiler_params=_cparams(1),
        name="proj",
    )(x2, w)


def _proj3_body(x_ref, wh_ref, wl_ref, o_ref):
    x = x_ref[...]
    xh = x.astype(BF16)
    xl = (x - xh.astype(F32)).astype(BF16)
    wh = wh_ref[...]
    o_ref[...] = _dot(xh, wh) + _dot(xl, wh) + _dot(xh, wl_ref[...])


def _proj3(x2, w, tm=1024):
    t, k = x2.shape
    n = w.shape[1]
    wh = w.astype(BF16)
    wl = (w - wh.astype(F32)).astype(BF16)
    return pl.pallas_call(
        _proj3_body,
        grid=(t // tm,),
        in_specs=[pl.BlockSpec((tm, k), lambda i: (i, 0)), _const_spec((k, n)), _const_spec((k, n))],
        out_specs=pl.BlockSpec((tm, n), lambda i: (i, 0)),
        out_shape=jax.ShapeDtypeStruct((t, n), F32),
        compiler_params=_cparams(1),
        name="proj3",
    )(x2, wh, wl)


def _mm_res_ln_body(a_ref, w_ref, x_ref, g_ref, b_ref, o_ref, p_ref):
    y = _dot(a_ref[...], w_ref[...])
    xn = _ln_rows(ALPHA * x_ref[...] + y, g_ref[...], b_ref[...])
    o_ref[...] = xn
    p_ref[...] = _pack_pairs(xn)


def _mm_res_ln(a, w, x2, g, b, tm=512):
    t, k = a.shape
    d = w.shape[1]
    return pl.pallas_call(
        _mm_res_ln_body,
        grid=(t // tm,),
        in_specs=[pl.BlockSpec((tm, k), lambda i: (i, 0)), _const_spec((k, d)),
                  pl.BlockSpec((tm, d), lambda i: (i, 0)), _const_spec((1, d)), _const_spec((1, d))],
        out_specs=[pl.BlockSpec((tm, d), lambda i: (i, 0)), pl.BlockSpec((tm, d // 2), lambda i: (i, 0))],
        out_shape=[jax.ShapeDtypeStruct((t, d), F32), jax.ShapeDtypeStruct((t, d // 2), U32)],
        compiler_params=_cparams(1),
        name="mm_res_ln",
    )(a, w, x2, g.reshape(1, d), b.reshape(1, d))


def _t5_bucket(dist):
    max_exact = NUM_BUCKETS // 2
    d = jnp.maximum(dist, 1).astype(F32)
    large = max_exact + (jnp.log(d / max_exact) / math.log(MAX_DISTANCE / max_exact)
                         * (NUM_BUCKETS - max_exact)).astype(I32)
    return jnp.where(dist < max_exact, dist, jnp.minimum(large, NUM_BUCKETS - 1))


def _attn_bias(rel_bias, window, dil):
    steps = window // dil
    qi = jnp.arange(A_BLOCK)[:, None]
    ki = jnp.arange(2 * A_BLOCK)[None, :]
    rel = qi + A_BLOCK - ki
    valid = (rel >= 0) & (rel <= steps)
    bucket = _t5_bucket(jnp.maximum(rel, 0) * dil)
    bias = jnp.zeros((A_HEADS, A_BLOCK, 2 * A_BLOCK), F32)
    for bkt in range(NUM_BUCKETS):
        bias = jnp.where((bucket == bkt)[None], rel_bias[bkt].astype(F32)[:, None, None], bias)
    return jnp.where(valid[None], bias * LOG2E, NEG)


def _attn_body(q_ref, k_ref, v_ref, kp_ref, vp_ref, bias_ref, o_ref, lse_ref, *, nres, nblk):
    li = pl.program_id(2)
    lane = lax.broadcasted_iota(I32, (1, LANES), 1)
    col2 = lax.broadcasted_iota(I32, (1, 2 * A_BLOCK), 1)
    first_pen = jnp.where(col2 < A_BLOCK, jnp.where(li == 0, NEG, 0.0).astype(F32), 0.0)
    head0 = lane < A_HEAD_DIM
    lse_ref[...] = jnp.zeros_like(lse_ref)

    def hp_stages(hp):
        c0 = pl.multiple_of(hp * LANES, LANES)
        cols = pl.ds(c0, LANES)
        blocks = [(r, j) for r in range(nres) for j in range(nblk)]
        bias2 = jnp.concatenate([bias_ref[2 * hp], bias_ref[2 * hp + 1]], axis=0)
        ones = jnp.ones((2 * A_BLOCK, LANES), BF16)
        vbs, s_, p_, m_ = {}, {}, {}, {}

        def scores(u):
            r, j = u
            rows = pl.ds(j * A_BLOCK, A_BLOCK)
            q = q_ref[r, rows, cols]
            if j == 0:
                kb = jnp.concatenate([kp_ref[r, :, cols], k_ref[r, rows, cols]], axis=0)
                vbs[u] = jnp.concatenate([vp_ref[r, :, cols], v_ref[r, rows, cols]], axis=0)
            else:
                band = pl.ds((j - 1) * A_BLOCK, 2 * A_BLOCK)
                kb = k_ref[r, band, cols]
                vbs[u] = v_ref[r, band, cols]
            zero = jnp.zeros_like(q)
            q2 = jnp.concatenate([jnp.where(head0, q, zero), jnp.where(head0, zero, q)], axis=0)
            s = _dot_nt(q2, kb) + bias2
            s_[u] = s + first_pen if j == 0 else s

        def softmax(u):
            m_[u] = jnp.max(s_[u], axis=-1, keepdims=True)
            p_[u] = jnp.exp2(s_[u] - m_[u]).astype(BF16)

        def values(u):
            r, j = u
            rows = pl.ds(j * A_BLOCK, A_BLOCK)
            pvl = _dot(p_[u], jnp.concatenate([vbs[u], ones], axis=1))
            den = pvl[:, LANES:]
            o = pvl[:, :LANES] / den
            lse = m_[u] * LN2 + jnp.log(den)
            o_ref[r, rows, cols] = jnp.where(head0, o[:A_BLOCK], o[A_BLOCK:]).astype(o_ref.dtype)
            cur = lse_ref[r, rows, :]
            cur = jnp.where(lane == 2 * hp, lse[:A_BLOCK], cur)
            cur = jnp.where(lane == 2 * hp + 1, lse[A_BLOCK:], cur)
            lse_ref[r, rows, :] = cur

        return blocks, scores, softmax, values

    def hp_pair(hh, carry):
        stages = [hp_stages(A_HP_GROUP * hh + i) for i in range(A_HP_GROUP)]
        for k in range(3):
            for blocks, *fns in stages:
                for u in blocks:
                    fns[k](u)
        return carry

    lax.fori_loop(0, A_HEADS // 2 // A_HP_GROUP, hp_pair, 0)


def _proj_perm_body(x_ref, w_ref, o_ref, xb_ref, y_ref, *, dil):
    tm, k = x_ref.shape
    n = tm // dil

    @pl.when(pl.program_id(1) == 0)
    def _():
        if dil == 1:
            xb_ref[...] = x_ref[...].astype(BF16)
        else:
            nc = k // LANES
            for c in range(nc):
                y_ref[c] = x_ref[:, c * LANES:(c + 1) * LANES]
            for r in range(dil):
                xb_ref[r * n:(r + 1) * n, :] = jnp.concatenate(
                    [y_ref[c, pl.ds(r, n, stride=dil), :] for c in range(nc)], axis=1).astype(BF16)

    cw = 1024
    for c in range(w_ref.shape[1] // cw):
        y = _dot(xb_ref[...], w_ref[:, c * cw:(c + 1) * cw])
        for r in range(dil):
            o_ref[r, :, c * cw:(c + 1) * cw] = y[r * n:(r + 1) * n].astype(o_ref.dtype)


def _proj_perm(x2, w, g, dil, bsz, seq, tm=1024, tn=3072):
    t, k = x2.shape
    hd = A_HEADS * A_HEAD_DIM
    tpb = seq // tm
    return pl.pallas_call(
        functools.partial(_proj_perm_body, dil=dil),
        grid=(t // tm, 3 * hd // tn),
        in_specs=[pl.BlockSpec((tm, k), lambda i, j: (i, 0)),
                  pl.BlockSpec((k, tn), lambda i, j: (0, g * (3 * hd // tn) + j))],
        out_specs=pl.BlockSpec((None, dil, tm // dil, tn), lambda i, j: (i // tpb, 0, i % tpb, j)),
        out_shape=jax.ShapeDtypeStruct((bsz, dil, seq // dil, 3 * hd), BF16),
        scratch_shapes=[pltpu.VMEM((tm, k), BF16), pltpu.VMEM((k // LANES, tm, LANES), F32)],
        compiler_params=_cparams(2),
        name=f"proj_d{dil}",
    )(x2, w)


def _dilated_group(qkv, dil, bias, bsz, seq):
    hd = A_HEADS * A_HEAD_DIM
    L = seq // dil
    rows_per_step = 512
    tl = min(rows_per_step, L)
    nblk = tl // A_BLOCK
    nres = rows_per_step // tl

    def blk(col):
        return pl.BlockSpec((None, nres, tl, hd), lambda b, r, li: (b, r, li, col))

    def prev(col):
        return pl.BlockSpec((None, nres, A_BLOCK, hd),
                            lambda b, r, li: (b, r, jnp.maximum(li * nblk - 1, 0), col))

    return pl.pallas_call(
        functools.partial(_attn_body, nres=nres, nblk=nblk),
        grid=(bsz, dil // nres, L // tl),
        in_specs=[blk(0), blk(1), blk(2), prev(1), prev(2), _const_spec((A_HEADS, A_BLOCK, 2 * A_BLOCK))],
        out_specs=[pl.BlockSpec((None, nres, tl, hd), lambda b, r, li: (b, r, li, 0)),
                   pl.BlockSpec((None, nres, tl, LANES), lambda b, r, li: (b, r, li, 0))],
        out_shape=[jax.ShapeDtypeStruct((bsz, dil, L, hd), BF16),
                   jax.ShapeDtypeStruct((bsz, dil, L, LANES), F32)],
        compiler_params=_cparams(3),
        name=f"dilated_attn_d{dil}",
    )(qkv, qkv, qkv, qkv, qkv, bias)


def _attn_out_body(o1_ref, o2_ref, o3_ref, l1_ref, l2_ref, l3_ref, e_ref, w_ref, x_ref, g_ref, b_ref,
                   xo_ref, p_ref, so_ref, sl2_ref, sl3_ref, *, dils):
    def natural(ref, scr, dil):
        if dil == 1:
            return ref[0].astype(F32)
        n = ref.shape[1]
        nc = ref.shape[2] // LANES
        for r in range(dil):
            blk = ref[r].astype(F32)
            for c in range(nc):
                scr[c, pl.ds(r, n, stride=dil), :] = blk[:, c * LANES:(c + 1) * LANES]
        return jnp.concatenate([scr[c] for c in range(nc)], axis=1) if nc > 1 else scr[0]

    l1 = natural(l1_ref, None, dils[0])
    l2 = natural(l2_ref, sl2_ref, dils[1])
    l3 = natural(l3_ref, sl3_ref, dils[2])
    m = jnp.maximum(jnp.maximum(l1, l2), l3)
    e1, e2, e3 = jnp.exp(l1 - m), jnp.exp(l2 - m), jnp.exp(l3 - m)
    inv = 1.0 / (e1 + e2 + e3)
    ex = e_ref[...]

    def expand(wt):
        h = wt.astype(BF16)
        lo = (wt - h.astype(F32)).astype(BF16)
        return _dot(h, ex) + _dot(lo, ex)

    comb = expand(e1 * inv) * natural(o1_ref, None, dils[0])
    comb = comb + expand(e2 * inv) * natural(o2_ref, so_ref, dils[1])
    comb = comb + expand(e3 * inv) * natural(o3_ref, so_ref, dils[2])
    y = _dot(comb.astype(BF16), w_ref[...])
    xn = _ln_rows(ALPHA * x_ref[...] + y, g_ref[...], b_ref[...])
    xo_ref[...] = xn
    p_ref[...] = _pack_pairs(xn)


def _attn_out(os, lses, dils, w_out, x2, g, b, seq, tm=512):
    t, d = x2.shape
    tpb = seq // tm
    expand = np.zeros((LANES, d), np.float32)
    for h in range(A_HEADS):
        expand[h, h * A_HEAD_DIM:(h + 1) * A_HEAD_DIM] = 1.0
    row = lambda n: pl.BlockSpec((tm, n), lambda i: (i, 0))
    res = lambda dil, n: pl.BlockSpec((None, dil, tm // dil, n), lambda i: (i // tpb, 0, i % tpb, 0))
    return pl.pallas_call(
        functools.partial(_attn_out_body, dils=dils),
        grid=(t // tm,),
        in_specs=[res(dils[0], d), res(dils[1], d), res(dils[2], d),
                  res(dils[0], LANES), res(dils[1], LANES), res(dils[2], LANES),
                  _const_spec((LANES, d)), _const_spec((d, d)), row(d), _const_spec((1, d)), _const_spec((1, d))],
        out_specs=[row(d), row(d // 2)],
        out_shape=[jax.ShapeDtypeStruct((t, d), F32), jax.ShapeDtypeStruct((t, d // 2), U32)],
        scratch_shapes=[pltpu.VMEM((d // LANES, tm, LANES), F32), pltpu.VMEM((1, tm, LANES), F32),
                        pltpu.VMEM((1, tm, LANES), F32)],
        compiler_params=_cparams(1),
        name="attn_out",
    )(*os, *lses, jnp.asarray(expand, BF16), w_out, x2, g.reshape(1, d), b.reshape(1, d))


def _mixer_a(x2, w_in, w_out, rel_bias, g, b, bsz, seq):
    hd = A_HEADS * A_HEAD_DIM
    scale = np.ones((9 * hd,), np.float32)
    for gi in range(len(A_GROUPS)):
        scale[3 * gi * hd:(3 * gi + 1) * hd] = A_HEAD_DIM ** -0.5 * LOG2E
    wb = (w_in * scale).astype(BF16)
    os, lses = [], []
    for gi, (window, dil) in enumerate(A_GROUPS):
        qkv = _proj_perm(x2, wb, gi, dil, bsz, seq)
        o, lse = _dilated_group(qkv, dil, _attn_bias(rel_bias, window, dil), bsz, seq)
        os.append(o)
        lses.append(lse)
    return _attn_out(os, lses, tuple(dl for _, dl in A_GROUPS), w_out.astype(BF16), x2, g, b, seq)


def _sgu_body(x_ref, wu_ref, wv_ref, ng_ref, nb_ref, wc_ref, bs_ref, a_ref, vb_ref, *, tm):
    xb = x_ref[...].astype(BF16)
    v = _gelu_tanh(_dot(xb, wv_ref[...]))
    vb_ref[...] = _ln_rows(v, ng_ref[...], nb_ref[...]).astype(BF16)
    gw = B_WIDTH // B_GROUPS
    ucols = 512
    for j in range(B_WIDTH // ucols):
        u = _gelu_tanh(_dot(xb, wu_ref[:, j * ucols:(j + 1) * ucols]))
        nchunk = tm // B_CHUNK
        for gg in range(ucols // gw):
            gi = j * (ucols // gw) + gg
            cols = slice(gi * gw, (gi + 1) * gw)
            vcat = jnp.concatenate([vb_ref[c * B_CHUNK:(c + 1) * B_CHUNK, cols] for c in range(nchunk)], axis=1)
            fcat = _dot(wc_ref[gi], vcat)
            for c in range(nchunk):
                rows = slice(c * B_CHUNK, (c + 1) * B_CHUNK)
                f = fcat[:, c * gw:(c + 1) * gw] + bs_ref[:, cols]
                a_ref[rows, cols] = (u[rows, gg * gw:(gg + 1) * gw] * f).astype(a_ref.dtype)


def _mixer_b(x2, w_in, norm_g, norm_b, w_s, b_s, w_out, g, b, tm=512):
    t, d = x2.shape
    wu = w_in[:, :B_WIDTH].astype(BF16)
    wv = w_in[:, B_WIDTH:].astype(BF16)
    wc = (w_s * jnp.tril(jnp.ones((B_CHUNK, B_CHUNK), w_s.dtype))).astype(BF16)
    bs_full = jnp.repeat(b_s.T, B_WIDTH // B_GROUPS, axis=1)
    a = pl.pallas_call(
        functools.partial(_sgu_body, tm=tm),
        grid=(t // tm,),
        in_specs=[pl.BlockSpec((tm, d), lambda i: (i, 0)), _const_spec((d, B_WIDTH)), _const_spec((d, B_WIDTH)),
                  _const_spec((1, B_WIDTH)), _const_spec((1, B_WIDTH)),
                  _const_spec((B_GROUPS, B_CHUNK, B_CHUNK)), _const_spec((B_CHUNK, B_WIDTH))],
        out_specs=pl.BlockSpec((tm, B_WIDTH), lambda i: (i, 0)),
        out_shape=jax.ShapeDtypeStruct((t, B_WIDTH), BF16),
        scratch_shapes=[pltpu.VMEM((tm, B_WIDTH), BF16)],
        compiler_params=_cparams(1),
        name="sgu",
    )(x2, wu, wv, norm_g.reshape(1, -1), norm_b.reshape(1, -1), wc, bs_full)
    return _mm_res_ln(a, w_out.astype(BF16), x2, g, b)


def _delta_body(qkvz_ref, gates_ref, cw_ref, alog_ref, dtb_ref, nw_ref, o_ref, ext_ref, state_ref, vnew_ref):
    ts = C_SUPER
    hd = C_HEADS * C_HEAD_DIM
    dk = C_HEAD_DIM
    nch = ts // C_CHUNK
    pr = 2 * C_CHUNK

    @pl.when(pl.program_id(1) == 0)
    def _():
        ext_ref[0:8, :] = jnp.zeros((8, 3 * hd), F32)
        state_ref[...] = jnp.zeros_like(state_ref)

    ext_ref[8:8 + ts, :] = qkvz_ref[:, 0:3 * hd].astype(F32)

    r_i = lax.broadcasted_iota(I32, (ts, ts), 0)
    c_i = lax.broadcasted_iota(I32, (ts, ts), 1)
    same = (r_i >> 6) == (c_i >> 6)
    u_cum = jnp.where(jnp.logical_and(same, r_i <= c_i), 1.0, 0.0).astype(BF16)
    u_tot = jnp.where(same, 1.0, 0.0).astype(BF16)
    r_p = lax.broadcasted_iota(I32, (pr, pr), 0)
    c_p = lax.broadcasted_iota(I32, (pr, pr), 1)
    same_p = (r_p >> 6) == (c_p >> 6)
    le = jnp.logical_and(same_p, c_p <= r_p)
    strict = jnp.logical_and(same_p, c_p < r_p)
    eye = jnp.where(r_p == c_p, 1.0, 0.0).astype(F32)

    gt = gates_ref[...].T
    zt = gt + dtb_ref[...]
    g_rows = -jnp.exp(alog_ref[...]) * (jnp.maximum(zt, 0.0) + jnp.log(1.0 + jnp.exp(-jnp.abs(zt))))
    gh, gm, gl = _split3(g_rows)
    gcum_rows = _dot(gh, u_cum) + _dot(gm, u_cum) + _dot(gl, u_cum)
    gtot_rows = _dot(gh, u_tot) + _dot(gm, u_tot) + _dot(gl, u_tot)
    gcum_cols = gcum_rows.T
    gtot_cols = gtot_rows.T
    beta_cols = _sigmoid(gates_ref[...])
    lane = lax.broadcasted_iota(I32, (1, LANES), 1)
    sub = lax.broadcasted_iota(I32, (LANES, 1), 0)
    csel = lax.broadcasted_iota(I32, (1, pr), 1) >> 6

    def conv_silu(c0):
        cols = pl.ds(pl.multiple_of(c0, dk), dk)
        y = (cw_ref[0:1, cols] * ext_ref[pl.ds(5, ts), cols] + cw_ref[1:2, cols] * ext_ref[pl.ds(6, ts), cols]
             + cw_ref[2:3, cols] * ext_ref[pl.ds(7, ts), cols] + cw_ref[3:4, cols] * ext_ref[pl.ds(8, ts), cols])
        return y * _sigmoid(y)

    def head_group(hg, carry):
        heads = [hg * C_GROUP + i for i in range(C_GROUP)]
        hv = []
        for h in heads:
            c0 = pl.multiple_of(h * dk, dk)
            pick_a = lane == (C_HEADS + h)
            gcol = jnp.sum(jnp.where(pick_a, gcum_cols, 0.0), axis=1, keepdims=True)
            glcol = jnp.sum(jnp.where(pick_a, gtot_cols, 0.0), axis=1, keepdims=True)
            bcol = jnp.sum(jnp.where(lane == h, beta_cols, 0.0), axis=1, keepdims=True)
            pick_r = sub == (C_HEADS + h)
            grow = jnp.sum(jnp.where(pick_r, gcum_rows, 0.0), axis=0, keepdims=True)
            glrow = jnp.sum(jnp.where(pick_r, gtot_rows, 0.0), axis=0, keepdims=True)
            q = conv_silu(c0)
            k = conv_silu(c0 + hd)
            v = conv_silu(c0 + 2 * hd)
            q = q * lax.rsqrt(jnp.sum(q * q, axis=-1, keepdims=True) + RMS_EPS) * (dk ** -0.5)
            k = k * lax.rsqrt(jnp.sum(k * k, axis=-1, keepdims=True) + RMS_EPS)
            eg = jnp.exp(gcol)
            kb = k * bcol
            hv.append(dict(
                c0=c0, gcol=gcol, grow=grow, glrow=glrow, kbf=k.astype(BF16), kbb=kb.astype(BF16),
                qbf=q.astype(BF16), rhs=jnp.concatenate([v * bcol, kb * eg], axis=1).astype(BF16),
                qe=q * eg, ktil_t=(k * jnp.exp(glcol - gcol)).T.astype(BF16)))

        inst = [(i, p) for i in range(C_GROUP) for p in range(ts // pr)]
        mpow, tinv, intra = {}, {}, {}
        for (i, p) in inst:
            d_ = hv[i]
            rp = slice(p * pr, (p + 1) * pr)
            decay = jnp.exp(jnp.where(le, d_["gcol"][rp] - d_["grow"][:, rp], NEG))
            lower = jnp.where(strict, _dot_nt(d_["kbb"][rp], d_["kbf"][rp]) * decay, 0.0)
            intra[i, p] = (_dot_nt(d_["qbf"][rp], d_["kbf"][rp]) * decay).astype(BF16)
            mpow[i, p] = -lower
            tinv[i, p] = eye - lower
        for _ in range(5):
            for key in inst:
                mb = mpow[key].astype(BF16)
                mpow[key] = _dot(mb, mb)
            for key in inst:
                tinv[key] = tinv[key] + _dot(tinv[key].astype(BF16), mpow[key].astype(BF16))
        wv_, kc = {}, {}
        for (i, p) in inst:
            wk = _dot(tinv[i, p].astype(BF16), hv[i]["rhs"][p * pr:(p + 1) * pr])
            wv_[i, p] = wk[:, :dk]
            kc[i, p] = wk[:, dk:]

        st = []
        for i, h in enumerate(heads):
            vnew_ref[i] = jnp.zeros(vnew_ref.shape[1:], vnew_ref.dtype)
            st.append(state_ref[h])
        outs = [[] for _ in heads]
        for j in range(nch):
            p, jj = j // 2, j % 2
            rows = slice(j * C_CHUNK, (j + 1) * C_CHUNK)
            lrows = slice(jj * C_CHUNK, (jj + 1) * C_CHUNK)
            rp = slice(p * pr, (p + 1) * pr)
            a1 = [_dot(jnp.concatenate([kc[i, p][lrows], hv[i]["qe"][rows]], axis=0).astype(BF16),
                       st[i].astype(BF16)) for i in range(C_GROUP)]
            for i in range(C_GROUP):
                vnew_ref[i, rows, :] = (wv_[i, p][lrows] - a1[i][:C_CHUNK]).astype(BF16)
            for i in range(C_GROUP):
                vn_pair = vnew_ref[i, rp, :]
                outs[i].append(a1[i][C_CHUNK:] + _dot(intra[i, p][lrows, :], vn_pair))
                kt_j = jnp.where(csel == jj, hv[i]["ktil_t"][:, rp], jnp.zeros((), BF16))
                dg = jnp.exp(jnp.sum(jnp.where(csel == jj, hv[i]["glrow"][:, rp], 0.0), axis=1, keepdims=True)
                             * (1.0 / C_CHUNK))
                st[i] = st[i] * dg + _dot(kt_j, vn_pair)
        for i, h in enumerate(heads):
            state_ref[h] = st[i]
            o = jnp.concatenate(outs[i], axis=0)
            o = o * lax.rsqrt(jnp.mean(o * o, axis=-1, keepdims=True) + RMS_EPS) * nw_ref[...]
            c0 = hv[i]["c0"]
            z = qkvz_ref[:, pl.ds(pl.multiple_of(c0 + 3 * hd, dk), dk)].astype(F32)
            o_ref[:, pl.ds(c0, dk)] = (o * (z * _sigmoid(z))).astype(o_ref.dtype)
        return carry

    lax.fori_loop(0, C_HEADS // C_GROUP, head_group, 0)
    ext_ref[0:8, :] = ext_ref[ts:ts + 8, :]


def _mixer_c(x2, w_in, conv_w, a_log, dt_bias, norm_w, w_out, g, b, bsz, seq):
    t, d = x2.shape
    hd = C_HEADS * C_HEAD_DIM
    qkvz = _proj(x2, w_in[:, :4 * hd].astype(BF16), BF16)
    wg = jnp.zeros((d, LANES), F32).at[:, :2 * C_HEADS].set(w_in[:, 4 * hd:])
    gates = _proj3(x2, wg)
    col = lambda v: jnp.zeros((LANES, 1), F32).at[C_HEADS:2 * C_HEADS, 0].set(v.astype(F32))
    nsteps = seq // C_SUPER
    o = pl.pallas_call(
        _delta_body,
        grid=(bsz, nsteps),
        in_specs=[pl.BlockSpec((C_SUPER, 4 * hd), lambda bi, i: (bi * nsteps + i, 0)),
                  pl.BlockSpec((C_SUPER, LANES), lambda bi, i: (bi * nsteps + i, 0)),
                  _const_spec((C_CONV, 3 * hd)), _const_spec((LANES, 1)), _const_spec((LANES, 1)),
                  _const_spec((1, C_HEAD_DIM))],
        out_specs=pl.BlockSpec((C_SUPER, hd), lambda bi, i: (bi * nsteps + i, 0)),
        out_shape=jax.ShapeDtypeStruct((t, hd), BF16),
        scratch_shapes=[pltpu.VMEM((C_SUPER + 8, 3 * hd), F32),
                        pltpu.VMEM((C_HEADS, C_HEAD_DIM, C_HEAD_DIM), F32),
                        pltpu.VMEM((C_GROUP, C_SUPER, C_HEAD_DIM), BF16)],
        compiler_params=_cparams(2),
        name="deltanet",
    )(qkvz, gates, conv_w.astype(F32), col(a_log), col(dt_bias), norm_w.reshape(1, -1).astype(F32))
    return _mm_res_ln(o, w_out.astype(BF16), x2, g, b)


def _route_rows(x, wh_ref, wl_ref, o_ref, g1_ref, g2_ref, cnt_ref, run_ref, tri_ref):
    tm = x.shape[0]

    @pl.when(pl.program_id(0) == 0)
    def _():
        run_ref[...] = jnp.zeros_like(run_ref)
        r_i = lax.broadcasted_iota(I32, (tm, tm), 0)
        c_i = lax.broadcasted_iota(I32, (tm, tm), 1)
        tri_ref[...] = jnp.where(c_i < r_i, 1.0, 0.0).astype(BF16)

    xh = x.astype(BF16)
    xl = (x - xh.astype(F32)).astype(BF16)
    wh = wh_ref[...]
    logits = _dot(xh, wh) + _dot(xl, wh) + _dot(xh, wl_ref[...])
    lane = lax.broadcasted_iota(I32, (1, LANES), 1)
    lane_f = lane.astype(F32)

    def top1(vals):
        m = jnp.max(vals, axis=-1, keepdims=True)
        idx = jnp.min(jnp.where(vals == m, lane_f, 1e9), axis=-1, keepdims=True)
        return m, idx.astype(I32)

    lc = jnp.where(lane < MOE_GROUPS, logits, NEG)
    mc, grp = top1(lc)
    p_grp = 1.0 / jnp.sum(jnp.exp(lc - mc), axis=-1, keepdims=True)
    lo = MOE_GROUPS + MOE_PER_GROUP * grp
    lf = jnp.where(jnp.logical_and(lane >= lo, lane < lo + MOE_PER_GROUP), logits, NEG)
    m1, i1 = top1(lf)
    m2, i2 = top1(jnp.where(lane == i1, NEG, lf))
    e21 = jnp.exp(m2 - m1)
    g1 = p_grp / (1.0 + e21)
    g2 = p_grp * e21 / (1.0 + e21)
    ex1 = i1 - MOE_GROUPS
    ex2 = i2 - MOE_GROUPS

    oh1 = jnp.where(lane == ex1, 1.0, 0.0).astype(F32)
    oh2 = jnp.where(lane == ex2, 1.0, 0.0).astype(F32)
    ohs = oh1 + oh2
    before = _dot(tri_ref[...], ohs.astype(BF16)) + run_ref[...]
    rank1 = jnp.sum(oh1 * before, axis=-1, keepdims=True)
    rank2 = jnp.sum(oh2 * before, axis=-1, keepdims=True)
    run_ref[...] = run_ref[...] + jnp.sum(ohs, axis=0, keepdims=True)

    out = jnp.where(lane == 0, ex1.astype(F32), 0.0)
    out = jnp.where(lane == 1, ex2.astype(F32), out)
    out = jnp.where(lane == 2, g1, out)
    out = jnp.where(lane == 3, g2, out)
    out = jnp.where(lane == 4, rank1, out)
    out = jnp.where(lane == 5, rank2, out)
    o_ref[...] = out
    g1_ref[...] = jnp.broadcast_to(g1, g1_ref.shape)
    g2_ref[...] = jnp.broadcast_to(g2, g2_ref.shape)
    cnt_ref[...] = jnp.broadcast_to(run_ref[...], cnt_ref.shape)


def _route_weights(w_coarse, w_fine):
    d = w_coarse.shape[0]
    wr = jnp.zeros((d, LANES), F32)
    wr = wr.at[:, :MOE_GROUPS].set(w_coarse).at[:, MOE_GROUPS:MOE_GROUPS + MOE_EXPERTS].set(
        w_fine.reshape(d, MOE_EXPERTS))
    wh = wr.astype(BF16)
    wl = (wr - wh.astype(F32)).astype(BF16)
    return wh, wl


def _route_body(x_ref, wh_ref, wl_ref, o_ref, g1_ref, g2_ref, cnt_ref, run_ref, tri_ref):
    _route_rows(x_ref[...], wh_ref, wl_ref, o_ref, g1_ref, g2_ref, cnt_ref, run_ref, tri_ref)


def _route(x2, w_coarse, w_fine, tm=512):
    t, d = x2.shape
    wh, wl = _route_weights(w_coarse, w_fine)
    row = pl.BlockSpec((tm, LANES), lambda i: (i, 0))
    return pl.pallas_call(
        _route_body,
        grid=(t // tm,),
        in_specs=[pl.BlockSpec((tm, d), lambda i: (i, 0)), _const_spec((d, LANES)), _const_spec((d, LANES))],
        out_specs=[row, row, row, pl.BlockSpec((8, LANES), lambda i: (0, 0))],
        out_shape=[jax.ShapeDtypeStruct((t, LANES), F32), jax.ShapeDtypeStruct((t, LANES), F32),
                   jax.ShapeDtypeStruct((t, LANES), F32), jax.ShapeDtypeStruct((8, LANES), F32)],
        scratch_shapes=[pltpu.VMEM((1, LANES), F32), pltpu.VMEM((tm, tm), BF16)],
        compiler_params=_cparams(1),
        name="moe_route",
    )(x2, wh, wl)


def _dispatch_body(p1_ref, p2_ref, x_ref, o_hbm, slab_ref, sem, *, tm):
    s = pl.program_id(0)
    i = pl.program_id(1)

    @pl.when(jnp.logical_and(s == 0, i == 0))
    def _():
        slab_ref[...] = jnp.zeros_like(slab_ref)

    def row(r, carry):
        v = x_ref[pl.ds(r, 1), :]
        slab_ref[pl.ds(p1_ref[0, r], 1), :] = v
        slab_ref[pl.ds(p2_ref[0, r], 1), :] = v
        return carry
    lax.fori_loop(0, tm, row, 0, unroll=ROW_UNROLL)

    @pl.when(i == pl.num_programs(1) - 1)
    def _():
        cp = pltpu.make_async_copy(slab_ref, o_hbm.at[s], sem)
        cp.start()
        cp.wait()


def _dispatch(xp, pos1, pos2, s_pad, tm=1024):
    t, w = xp.shape
    nslab = w // LANES
    nt = t // tm
    return pl.pallas_call(
        functools.partial(_dispatch_body, tm=tm),
        grid=(nslab, nt),
        in_specs=[pl.BlockSpec((None, 1, tm), lambda s, i: (i, 0, 0), memory_space=pltpu.SMEM),
                  pl.BlockSpec((None, 1, tm), lambda s, i: (i, 0, 0), memory_space=pltpu.SMEM),
                  pl.BlockSpec((tm, LANES), lambda s, i: (i, s))],
        out_specs=pl.BlockSpec(memory_space=pl.ANY),
        out_shape=jax.ShapeDtypeStruct((nslab, s_pad, LANES), U32),
        scratch_shapes=[pltpu.VMEM((s_pad, LANES), U32), pltpu.SemaphoreType.DMA(())],
        compiler_params=_cparams(2),
        name="moe_dispatch",
    )(pos1.reshape(nt, 1, tm), pos2.reshape(nt, 1, tm), xp)


def _expert_body(te_ref, nv_ref, first_ref, nxt_ref, slot_ref, xs_ref, wg_hbm, wu_hbm, wd_hbm, y_ref,
                 wgb_ref, wub_ref, wdb_ref, wg_buf, wu_buf, wd_buf, sem, *, layer):
    i = pl.program_id(0)
    half = D_MODEL // 2
    nslab = half // LANES

    def copies(e, sl):
        return (pltpu.make_async_copy(wg_hbm.at[layer, e], wg_buf.at[sl], sem.at[sl, 0]),
                pltpu.make_async_copy(wu_hbm.at[layer, e], wu_buf.at[sl], sem.at[sl, 1]),
                pltpu.make_async_copy(wd_hbm.at[layer, e], wd_buf.at[sl], sem.at[sl, 2]))

    @pl.when(i < nv_ref[0])
    def _():
        @pl.when(first_ref[i] == 1)
        def _():
            sl = slot_ref[i]

            @pl.when(i == 0)
            def _():
                for cp in copies(te_ref[0], sl):
                    cp.start()

            for cp in copies(te_ref[i], sl):
                cp.wait()

            @pl.when(nxt_ref[i] >= 0)
            def _():
                for cp in copies(nxt_ref[i], 1 - sl):
                    cp.start()

            wgb_ref[...] = wg_buf[sl].astype(BF16)
            wub_ref[...] = wu_buf[sl].astype(BF16)
            wdb_ref[...] = wd_buf[sl].astype(BF16)

        tm = xs_ref.shape[1]
        parts = [slice(a * (tm // MOE_PARTS), (a + 1) * (tm // MOE_PARTS)) for a in range(MOE_PARTS)]
        xlo = [jnp.concatenate([_unpack_lo(xs_ref[s, rs, :]) for s in range(nslab)], axis=1).astype(BF16)
               for rs in parts]
        xhi = [jnp.concatenate([_unpack_hi(xs_ref[s, rs, :]) for s in range(nslab)], axis=1).astype(BF16)
               for rs in parts]
        hg = [_dot(xlo[a], wgb_ref[0:half, :]) + _dot(xhi[a], wgb_ref[half:, :]) for a in range(MOE_PARTS)]
        hu = [_dot(xlo[a], wub_ref[0:half, :]) + _dot(xhi[a], wub_ref[half:, :]) for a in range(MOE_PARTS)]
        h = [(hg[a] * _sigmoid(hg[a]) * hu[a]).astype(BF16) for a in range(MOE_PARTS)]
        yd = [_dot(h[a], wdb_ref[...]) for a in range(MOE_PARTS)]
        for a, rs in enumerate(parts):
            packed = _pack_pairs(yd[a])
            for s in range(nslab):
                y_ref[s, rs, :] = packed[:, s * LANES:(s + 1) * LANES]

    @pl.when(i >= nv_ref[0])
    def _():
        y_ref[...] = jnp.zeros_like(y_ref)


def _experts(xs, tile_expert, nvalid, first, nxt, slot, w_gate, w_up, w_down, layer, n_tiles):
    tm = MOE_TM
    d, hdn = D_MODEL, MOE_HIDDEN
    nslab = d // 2 // LANES
    any_spec = pl.BlockSpec(memory_space=pl.ANY)
    return pl.pallas_call(
        functools.partial(_expert_body, layer=layer),
        grid_spec=pltpu.PrefetchScalarGridSpec(
            num_scalar_prefetch=5,
            grid=(n_tiles,),
            in_specs=[pl.BlockSpec((nslab, tm, LANES), lambda i, te, nv, *_: (0, jnp.minimum(i, nv[0] - 1), 0)),
                      any_spec, any_spec, any_spec],
            out_specs=pl.BlockSpec((nslab, tm, LANES), lambda i, *_: (0, i, 0)),
            scratch_shapes=[pltpu.VMEM((d, hdn), BF16), pltpu.VMEM((d, hdn), BF16), pltpu.VMEM((hdn, d), BF16),
                            pltpu.VMEM((2, d, hdn), F32), pltpu.VMEM((2, d, hdn), F32),
                            pltpu.VMEM((2, hdn, d), F32), pltpu.SemaphoreType.DMA((2, 3))],
        ),
        out_shape=jax.ShapeDtypeStruct((nslab, n_tiles * tm, LANES), U32),
        compiler_params=_cparams(1),
        name="moe_experts",
    )(tile_expert, nvalid, first, nxt, slot, xs, w_gate, w_up, w_down)


def _combine_body(p1_ref, p2_ref, y_ref, g1_ref, g2_ref, lo_ref, hi_ref, b1_ref, b2_ref, *, tm):
    def row(r, carry):
        b1_ref[pl.ds(r, 1), :] = y_ref[pl.ds(p1_ref[0, r], 1), :]
        b2_ref[pl.ds(r, 1), :] = y_ref[pl.ds(p2_ref[0, r], 1), :]
        return carry
    lax.fori_loop(0, tm, row, 0, unroll=ROW_UNROLL)
    g1 = g1_ref[...]
    g2 = g2_ref[...]
    y1 = b1_ref[...]
    y2 = b2_ref[...]
    lo_ref[...] = g1 * _unpack_lo(y1) + g2 * _unpack_lo(y2)
    hi_ref[...] = g1 * _unpack_hi(y1) + g2 * _unpack_hi(y2)


def _combine(y, pos1, pos2, g1b, g2b, tm=1024):
    nslab, s_pad, _ = y.shape
    t = pos1.shape[0]
    nt = t // tm
    return pl.pallas_call(
        functools.partial(_combine_body, tm=tm),
        grid=(nslab, nt),
        in_specs=[pl.BlockSpec((None, 1, tm), lambda s, i: (i, 0, 0), memory_space=pltpu.SMEM),
                  pl.BlockSpec((None, 1, tm), lambda s, i: (i, 0, 0), memory_space=pltpu.SMEM),
                  pl.BlockSpec((None, s_pad, LANES), lambda s, i: (s, 0, 0), pipeline_mode=pl.Buffered(1)),
                  pl.BlockSpec((tm, LANES), lambda s, i: (i, 0)), pl.BlockSpec((tm, LANES), lambda s, i: (i, 0))],
        out_specs=[pl.BlockSpec((tm, LANES), lambda s, i: (i, s)), pl.BlockSpec((tm, LANES), lambda s, i: (i, s))],
        out_shape=[jax.ShapeDtypeStruct((t, nslab * LANES), F32), jax.ShapeDtypeStruct((t, nslab * LANES), F32)],
        scratch_shapes=[pltpu.VMEM((tm, LANES), U32), pltpu.VMEM((tm, LANES), U32)],
        compiler_params=_cparams(2),
        name="moe_combine",
    )(pos1.reshape(nt, 1, tm), pos2.reshape(nt, 1, tm), y, g1b, g2b)


def _res_ln_body(x_ref, lo_ref, hi_ref, g_ref, b_ref, o_ref):
    h = jnp.concatenate([lo_ref[...], hi_ref[...]], axis=1)
    o_ref[...] = _ln_rows(ALPHA * x_ref[...] + h, g_ref[...], b_ref[...])


def _res_ln(x2, lo, hi, g, b, tm=1024):
    t, d = x2.shape
    return pl.pallas_call(
        _res_ln_body,
        grid=(t // tm,),
        in_specs=[pl.BlockSpec((tm, d), lambda i: (i, 0)), pl.BlockSpec((tm, d // 2), lambda i: (i, 0)),
                  pl.BlockSpec((tm, d // 2), lambda i: (i, 0)), _const_spec((1, d)), _const_spec((1, d))],
        out_specs=pl.BlockSpec((tm, d), lambda i: (i, 0)),
        out_shape=jax.ShapeDtypeStruct((t, d), F32),
        compiler_params=_cparams(1),
        name="res_ln",
    )(x2, lo, hi, g.reshape(1, d), b.reshape(1, d))


def _moe(x2, xp, w_coarse, w_fine, w_gate, w_up, w_down, layer, g, b):
    t = x2.shape[0]
    tm = MOE_TM
    n_tiles = (2 * t) // tm + MOE_EXPERTS
    route, g1b, g2b, cnt = _route(x2, w_coarse, w_fine)
    ex = route[:, 0:2].astype(I32)
    rank = route[:, 4:6].astype(I32)
    counts = cnt[0, :MOE_EXPERTS].astype(I32)
    ptiles = (counts + tm - 1) // tm
    tile_end = jnp.cumsum(ptiles)
    pstart = (tile_end - ptiles) * tm
    eids = jnp.arange(MOE_EXPERTS, dtype=I32)
    pos = jnp.sum(jnp.where(ex[:, :, None] == eids, pstart, 0), axis=-1) + rank
    nvalid = tile_end[-1:].astype(I32)
    tidx = jnp.arange(n_tiles, dtype=I32)
    te = jnp.sum((tile_end[None, :] <= jnp.minimum(tidx, nvalid[0] - 1)[:, None]).astype(I32), axis=1)
    nonempty = ptiles > 0
    slot_e = (jnp.cumsum(nonempty.astype(I32)) - 1) & 1
    later = jnp.where(nonempty, eids, MOE_EXPERTS)
    nxt_incl = lax.cummin(later, axis=0, reverse=True)
    nxt_e = jnp.concatenate([nxt_incl[1:], jnp.full((1,), MOE_EXPERTS, I32)])
    nxt_e = jnp.where(nxt_e >= MOE_EXPERTS, -1, nxt_e)
    first = jnp.logical_and(tidx < nvalid[0],
                            jnp.logical_or(tidx == 0, te != jnp.concatenate([te[:1], te[:-1]]))).astype(I32)
    xs = _dispatch(xp, pos[:, 0], pos[:, 1], n_tiles * tm)
    y = _experts(xs, te, nvalid, first, nxt_e[te], slot_e[te], w_gate, w_up, w_down, layer, n_tiles)
    lo, hi = _combine(y, pos[:, 0], pos[:, 1], g1b, g2b)
    return _res_ln(x2, lo, hi, g, b)


def kernel(x, rel_bias, a_w_in, a_w_out, b_w_in, b_norm_g, b_norm_b, b_w_s, b_b_s, b_w_out, c_w_in, c_conv,
           c_a_log, c_dt_bias, c_norm_w, c_w_out, ln_g, ln_b, moe_w_coarse, moe_w_fine, moe_w_gate, moe_w_up,
           moe_w_down):
    bsz, seq, d = x.shape
    x2 = x.reshape(bsz * seq, d)
    for i in range(DEPTH):
        kind, j = i % 3, i // 3
        g1, b1 = ln_g[i, 0], ln_b[i, 0]
        if kind == 0:
            x2, xp = _mixer_a(x2, a_w_in[j], a_w_out[j], rel_bias, g1, b1, bsz, seq)
        elif kind == 1:
            x2, xp = _mixer_b(x2, b_w_in[j], b_norm_g[j], b_norm_b[j], b_w_s[j], b_b_s[j], b_w_out[j], g1, b1)
        else:
            x2, xp = _mixer_c(x2, c_w_in[j], c_conv[j], c_a_log[j], c_dt_bias[j], c_norm_w[j], c_w_out[j],
                              g1, b1, bsz, seq)
        x2 = _moe(x2, xp, moe_w_coarse[i], moe_w_fine[i], moe_w_gate, moe_w_up, moe_w_down, i,
                  ln_g[i, 1], ln_b[i, 1])
    return x2.reshape(bsz, seq, d)
```

```python
import functools
import math

import numpy as np
import jax
import jax.numpy as jnp
from jax import lax
from jax.experimental import pallas as pl
from jax.experimental.pallas import tpu as pltpu

F32 = jnp.float32
BF16 = jnp.bfloat16
U32 = jnp.uint32
I32 = jnp.int32

D_MODEL = 1024
DEPTH = 4
A_GROUPS = ((128, 1), (512, 4), (2048, 16))
A_HEADS = 16
A_HEAD_DIM = 64
A_BLOCK = 128
NUM_BUCKETS = 32
MAX_DISTANCE = 2048
B_CHUNK = 128
B_WIDTH = 2 * D_MODEL
B_GROUPS = 16
C_HEADS = 8
C_HEAD_DIM = 128
C_CONV = 4
C_CHUNK = 64
MOE_GROUPS = 8
MOE_PER_GROUP = 8
MOE_EXPERTS = 64
MOE_HIDDEN = 512
LN_EPS = 1e-5
RMS_EPS = 1e-6
ALPHA = (2 * DEPTH) ** 0.25

LANES = 128
NEG = -1e30
LOG2E = 1.4426950408889634
LN2 = 0.6931471805599453
VMEM_LIMIT = 56 * 1024 * 1024
MOE_TM = 512
ROW_UNROLL = 32
MOE_PARTS = 2
C_SUPER = 256
C_GROUP = 8
A_HP_GROUP = 2


def _cparams(n_axes, vmem=VMEM_LIMIT):
    return pltpu.CompilerParams(dimension_semantics=("arbitrary",) * n_axes, vmem_limit_bytes=vmem)


def _const_spec(shape):
    nd = len(shape)
    return pl.BlockSpec(shape, lambda *_: (0,) * nd, pipeline_mode=pl.Buffered(1))


def _ln_rows(y, g, b):
    mu = jnp.mean(y, axis=-1, keepdims=True)
    yc = y - mu
    var = jnp.mean(yc * yc, axis=-1, keepdims=True)
    return yc * lax.rsqrt(var + LN_EPS) * g + b


def _pack_pairs(y):
    w = y.shape[1] // 2
    lo = lax.bitcast_convert_type(y[:, :w].astype(BF16).astype(F32), U32)
    hi = lax.bitcast_convert_type(y[:, w:].astype(BF16).astype(F32), U32)
    return (lo >> 16) | (hi & jnp.uint32(0xFFFF0000))


def _unpack_lo(p):
    return lax.bitcast_convert_type(p << 16, F32)


def _unpack_hi(p):
    return lax.bitcast_convert_type(p & jnp.uint32(0xFFFF0000), F32)


def _split3(a):
    h = a.astype(BF16)
    r = a - h.astype(F32)
    m = r.astype(BF16)
    l = (r - m.astype(F32)).astype(BF16)
    return h, m, l


def _dot(a, b):
    return jnp.dot(a, b, preferred_element_type=F32)


def _dot_nt(a, b):
    return lax.dot_general(a, b, (((1,), (1,)), ((), ())), preferred_element_type=F32)


def _gelu_tanh(x):
    return 0.5 * x * (1.0 + jnp.tanh(0.7978845608028654 * (x + 0.044715 * (x * x * x))))


def _sigmoid(x):
    return 0.5 * jnp.tanh(0.5 * x) + 0.5


def _proj_body(x_ref, w_ref, o_ref):
    xb = x_ref[...].astype(BF16)
    cw = 1024
    for c in range(w_ref.shape[1] // cw):
        o_ref[:, c * cw:(c + 1) * cw] = _dot(xb, w_ref[:, c * cw:(c + 1) * cw]).astype(o_ref.dtype)


def _proj(x2, w, out_dtype, tm=1024):
    t, k = x2.shape
    n = w.shape[1]
    return pl.pallas_call(
        _proj_body,
        grid=(t // tm,),
        in_specs=[pl.BlockSpec((tm, k), lambda i: (i, 0)), _const_spec((k, n))],
        out_specs=pl.BlockSpec((tm, n), lambda i: (i, 0)),
        out_shape=jax.ShapeDtypeStruct((t, n), out_dtype),
        compiler_params=_cparams(1),
        name="proj",
    )(x2, w)


def _proj3_body(x_ref, wh_ref, wl_ref, o_ref):
    x = x_ref[...]
    xh = x.astype(BF16)
    xl = (x - xh.astype(F32)).astype(BF16)
    wh = wh_ref[...]
    o_ref[...] = _dot(xh, wh) + _dot(xl, wh) + _dot(xh, wl_ref[...])


def _proj3(x2, w, tm=1024):
    t, k = x2.shape
    n = w.shape[1]
    wh = w.astype(BF16)
    wl = (w - wh.astype(F32)).astype(BF16)
    return pl.pallas_call(
        _proj3_body,
        grid=(t // tm,),
        in_specs=[pl.BlockSpec((tm, k), lambda i: (i, 0)), _const_spec((k, n)), _const_spec((k, n))],
        out_specs=pl.BlockSpec((tm, n), lambda i: (i, 0)),
        out_shape=jax.ShapeDtypeStruct((t, n), F32),
        compiler_params=_cparams(1),
        name="proj3",
    )(x2, wh, wl)


def _mm_res_ln_body(a_ref, w_ref, x_ref, g_ref, b_ref, o_ref, p_ref):
    tm = a_ref.shape[0]
    halves = [slice(0, tm // 2), slice(tm // 2, tm)]
    ys = [_dot(a_ref[rs, :], w_ref[...]) for rs in halves]
    for rs, y in zip(halves, ys):
        xn = _ln_rows(ALPHA * x_ref[rs, :] + y, g_ref[...], b_ref[...])
        o_ref[rs, :] = xn
        p_ref[rs, :] = _pack_pairs(xn)


def _mm_res_ln(a, w, x2, g, b, tm=512):
    t, k = a.shape
    d = w.shape[1]
    return pl.pallas_call(
        _mm_res_ln_body,
        grid=(t // tm,),
        in_specs=[pl.BlockSpec((tm, k), lambda i: (i, 0)), _const_spec((k, d)),
                  pl.BlockSpec((tm, d), lambda i: (i, 0)), _const_spec((1, d)), _const_spec((1, d))],
        out_specs=[pl.BlockSpec((tm, d), lambda i: (i, 0)), pl.BlockSpec((tm, d // 2), lambda i: (i, 0))],
        out_shape=[jax.ShapeDtypeStruct((t, d), F32), jax.ShapeDtypeStruct((t, d // 2), U32)],
        compiler_params=_cparams(1),
        name="mm_res_ln",
    )(a, w, x2, g.reshape(1, d), b.reshape(1, d))


def _t5_bucket(dist):
    max_exact = NUM_BUCKETS // 2
    d = jnp.maximum(dist, 1).astype(F32)
    large = max_exact + (jnp.log(d / max_exact) / math.log(MAX_DISTANCE / max_exact)
                         * (NUM_BUCKETS - max_exact)).astype(I32)
    return jnp.where(dist < max_exact, dist, jnp.minimum(large, NUM_BUCKETS - 1))


def _attn_bias(rel_bias, window, dil):
    steps = window // dil
    qi = jnp.arange(A_BLOCK)[:, None]
    ki = jnp.arange(2 * A_BLOCK)[None, :]
    rel = qi + A_BLOCK - ki
    valid = (rel >= 0) & (rel <= steps)
    bucket = _t5_bucket(jnp.maximum(rel, 0) * dil)
    bias = jnp.zeros((A_HEADS, A_BLOCK, 2 * A_BLOCK), F32)
    for bkt in range(NUM_BUCKETS):
        bias = jnp.where((bucket == bkt)[None], rel_bias[bkt].astype(F32)[:, None, None], bias)
    return jnp.where(valid[None], bias * LOG2E, NEG)


def _attn_body(q_ref, k_ref, v_ref, kp_ref, vp_ref, bias_ref, o_ref, lse_ref, *, nres, nblk):
    li = pl.program_id(2)
    lane = lax.broadcasted_iota(I32, (1, LANES), 1)
    col2 = lax.broadcasted_iota(I32, (1, 2 * A_BLOCK), 1)
    first_pen = jnp.where(col2 < A_BLOCK, jnp.where(li == 0, NEG, 0.0).astype(F32), 0.0)
    head0 = lane < A_HEAD_DIM
    lse_ref[...] = jnp.zeros_like(lse_ref)

    def hp_stages(hp):
        c0 = pl.multiple_of(hp * LANES, LANES)
        cols = pl.ds(c0, LANES)
        blocks = [(r, j) for r in range(nres) for j in range(nblk)]
        bias2 = jnp.concatenate([bias_ref[2 * hp], bias_ref[2 * hp + 1]], axis=0)
        ones = jnp.ones((2 * A_BLOCK, LANES), BF16)
        vbs, s_, p_, m_ = {}, {}, {}, {}

        def scores(u):
            r, j = u
            rows = pl.ds(j * A_BLOCK, A_BLOCK)
            q = q_ref[r, rows, cols]
            if j == 0:
                kb = jnp.concatenate([kp_ref[r, :, cols], k_ref[r, rows, cols]], axis=0)
                vbs[u] = jnp.concatenate([vp_ref[r, :, cols], v_ref[r, rows, cols]], axis=0)
            else:
                band = pl.ds((j - 1) * A_BLOCK, 2 * A_BLOCK)
                kb = k_ref[r, band, cols]
                vbs[u] = v_ref[r, band, cols]
            zero = jnp.zeros_like(q)
            q2 = jnp.concatenate([jnp.where(head0, q, zero), jnp.where(head0, zero, q)], axis=0)
            s = _dot_nt(q2, kb) + bias2
            s_[u] = s + first_pen if j == 0 else s

        def softmax(u):
            m_[u] = jnp.max(s_[u], axis=-1, keepdims=True)
            p_[u] = jnp.exp2(s_[u] - m_[u]).astype(BF16)

        def values(u):
            r, j = u
            rows = pl.ds(j * A_BLOCK, A_BLOCK)
            pvl = _dot(p_[u], jnp.concatenate([vbs[u], ones], axis=1))
            den = pvl[:, LANES:]
            o = pvl[:, :LANES] / den
            lse = m_[u] * LN2 + jnp.log(den)
            o_ref[r, rows, cols] = jnp.where(head0, o[:A_BLOCK], o[A_BLOCK:]).astype(o_ref.dtype)
            cur = lse_ref[r, rows, :]
            cur = jnp.where(lane == 2 * hp, lse[:A_BLOCK], cur)
            cur = jnp.where(lane == 2 * hp + 1, lse[A_BLOCK:], cur)
            lse_ref[r, rows, :] = cur

        return blocks, scores, softmax, values

    def hp_pair(hh, carry):
        stages = [hp_stages(A_HP_GROUP * hh + i) for i in range(A_HP_GROUP)]
        for k in range(3):
            for blocks, *fns in stages:
                for u in blocks:
                    fns[k](u)
        return carry

    lax.fori_loop(0, A_HEADS // 2 // A_HP_GROUP, hp_pair, 0)


def _proj_perm_body(x_ref, w_ref, o_ref, xb_ref, y_ref, *, dil):
    tm, k = x_ref.shape
    n = tm // dil

    @pl.when(pl.program_id(1) == 0)
    def _():
        if dil == 1:
            xb_ref[...] = x_ref[...].astype(BF16)
        else:
            nc = k // LANES
            for c in range(nc):
                y_ref[c] = x_ref[:, c * LANES:(c + 1) * LANES]
            for r in range(dil):
                xb_ref[r * n:(r + 1) * n, :] = jnp.concatenate(
                    [y_ref[c, pl.ds(r, n, stride=dil), :] for c in range(nc)], axis=1).astype(BF16)

    cw = 1024
    for c in range(w_ref.shape[1] // cw):
        y = _dot(xb_ref[...], w_ref[:, c * cw:(c + 1) * cw])
        for r in range(dil):
            o_ref[r, :, c * cw:(c + 1) * cw] = y[r * n:(r + 1) * n].astype(o_ref.dtype)


def _proj_perm(x2, w, g, dil, bsz, seq, tm=1024, tn=3072):
    t, k = x2.shape
    hd = A_HEADS * A_HEAD_DIM
    tpb = seq // tm
    return pl.pallas_call(
        functools.partial(_proj_perm_body, dil=dil),
        grid=(t // tm, 3 * hd // tn),
        in_specs=[pl.BlockSpec((tm, k), lambda i, j: (i, 0)),
                  pl.BlockSpec((k, tn), lambda i, j: (0, g * (3 * hd // tn) + j))],
        out_specs=pl.BlockSpec((None, dil, tm // dil, tn), lambda i, j: (i // tpb, 0, i % tpb, j)),
        out_shape=jax.ShapeDtypeStruct((bsz, dil, seq // dil, 3 * hd), BF16),
        scratch_shapes=[pltpu.VMEM((tm, k), BF16), pltpu.VMEM((k // LANES, tm, LANES), F32)],
        compiler_params=_cparams(2),
        name=f"proj_d{dil}",
    )(x2, w)


def _dilated_group(qkv, dil, bias, bsz, seq):
    hd = A_HEADS * A_HEAD_DIM
    L = seq // dil
    rows_per_step = 512
    tl = min(rows_per_step, L)
    nblk = tl // A_BLOCK
    nres = rows_per_step // tl

    def blk(col):
        return pl.BlockSpec((None, nres, tl, hd), lambda b, r, li: (b, r, li, col))

    def prev(col):
        return pl.BlockSpec((None, nres, A_BLOCK, hd),
                            lambda b, r, li: (b, r, jnp.maximum(li * nblk - 1, 0), col))

    return pl.pallas_call(
        functools.partial(_attn_body, nres=nres, nblk=nblk),
        grid=(bsz, dil // nres, L // tl),
        in_specs=[blk(0), blk(1), blk(2), prev(1), prev(2), _const_spec((A_HEADS, A_BLOCK, 2 * A_BLOCK))],
        out_specs=[pl.BlockSpec((None, nres, tl, hd), lambda b, r, li: (b, r, li, 0)),
                   pl.BlockSpec((None, nres, tl, LANES), lambda b, r, li: (b, r, li, 0))],
        out_shape=[jax.ShapeDtypeStruct((bsz, dil, L, hd), BF16),
                   jax.ShapeDtypeStruct((bsz, dil, L, LANES), F32)],
        compiler_params=_cparams(3),
        name=f"dilated_attn_d{dil}",
    )(qkv, qkv, qkv, qkv, qkv, bias)


def _attn_out_body(o1_ref, o2_ref, o3_ref, l1_ref, l2_ref, l3_ref, e_ref, w_ref, x_ref, g_ref, b_ref,
                   xo_ref, p_ref, so_ref, sl2_ref, sl3_ref, *, dils):
    def natural(ref, scr, dil):
        if dil == 1:
            return ref[0].astype(F32)
        n = ref.shape[1]
        nc = ref.shape[2] // LANES
        for r in range(dil):
            blk = ref[r].astype(F32)
            for c in range(nc):
                scr[c, pl.ds(r, n, stride=dil), :] = blk[:, c * LANES:(c + 1) * LANES]
        return jnp.concatenate([scr[c] for c in range(nc)], axis=1) if nc > 1 else scr[0]

    l1 = natural(l1_ref, None, dils[0])
    l2 = natural(l2_ref, sl2_ref, dils[1])
    l3 = natural(l3_ref, sl3_ref, dils[2])
    m = jnp.maximum(jnp.maximum(l1, l2), l3)
    e1, e2, e3 = jnp.exp(l1 - m), jnp.exp(l2 - m), jnp.exp(l3 - m)
    inv = 1.0 / (e1 + e2 + e3)
    ex = e_ref[...]

    def expand(wt):
        h = wt.astype(BF16)
        lo = (wt - h.astype(F32)).astype(BF16)
        return _dot(h, ex) + _dot(lo, ex)

    comb = expand(e1 * inv) * natural(o1_ref, None, dils[0])
    comb = comb + expand(e2 * inv) * natural(o2_ref, so_ref, dils[1])
    comb = comb + expand(e3 * inv) * natural(o3_ref, so_ref, dils[2])
    y = _dot(comb.astype(BF16), w_ref[...])
    xn = _ln_rows(ALPHA * x_ref[...] + y, g_ref[...], b_ref[...])
    xo_ref[...] = xn
    p_ref[...] = _pack_pairs(xn)


def _attn_out(os, lses, dils, w_out, x2, g, b, seq, tm=512):
    t, d = x2.shape
    tpb = seq // tm
    expand = np.zeros((LANES, d), np.float32)
    for h in range(A_HEADS):
        expand[h, h * A_HEAD_DIM:(h + 1) * A_HEAD_DIM] = 1.0
    row = lambda n: pl.BlockSpec((tm, n), lambda i: (i, 0))
    res = lambda dil, n: pl.BlockSpec((None, dil, tm // dil, n), lambda i: (i // tpb, 0, i % tpb, 0))
    return pl.pallas_call(
        functools.partial(_attn_out_body, dils=dils),
        grid=(t // tm,),
        in_specs=[res(dils[0], d), res(dils[1], d), res(dils[2], d),
                  res(dils[0], LANES), res(dils[1], LANES), res(dils[2], LANES),
                  _const_spec((LANES, d)), _const_spec((d, d)), row(d), _const_spec((1, d)), _const_spec((1, d))],
        out_specs=[row(d), row(d // 2)],
        out_shape=[jax.ShapeDtypeStruct((t, d), F32), jax.ShapeDtypeStruct((t, d // 2), U32)],
        scratch_shapes=[pltpu.VMEM((d // LANES, tm, LANES), F32), pltpu.VMEM((1, tm, LANES), F32),
                        pltpu.VMEM((1, tm, LANES), F32)],
        compiler_params=_cparams(1),
        name="attn_out",
    )(*os, *lses, jnp.asarray(expand, BF16), w_out, x2, g.reshape(1, d), b.reshape(1, d))


def _mixer_a(x2, w_in, w_out, rel_bias, g, b, bsz, seq):
    hd = A_HEADS * A_HEAD_DIM
    scale = np.ones((9 * hd,), np.float32)
    for gi in range(len(A_GROUPS)):
        scale[3 * gi * hd:(3 * gi + 1) * hd] = A_HEAD_DIM ** -0.5 * LOG2E
    wb = (w_in * scale).astype(BF16)
    os, lses = [], []
    for gi, (window, dil) in enumerate(A_GROUPS):
        qkv = _proj_perm(x2, wb, gi, dil, bsz, seq)
        o, lse = _dilated_group(qkv, dil, _attn_bias(rel_bias, window, dil), bsz, seq)
        os.append(o)
        lses.append(lse)
    return _attn_out(os, lses, tuple(dl for _, dl in A_GROUPS), w_out.astype(BF16), x2, g, b, seq)


def _sgu_body(x_ref, wu_ref, wv_ref, ng_ref, nb_ref, wc_ref, bs_ref, a_ref, vb_ref, *, tm):
    xb = x_ref[...].astype(BF16)
    v = _gelu_tanh(_dot(xb, wv_ref[...]))
    vb_ref[...] = _ln_rows(v, ng_ref[...], nb_ref[...]).astype(BF16)
    gw = B_WIDTH // B_GROUPS
    ucols = 512
    for j in range(B_WIDTH // ucols):
        u = _gelu_tanh(_dot(xb, wu_ref[:, j * ucols:(j + 1) * ucols]))
        nchunk = tm // B_CHUNK
        for gg in range(ucols // gw):
            gi = j * (ucols // gw) + gg
            cols = slice(gi * gw, (gi + 1) * gw)
            vcat = jnp.concatenate([vb_ref[c * B_CHUNK:(c + 1) * B_CHUNK, cols] for c in range(nchunk)], axis=1)
            fcat = _dot(wc_ref[gi], vcat)
            for c in range(nchunk):
                rows = slice(c * B_CHUNK, (c + 1) * B_CHUNK)
                f = fcat[:, c * gw:(c + 1) * gw] + bs_ref[:, cols]
                a_ref[rows, cols] = (u[rows, gg * gw:(gg + 1) * gw] * f).astype(a_ref.dtype)


def _mixer_b(x2, w_in, norm_g, norm_b, w_s, b_s, w_out, g, b, tm=512):
    t, d = x2.shape
    wu = w_in[:, :B_WIDTH].astype(BF16)
    wv = w_in[:, B_WIDTH:].astype(BF16)
    wc = (w_s * jnp.tril(jnp.ones((B_CHUNK, B_CHUNK), w_s.dtype))).astype(BF16)
    bs_full = jnp.repeat(b_s.T, B_WIDTH // B_GROUPS, axis=1)
    a = pl.pallas_call(
        functools.partial(_sgu_body, tm=tm),
        grid=(t // tm,),
        in_specs=[pl.BlockSpec((tm, d), lambda i: (i, 0)), _const_spec((d, B_WIDTH)), _const_spec((d, B_WIDTH)),
                  _const_spec((1, B_WIDTH)), _const_spec((1, B_WIDTH)),
                  _const_spec((B_GROUPS, B_CHUNK, B_CHUNK)), _const_spec((B_CHUNK, B_WIDTH))],
        out_specs=pl.BlockSpec((tm, B_WIDTH), lambda i: (i, 0)),
        out_shape=jax.ShapeDtypeStruct((t, B_WIDTH), BF16),
        scratch_shapes=[pltpu.VMEM((tm, B_WIDTH), BF16)],
        compiler_params=_cparams(1),
        name="sgu",
    )(x2, wu, wv, norm_g.reshape(1, -1), norm_b.reshape(1, -1), wc, bs_full)
    return _mm_res_ln(a, w_out.astype(BF16), x2, g, b)


def _delta_body(qkvz_ref, gates_ref, cw_ref, alog_ref, dtb_ref, nw_ref, o_ref, ext_ref, state_ref, vnew_ref):
    ts = C_SUPER
    hd = C_HEADS * C_HEAD_DIM
    dk = C_HEAD_DIM
    nch = ts // C_CHUNK
    pr = 2 * C_CHUNK

    @pl.when(pl.program_id(1) == 0)
    def _():
        ext_ref[0:8, :] = jnp.zeros((8, 3 * hd), F32)
        state_ref[...] = jnp.zeros_like(state_ref)

    ext_ref[8:8 + ts, :] = qkvz_ref[:, 0:3 * hd].astype(F32)

    r_i = lax.broadcasted_iota(I32, (ts, ts), 0)
    c_i = lax.broadcasted_iota(I32, (ts, ts), 1)
    same = (r_i >> 6) == (c_i >> 6)
    u_cum = jnp.where(jnp.logical_and(same, r_i <= c_i), 1.0, 0.0).astype(BF16)
    u_tot = jnp.where(same, 1.0, 0.0).astype(BF16)
    r_p = lax.broadcasted_iota(I32, (pr, pr), 0)
    c_p = lax.broadcasted_iota(I32, (pr, pr), 1)
    same_p = (r_p >> 6) == (c_p >> 6)
    le = jnp.logical_and(same_p, c_p <= r_p)
    strict = jnp.logical_and(same_p, c_p < r_p)
    eye = jnp.where(r_p == c_p, 1.0, 0.0).astype(F32)

    gt = gates_ref[...].T
    zt = gt + dtb_ref[...]
    g_rows = -jnp.exp(alog_ref[...]) * (jnp.maximum(zt, 0.0) + jnp.log(1.0 + jnp.exp(-jnp.abs(zt))))
    gh, gm, gl = _split3(g_rows)
    gcum_rows = _dot(gh, u_cum) + _dot(gm, u_cum) + _dot(gl, u_cum)
    gtot_rows = _dot(gh, u_tot) + _dot(gm, u_tot) + _dot(gl, u_tot)
    gcum_cols = gcum_rows.T
    gtot_cols = gtot_rows.T
    beta_cols = _sigmoid(gates_ref[...])
    lane = lax.broadcasted_iota(I32, (1, LANES), 1)
    sub = lax.broadcasted_iota(I32, (LANES, 1), 0)
    csel = lax.broadcasted_iota(I32, (1, pr), 1) >> 6

    def conv_silu(c0):
        cols = pl.ds(pl.multiple_of(c0, dk), dk)
        y = (cw_ref[0:1, cols] * ext_ref[pl.ds(5, ts), cols] + cw_ref[1:2, cols] * ext_ref[pl.ds(6, ts), cols]
             + cw_ref[2:3, cols] * ext_ref[pl.ds(7, ts), cols] + cw_ref[3:4, cols] * ext_ref[pl.ds(8, ts), cols])
        return y * _sigmoid(y)

    def head_group(hg, carry):
        heads = [hg * C_GROUP + i for i in range(C_GROUP)]
        hv = []
        for h in heads:
            c0 = pl.multiple_of(h * dk, dk)
            pick_a = lane == (C_HEADS + h)
            gcol = jnp.sum(jnp.where(pick_a, gcum_cols, 0.0), axis=1, keepdims=True)
            glcol = jnp.sum(jnp.where(pick_a, gtot_cols, 0.0), axis=1, keepdims=True)
            bcol = jnp.sum(jnp.where(lane == h, beta_cols, 0.0), axis=1, keepdims=True)
            pick_r = sub == (C_HEADS + h)
            grow = jnp.sum(jnp.where(pick_r, gcum_rows, 0.0), axis=0, keepdims=True)
            glrow = jnp.sum(jnp.where(pick_r, gtot_rows, 0.0), axis=0, keepdims=True)
            q = conv_silu(c0)
            k = conv_silu(c0 + hd)
            v = conv_silu(c0 + 2 * hd)
            q = q * lax.rsqrt(jnp.sum(q * q, axis=-1, keepdims=True) + RMS_EPS) * (dk ** -0.5)
            k = k * lax.rsqrt(jnp.sum(k * k, axis=-1, keepdims=True) + RMS_EPS)
            eg = jnp.exp(gcol)
            kb = k * bcol
            hv.append(dict(
                c0=c0, gcol=gcol, grow=grow, glrow=glrow, kbf=k.astype(BF16), kbb=kb.astype(BF16),
                qbf=q.astype(BF16), rhs=jnp.concatenate([v * bcol, kb * eg], axis=1).astype(BF16),
                qe=q * eg, ktil_t=(k * jnp.exp(glcol - gcol)).T.astype(BF16)))

        inst = [(i, p) for i in range(C_GROUP) for p in range(ts // pr)]
        mpow, tinv, intra = {}, {}, {}
        for (i, p) in inst:
            d_ = hv[i]
            rp = slice(p * pr, (p + 1) * pr)
            decay = jnp.exp(jnp.where(le, d_["gcol"][rp] - d_["grow"][:, rp], NEG))
            lower = jnp.where(strict, _dot_nt(d_["kbb"][rp], d_["kbf"][rp]) * decay, 0.0)
            intra[i, p] = (_dot_nt(d_["qbf"][rp], d_["kbf"][rp]) * decay).astype(BF16)
            mpow[i, p] = -lower
            tinv[i, p] = eye - lower
        for _ in range(5):
            for key in inst:
                mb = mpow[key].astype(BF16)
                mpow[key] = _dot(mb, mb)
            for key in inst:
                tinv[key] = tinv[key] + _dot(tinv[key].astype(BF16), mpow[key].astype(BF16))
        wv_, kc = {}, {}
        for (i, p) in inst:
            wk = _dot(tinv[i, p].astype(BF16), hv[i]["rhs"][p * pr:(p + 1) * pr])
            wv_[i, p] = wk[:, :dk]
            kc[i, p] = wk[:, dk:]

        st = []
        for i, h in enumerate(heads):
            vnew_ref[i] = jnp.zeros(vnew_ref.shape[1:], vnew_ref.dtype)
            st.append(state_ref[h])
        outs = [[] for _ in heads]
        for j in range(nch):
            p, jj = j // 2, j % 2
            rows = slice(j * C_CHUNK, (j + 1) * C_CHUNK)
            lrows = slice(jj * C_CHUNK, (jj + 1) * C_CHUNK)
            rp = slice(p * pr, (p + 1) * pr)
            a1 = [_dot(jnp.concatenate([kc[i, p][lrows], hv[i]["qe"][rows]], axis=0).astype(BF16),
                       st[i].astype(BF16)) for i in range(C_GROUP)]
            for i in range(C_GROUP):
                vnew_ref[i, rows, :] = (wv_[i, p][lrows] - a1[i][:C_CHUNK]).astype(BF16)
            for i in range(C_GROUP):
                vn_pair = vnew_ref[i, rp, :]
                outs[i].append(a1[i][C_CHUNK:] + _dot(intra[i, p][lrows, :], vn_pair))
                kt_j = jnp.where(csel == jj, hv[i]["ktil_t"][:, rp], jnp.zeros((), BF16))
                dg = jnp.exp(jnp.sum(jnp.where(csel == jj, hv[i]["glrow"][:, rp], 0.0), axis=1, keepdims=True)
                             * (1.0 / C_CHUNK))
                st[i] = st[i] * dg + _dot(kt_j, vn_pair)
        for i, h in enumerate(heads):
            state_ref[h] = st[i]
            o = jnp.concatenate(outs[i], axis=0)
            o = o * lax.rsqrt(jnp.mean(o * o, axis=-1, keepdims=True) + RMS_EPS) * nw_ref[...]
            c0 = hv[i]["c0"]
            z = qkvz_ref[:, pl.ds(pl.multiple_of(c0 + 3 * hd, dk), dk)].astype(F32)
            o_ref[:, pl.ds(c0, dk)] = (o * (z * _sigmoid(z))).astype(o_ref.dtype)
        return carry

    lax.fori_loop(0, C_HEADS // C_GROUP, head_group, 0)
    ext_ref[0:8, :] = ext_ref[ts:ts + 8, :]


def _mixer_c(x2, w_in, conv_w, a_log, dt_bias, norm_w, w_out, g, b, bsz, seq):
    t, d = x2.shape
    hd = C_HEADS * C_HEAD_DIM
    qkvz = _proj(x2, w_in[:, :4 * hd].astype(BF16), BF16)
    wg = jnp.zeros((d, LANES), F32).at[:, :2 * C_HEADS].set(w_in[:, 4 * hd:])
    gates = _proj3(x2, wg)
    col = lambda v: jnp.zeros((LANES, 1), F32).at[C_HEADS:2 * C_HEADS, 0].set(v.astype(F32))
    nsteps = seq // C_SUPER
    o = pl.pallas_call(
        _delta_body,
        grid=(bsz, nsteps),
        in_specs=[pl.BlockSpec((C_SUPER, 4 * hd), lambda bi, i: (bi * nsteps + i, 0)),
                  pl.BlockSpec((C_SUPER, LANES), lambda bi, i: (bi * nsteps + i, 0)),
                  _const_spec((C_CONV, 3 * hd)), _const_spec((LANES, 1)), _const_spec((LANES, 1)),
                  _const_spec((1, C_HEAD_DIM))],
        out_specs=pl.BlockSpec((C_SUPER, hd), lambda bi, i: (bi * nsteps + i, 0)),
        out_shape=jax.ShapeDtypeStruct((t, hd), BF16),
        scratch_shapes=[pltpu.VMEM((C_SUPER + 8, 3 * hd), F32),
                        pltpu.VMEM((C_HEADS, C_HEAD_DIM, C_HEAD_DIM), F32),
                        pltpu.VMEM((C_GROUP, C_SUPER, C_HEAD_DIM), BF16)],
        compiler_params=_cparams(2),
        name="deltanet",
    )(qkvz, gates, conv_w.astype(F32), col(a_log), col(dt_bias), norm_w.reshape(1, -1).astype(F32))
    return _mm_res_ln(o, w_out.astype(BF16), x2, g, b)


def _route_rows(x, wh_ref, wl_ref, o_ref, g1_ref, g2_ref, cnt_ref, run_ref, tri_ref):
    tm = x.shape[0]

    @pl.when(pl.program_id(0) == 0)
    def _():
        run_ref[...] = jnp.zeros_like(run_ref)
        r_i = lax.broadcasted_iota(I32, (tm, tm), 0)
        c_i = lax.broadcasted_iota(I32, (tm, tm), 1)
        tri_ref[...] = jnp.where(c_i < r_i, 1.0, 0.0).astype(BF16)

    xh = x.astype(BF16)
    xl = (x - xh.astype(F32)).astype(BF16)
    wh = wh_ref[...]
    logits = _dot(xh, wh) + _dot(xl, wh) + _dot(xh, wl_ref[...])
    lane = lax.broadcasted_iota(I32, (1, LANES), 1)
    lane_f = lane.astype(F32)

    def top1(vals):
        m = jnp.max(vals, axis=-1, keepdims=True)
        idx = jnp.min(jnp.where(vals == m, lane_f, 1e9), axis=-1, keepdims=True)
        return m, idx.astype(I32)

    lc = jnp.where(lane < MOE_GROUPS, logits, NEG)
    mc, grp = top1(lc)
    p_grp = 1.0 / jnp.sum(jnp.exp(lc - mc), axis=-1, keepdims=True)
    lo = MOE_GROUPS + MOE_PER_GROUP * grp
    lf = jnp.where(jnp.logical_and(lane >= lo, lane < lo + MOE_PER_GROUP), logits, NEG)
    m1, i1 = top1(lf)
    m2, i2 = top1(jnp.where(lane == i1, NEG, lf))
    e21 = jnp.exp(m2 - m1)
    g1 = p_grp / (1.0 + e21)
    g2 = p_grp * e21 / (1.0 + e21)
    ex1 = i1 - MOE_GROUPS
    ex2 = i2 - MOE_GROUPS

    oh1 = jnp.where(lane == ex1, 1.0, 0.0).astype(F32)
    oh2 = jnp.where(lane == ex2, 1.0, 0.0).astype(F32)
    ohs = oh1 + oh2
    before = _dot(tri_ref[...], ohs.astype(BF16)) + run_ref[...]
    rank1 = jnp.sum(oh1 * before, axis=-1, keepdims=True)
    rank2 = jnp.sum(oh2 * before, axis=-1, keepdims=True)
    run_ref[...] = run_ref[...] + jnp.sum(ohs, axis=0, keepdims=True)

    out = jnp.where(lane == 0, ex1.astype(F32), 0.0)
    out = jnp.where(lane == 1, ex2.astype(F32), out)
    out = jnp.where(lane == 2, g1, out)
    out = jnp.where(lane == 3, g2, out)
    out = jnp.where(lane == 4, rank1, out)
    out = jnp.where(lane == 5, rank2, out)
    o_ref[...] = out
    g1_ref[...] = jnp.broadcast_to(g1, g1_ref.shape)
    g2_ref[...] = jnp.broadcast_to(g2, g2_ref.shape)
    cnt_ref[...] = jnp.broadcast_to(run_ref[...], cnt_ref.shape)


def _route_weights(w_coarse, w_fine):
    d = w_coarse.shape[0]
    wr = jnp.zeros((d, LANES), F32)
    wr = wr.at[:, :MOE_GROUPS].set(w_coarse).at[:, MOE_GROUPS:MOE_GROUPS + MOE_EXPERTS].set(
        w_fine.reshape(d, MOE_EXPERTS))
    wh = wr.astype(BF16)
    wl = (wr - wh.astype(F32)).astype(BF16)
    return wh, wl


def _route_body(x_ref, wh_ref, wl_ref, o_ref, g1_ref, g2_ref, cnt_ref, run_ref, tri_ref):
    _route_rows(x_ref[...], wh_ref, wl_ref, o_ref, g1_ref, g2_ref, cnt_ref, run_ref, tri_ref)


def _route(x2, w_coarse, w_fine, tm=512):
    t, d = x2.shape
    wh, wl = _route_weights(w_coarse, w_fine)
    row = pl.BlockSpec((tm, LANES), lambda i: (i, 0))
    return pl.pallas_call(
        _route_body,
        grid=(t // tm,),
        in_specs=[pl.BlockSpec((tm, d), lambda i: (i, 0)), _const_spec((d, LANES)), _const_spec((d, LANES))],
        out_specs=[row, row, row, pl.BlockSpec((8, LANES), lambda i: (0, 0))],
        out_shape=[jax.ShapeDtypeStruct((t, LANES), F32), jax.ShapeDtypeStruct((t, LANES), F32),
                   jax.ShapeDtypeStruct((t, LANES), F32), jax.ShapeDtypeStruct((8, LANES), F32)],
        scratch_shapes=[pltpu.VMEM((1, LANES), F32), pltpu.VMEM((tm, tm), BF16)],
        compiler_params=_cparams(1),
        name="moe_route",
    )(x2, wh, wl)


def _dispatch_body(p1_ref, p2_ref, x_ref, o_hbm, slab_ref, sem, *, tm):
    s = pl.program_id(0)
    i = pl.program_id(1)

    @pl.when(jnp.logical_and(s == 0, i == 0))
    def _():
        slab_ref[...] = jnp.zeros_like(slab_ref)

    def row(r, carry):
        v = x_ref[pl.ds(r, 1), :]
        slab_ref[pl.ds(p1_ref[0, r], 1), :] = v
        slab_ref[pl.ds(p2_ref[0, r], 1), :] = v
        return carry
    lax.fori_loop(0, tm, row, 0, unroll=ROW_UNROLL)

    @pl.when(i == pl.num_programs(1) - 1)
    def _():
        cp = pltpu.make_async_copy(slab_ref, o_hbm.at[s], sem)
        cp.start()
        cp.wait()


def _dispatch(xp, pos1, pos2, s_pad, tm=2048):
    t, w = xp.shape
    nslab = w // LANES
    nt = t // tm
    return pl.pallas_call(
        functools.partial(_dispatch_body, tm=tm),
        grid=(nslab, nt),
        in_specs=[pl.BlockSpec((None, 1, tm), lambda s, i: (i, 0, 0), memory_space=pltpu.SMEM),
                  pl.BlockSpec((None, 1, tm), lambda s, i: (i, 0, 0), memory_space=pltpu.SMEM),
                  pl.BlockSpec((tm, LANES), lambda s, i: (i, s))],
        out_specs=pl.BlockSpec(memory_space=pl.ANY),
        out_shape=jax.ShapeDtypeStruct((nslab, s_pad, LANES), U32),
        scratch_shapes=[pltpu.VMEM((s_pad, LANES), U32), pltpu.SemaphoreType.DMA(())],
        compiler_params=_cparams(2),
        name="moe_dispatch",
    )(pos1.reshape(nt, 1, tm), pos2.reshape(nt, 1, tm), xp)


def _expert_body(te_ref, nv_ref, first_ref, nxt_ref, slot_ref, xs_ref, wg_hbm, wu_hbm, wd_hbm, y_ref,
                 wgb_ref, wub_ref, wdb_ref, wg_buf, wu_buf, wd_buf, sem, *, layer):
    i = pl.program_id(0)
    half = D_MODEL // 2
    nslab = half // LANES

    def copies(e, sl):
        return (pltpu.make_async_copy(wg_hbm.at[layer, e], wg_buf.at[sl], sem.at[sl, 0]),
                pltpu.make_async_copy(wu_hbm.at[layer, e], wu_buf.at[sl], sem.at[sl, 1]),
                pltpu.make_async_copy(wd_hbm.at[layer, e], wd_buf.at[sl], sem.at[sl, 2]))

    @pl.when(i < nv_ref[0])
    def _():
        @pl.when(first_ref[i] == 1)
        def _():
            sl = slot_ref[i]

            @pl.when(i == 0)
            def _():
                for cp in copies(te_ref[0], sl):
                    cp.start()

            for cp in copies(te_ref[i], sl):
                cp.wait()

            @pl.when(nxt_ref[i] >= 0)
            def _():
                for cp in copies(nxt_ref[i], 1 - sl):
                    cp.start()

            wgb_ref[...] = wg_buf[sl].astype(BF16)
            wub_ref[...] = wu_buf[sl].astype(BF16)
            wdb_ref[...] = wd_buf[sl].astype(BF16)

        tm = xs_ref.shape[1]
        parts = [slice(a * (tm // MOE_PARTS), (a + 1) * (tm // MOE_PARTS)) for a in range(MOE_PARTS)]
        xlo = [jnp.concatenate([_unpack_lo(xs_ref[s, rs, :]) for s in range(nslab)], axis=1).astype(BF16)
               for rs in parts]
        xhi = [jnp.concatenate([_unpack_hi(xs_ref[s, rs, :]) for s in range(nslab)], axis=1).astype(BF16)
               for rs in parts]
        hg = [_dot(xlo[a], wgb_ref[0:half, :]) + _dot(xhi[a], wgb_ref[half:, :]) for a in range(MOE_PARTS)]
        hu = [_dot(xlo[a], wub_ref[0:half, :]) + _dot(xhi[a], wub_ref[half:, :]) for a in range(MOE_PARTS)]
        h = [(hg[a] * _sigmoid(hg[a]) * hu[a]).astype(BF16) for a in range(MOE_PARTS)]
        yd = [_dot(h[a], wdb_ref[...]) for a in range(MOE_PARTS)]
        for a, rs in enumerate(parts):
            packed = _pack_pairs(yd[a])
            for s in range(nslab):
                y_ref[s, rs, :] = packed[:, s * LANES:(s + 1) * LANES]

    @pl.when(i >= nv_ref[0])
    def _():
        y_ref[...] = jnp.zeros_like(y_ref)


def _experts(xs, tile_expert, nvalid, first, nxt, slot, w_gate, w_up, w_down, layer, n_tiles):
    tm = MOE_TM
    d, hdn = D_MODEL, MOE_HIDDEN
    nslab = d // 2 // LANES
    any_spec = pl.BlockSpec(memory_space=pl.ANY)
    return pl.pallas_call(
        functools.partial(_expert_body, layer=layer),
        grid_spec=pltpu.PrefetchScalarGridSpec(
            num_scalar_prefetch=5,
            grid=(n_tiles,),
            in_specs=[pl.BlockSpec((nslab, tm, LANES), lambda i, te, nv, *_: (0, jnp.minimum(i, nv[0] - 1), 0)),
                      any_spec, any_spec, any_spec],
            out_specs=pl.BlockSpec((nslab, tm, LANES), lambda i, *_: (0, i, 0)),
            scratch_shapes=[pltpu.VMEM((d, hdn), BF16), pltpu.VMEM((d, hdn), BF16), pltpu.VMEM((hdn, d), BF16),
                            pltpu.VMEM((2, d, hdn), F32), pltpu.VMEM((2, d, hdn), F32),
                            pltpu.VMEM((2, hdn, d), F32), pltpu.SemaphoreType.DMA((2, 3))],
        ),
        out_shape=jax.ShapeDtypeStruct((nslab, n_tiles * tm, LANES), U32),
        compiler_params=_cparams(1),
        name="moe_experts",
    )(tile_expert, nvalid, first, nxt, slot, xs, w_gate, w_up, w_down)


def _combine_body(p1_ref, p2_ref, y_ref, g1_ref, g2_ref, lo_ref, hi_ref, b1_ref, b2_ref, *, tm):
    def row(r, carry):
        b1_ref[pl.ds(r, 1), :] = y_ref[pl.ds(p1_ref[0, r], 1), :]
        b2_ref[pl.ds(r, 1), :] = y_ref[pl.ds(p2_ref[0, r], 1), :]
        return carry
    lax.fori_loop(0, tm, row, 0, unroll=ROW_UNROLL)
    g1 = g1_ref[...]
    g2 = g2_ref[...]
    y1 = b1_ref[...]
    y2 = b2_ref[...]
    lo_ref[...] = g1 * _unpack_lo(y1) + g2 * _unpack_lo(y2)
    hi_ref[...] = g1 * _unpack_hi(y1) + g2 * _unpack_hi(y2)


def _combine(y, pos1, pos2, g1b, g2b, tm=2048):
    nslab, s_pad, _ = y.shape
    t = pos1.shape[0]
    nt = t // tm
    return pl.pallas_call(
        functools.partial(_combine_body, tm=tm),
        grid=(nslab, nt),
        in_specs=[pl.BlockSpec((None, 1, tm), lambda s, i: (i, 0, 0), memory_space=pltpu.SMEM),
                  pl.BlockSpec((None, 1, tm), lambda s, i: (i, 0, 0), memory_space=pltpu.SMEM),
                  pl.BlockSpec((None, s_pad, LANES), lambda s, i: (s, 0, 0), pipeline_mode=pl.Buffered(1)),
                  pl.BlockSpec((tm, LANES), lambda s, i: (i, 0)), pl.BlockSpec((tm, LANES), lambda s, i: (i, 0))],
        out_specs=[pl.BlockSpec((tm, LANES), lambda s, i: (i, s)), pl.BlockSpec((tm, LANES), lambda s, i: (i, s))],
        out_shape=[jax.ShapeDtypeStruct((t, nslab * LANES), F32), jax.ShapeDtypeStruct((t, nslab * LANES), F32)],
        scratch_shapes=[pltpu.VMEM((tm, LANES), U32), pltpu.VMEM((tm, LANES), U32)],
        compiler_params=_cparams(2),
        name="moe_combine",
    )(pos1.reshape(nt, 1, tm), pos2.reshape(nt, 1, tm), y, g1b, g2b)


def _res_ln_body(x_ref, lo_ref, hi_ref, g_ref, b_ref, o_ref):
    h = jnp.concatenate([lo_ref[...], hi_ref[...]], axis=1)
    o_ref[...] = _ln_rows(ALPHA * x_ref[...] + h, g_ref[...], b_ref[...])


def _res_ln(x2, lo, hi, g, b, tm=1024):
    t, d = x2.shape
    return pl.pallas_call(
        _res_ln_body,
        grid=(t // tm,),
        in_specs=[pl.BlockSpec((tm, d), lambda i: (i, 0)), pl.BlockSpec((tm, d // 2), lambda i: (i, 0)),
                  pl.BlockSpec((tm, d // 2), lambda i: (i, 0)), _const_spec((1, d)), _const_spec((1, d))],
        out_specs=pl.BlockSpec((tm, d), lambda i: (i, 0)),
        out_shape=jax.ShapeDtypeStruct((t, d), F32),
        compiler_params=_cparams(1),
        name="res_ln",
    )(x2, lo, hi, g.reshape(1, d), b.reshape(1, d))


def _moe(x2, xp, w_coarse, w_fine, w_gate, w_up, w_down, layer, g, b):
    t = x2.shape[0]
    tm = MOE_TM
    n_tiles = (2 * t) // tm + MOE_EXPERTS
    route, g1b, g2b, cnt = _route(x2, w_coarse, w_fine)
    ex = route[:, 0:2].astype(I32)
    rank = route[:, 4:6].astype(I32)
    counts = cnt[0, :MOE_EXPERTS].astype(I32)
    ptiles = (counts + tm - 1) // tm
    tile_end = jnp.cumsum(ptiles)
    pstart = (tile_end - ptiles) * tm
    eids = jnp.arange(MOE_EXPERTS, dtype=I32)
    pos = jnp.sum(jnp.where(ex[:, :, None] == eids, pstart, 0), axis=-1) + rank
    nvalid = tile_end[-1:].astype(I32)
    tidx = jnp.arange(n_tiles, dtype=I32)
    te = jnp.sum((tile_end[None, :] <= jnp.minimum(tidx, nvalid[0] - 1)[:, None]).astype(I32), axis=1)
    nonempty = ptiles > 0
    slot_e = (jnp.cumsum(nonempty.astype(I32)) - 1) & 1
    later = jnp.where(nonempty, eids, MOE_EXPERTS)
    nxt_incl = lax.cummin(later, axis=0, reverse=True)
    nxt_e = jnp.concatenate([nxt_incl[1:], jnp.full((1,), MOE_EXPERTS, I32)])
    nxt_e = jnp.where(nxt_e >= MOE_EXPERTS, -1, nxt_e)
    first = jnp.logical_and(tidx < nvalid[0],
                            jnp.logical_or(tidx == 0, te != jnp.concatenate([te[:1], te[:-1]]))).astype(I32)
    xs = _dispatch(xp, pos[:, 0], pos[:, 1], n_tiles * tm)
    y = _experts(xs, te, nvalid, first, nxt_e[te], slot_e[te], w_gate, w_up, w_down, layer, n_tiles)
    lo, hi = _combine(y, pos[:, 0], pos[:, 1], g1b, g2b)
    return _res_ln(x2, lo, hi, g, b)


def kernel(x, rel_bias, a_w_in, a_w_out, b_w_in, b_norm_g, b_norm_b, b_w_s, b_b_s, b_w_out, c_w_in, c_conv,
           c_a_log, c_dt_bias, c_norm_w, c_w_out, ln_g, ln_b, moe_w_coarse, moe_w_fine, moe_w_gate, moe_w_up,
           moe_w_down):
    bsz, seq, d = x.shape
    x2 = x.reshape(bsz * seq, d)
    for i in range(DEPTH):
        kind, j = i % 3, i // 3
        g1, b1 = ln_g[i, 0], ln_b[i, 0]
        if kind == 0:
            x2, xp = _mixer_a(x2, a_w_in[j], a_w_out[j], rel_bias, g1, b1, bsz, seq)
        elif kind == 1:
            x2, xp = _mixer_b(x2, b_w_in[j], b_norm_g[j], b_norm_b[j], b_w_s[j], b_b_s[j], b_w_out[j], g1, b1)
        else:
            x2, xp = _mixer_c(x2, c_w_in[j], c_conv[j], c_a_log[j], c_dt_bias[j], c_norm_w[j], c_w_out[j],
                              g1, b1, bsz, seq)
        x2 = _moe(x2, xp, moe_w_coarse[i], moe_w_fine[i], moe_w_gate, moe_w_up, moe_w_down, i,
                  ln_g[i, 1], ln_b[i, 1])
    return x2.reshape(bsz, seq, d)
```

```python
import functools
import math

import numpy as np
import jax
import jax.numpy as jnp
from jax import lax
from jax.experimental import pallas as pl
from jax.experimental.pallas import tpu as pltpu

F32 = jnp.float32
BF16 = jnp.bfloat16
U32 = jnp.uint32
I32 = jnp.int32

D_MODEL = 1024
DEPTH = 4
A_GROUPS = ((128, 1), (512, 4), (2048, 16))
A_HEADS = 16
A_HEAD_DIM = 64
A_BLOCK = 128
NUM_BUCKETS = 32
MAX_DISTANCE = 2048
B_CHUNK = 128
B_WIDTH = 2 * D_MODEL
B_GROUPS = 16
C_HEADS = 8
C_HEAD_DIM = 128
C_CONV = 4
C_CHUNK = 64
MOE_GROUPS = 8
MOE_PER_GROUP = 8
MOE_EXPERTS = 64
MOE_HIDDEN = 512
LN_EPS = 1e-5
RMS_EPS = 1e-6
ALPHA = (2 * DEPTH) ** 0.25

LANES = 128
NEG = -1e30
LOG2E = 1.4426950408889634
LN2 = 0.6931471805599453
VMEM_LIMIT = 56 * 1024 * 1024
MOE_TM = 512
ROW_UNROLL = 32
MOE_PARTS = 2
C_SUPER = 256
C_GROUP = 8
A_HP_GROUP = 2


def _cparams(n_axes, vmem=VMEM_LIMIT):
    return pltpu.CompilerParams(dimension_semantics=("arbitrary",) * n_axes, vmem_limit_bytes=vmem)


def _const_spec(shape):
    nd = len(shape)
    return pl.BlockSpec(shape, lambda *_: (0,) * nd, pipeline_mode=pl.Buffered(1))


def _ln_rows(y, g, b):
    mu = jnp.mean(y, axis=-1, keepdims=True)
    yc = y - mu
    var = jnp.mean(yc * yc, axis=-1, keepdims=True)
    return yc * lax.rsqrt(var + LN_EPS) * g + b


def _pack_pairs(y):
    w = y.shape[1] // 2
    lo = lax.bitcast_convert_type(y[:, :w].astype(BF16).astype(F32), U32)
    hi = lax.bitcast_convert_type(y[:, w:].astype(BF16).astype(F32), U32)
    return (lo >> 16) | (hi & jnp.uint32(0xFFFF0000))


def _unpack_lo(p):
    return lax.bitcast_convert_type(p << 16, F32)


def _unpack_hi(p):
    return lax.bitcast_convert_type(p & jnp.uint32(0xFFFF0000), F32)


def _split3(a):
    h = a.astype(BF16)
    r = a - h.astype(F32)
    m = r.astype(BF16)
    l = (r - m.astype(F32)).astype(BF16)
    return h, m, l


def _dot(a, b):
    return jnp.dot(a, b, preferred_element_type=F32)


def _dot_nt(a, b):
    return lax.dot_general(a, b, (((1,), (1,)), ((), ())), preferred_element_type=F32)


def _gelu_tanh(x):
    return 0.5 * x * (1.0 + jnp.tanh(0.7978845608028654 * (x + 0.044715 * (x * x * x))))


def _sigmoid(x):
    return 0.5 * jnp.tanh(0.5 * x) + 0.5


def _proj_body(x_ref, w_ref, o_ref):
    xb = x_ref[...].astype(BF16)
    cw = 1024
    for c in range(w_ref.shape[1] // cw):
        o_ref[:, c * cw:(c + 1) * cw] = _dot(xb, w_ref[:, c * cw:(c + 1) * cw]).astype(o_ref.dtype)


def _proj(x2, w, out_dtype, tm=1024):
    t, k = x2.shape
    n = w.shape[1]
    return pl.pallas_call(
        _proj_body,
        grid=(t // tm,),
        in_specs=[pl.BlockSpec((tm, k), lambda i: (i, 0)), _const_spec((k, n))],
        out_specs=pl.BlockSpec((tm, n), lambda i: (i, 0)),
        out_shape=jax.ShapeDtypeStruct((t, n), out_dtype),
        compiler_params=_cparams(1),
        name="proj",
    )(x2, w)


def _proj3_body(x_ref, wh_ref, wl_ref, o_ref):
    x = x_ref[...]
    xh = x.astype(BF16)
    xl = (x - xh.astype(F32)).astype(BF16)
    wh = wh_ref[...]
    o_ref[...] = _dot(xh, wh) + _dot(xl, wh) + _dot(xh, wl_ref[...])


def _proj3(x2, w, tm=1024):
    t, k = x2.shape
    n = w.shape[1]
    wh = w.astype(BF16)
    wl = (w - wh.astype(F32)).astype(BF16)
    return pl.pallas_call(
        _proj3_body,
        grid=(t // tm,),
        in_specs=[pl.BlockSpec((tm, k), lambda i: (i, 0)), _const_spec((k, n)), _const_spec((k, n))],
        out_specs=pl.BlockSpec((tm, n), lambda i: (i, 0)),
        out_shape=jax.ShapeDtypeStruct((t, n), F32),
        compiler_params=_cparams(1),
        name="proj3",
    )(x2, wh, wl)


def _mm_res_ln_body(a_ref, w_ref, x_ref, g_ref, b_ref, o_ref, p_ref):
    tm = a_ref.shape[0]
    halves = [slice(0, tm // 2), slice(tm // 2, tm)]
    ys = [_dot(a_ref[rs, :], w_ref[...]) for rs in halves]
    for rs, y in zip(halves, ys):
        xn = _ln_rows(ALPHA * x_ref[rs, :] + y, g_ref[...], b_ref[...])
        o_ref[rs, :] = xn
        p_ref[rs, :] = _pack_pairs(xn)


def _mm_res_ln(a, w, x2, g, b, tm=512):
    t, k = a.shape
    d = w.shape[1]
    return pl.pallas_call(
        _mm_res_ln_body,
        grid=(t // tm,),
        in_specs=[pl.BlockSpec((tm, k), lambda i: (i, 0)), _const_spec((k, d)),
                  pl.BlockSpec((tm, d), lambda i: (i, 0)), _const_spec((1, d)), _const_spec((1, d))],
        out_specs=[pl.BlockSpec((tm, d), lambda i: (i, 0)), pl.BlockSpec((tm, d // 2), lambda i: (i, 0))],
        out_shape=[jax.ShapeDtypeStruct((t, d), F32), jax.ShapeDtypeStruct((t, d // 2), U32)],
        compiler_params=_cparams(1),
        name="mm_res_ln",
    )(a, w, x2, g.reshape(1, d), b.reshape(1, d))


def _t5_bucket(dist):
    max_exact = NUM_BUCKETS // 2
    d = jnp.maximum(dist, 1).astype(F32)
    large = max_exact + (jnp.log(d / max_exact) / math.log(MAX_DISTANCE / max_exact)
                         * (NUM_BUCKETS - max_exact)).astype(I32)
    return jnp.where(dist < max_exact, dist, jnp.minimum(large, NUM_BUCKETS - 1))


def _attn_bias(rel_bias, window, dil):
    steps = window // dil
    qi = jnp.arange(A_BLOCK)[:, None]
    ki = jnp.arange(2 * A_BLOCK)[None, :]
    rel = qi + A_BLOCK - ki
    valid = (rel >= 0) & (rel <= steps)
    bucket = _t5_bucket(jnp.maximum(rel, 0) * dil)
    bias = jnp.zeros((A_HEADS, A_BLOCK, 2 * A_BLOCK), F32)
    for bkt in range(NUM_BUCKETS):
        bias = jnp.where((bucket == bkt)[None], rel_bias[bkt].astype(F32)[:, None, None], bias)
    return jnp.where(valid[None], bias * LOG2E, NEG)


def _attn_body(q_ref, k_ref, v_ref, kp_ref, vp_ref, bias_ref, o_ref, lse_ref, *, nres, nblk):
    li = pl.program_id(2)
    lane = lax.broadcasted_iota(I32, (1, LANES), 1)
    col2 = lax.broadcasted_iota(I32, (1, 2 * A_BLOCK), 1)
    first_pen = jnp.where(col2 < A_BLOCK, jnp.where(li == 0, NEG, 0.0).astype(F32), 0.0)
    head0 = lane < A_HEAD_DIM
    lse_ref[...] = jnp.zeros_like(lse_ref)

    def hp_stages(hp):
        c0 = pl.multiple_of(hp * LANES, LANES)
        cols = pl.ds(c0, LANES)
        blocks = [(r, j) for r in range(nres) for j in range(nblk)]
        bias2 = jnp.concatenate([bias_ref[2 * hp], bias_ref[2 * hp + 1]], axis=0)
        ones = jnp.ones((2 * A_BLOCK, LANES), BF16)
        vbs, s_, p_, m_ = {}, {}, {}, {}

        def scores(u):
            r, j = u
            rows = pl.ds(j * A_BLOCK, A_BLOCK)
            q = q_ref[r, rows, cols]
            if j == 0:
                kb = jnp.concatenate([kp_ref[r, :, cols], k_ref[r, rows, cols]], axis=0)
                vbs[u] = jnp.concatenate([vp_ref[r, :, cols], v_ref[r, rows, cols]], axis=0)
            else:
                band = pl.ds((j - 1) * A_BLOCK, 2 * A_BLOCK)
                kb = k_ref[r, band, cols]
                vbs[u] = v_ref[r, band, cols]
            zero = jnp.zeros_like(q)
            q2 = jnp.concatenate([jnp.where(head0, q, zero), jnp.where(head0, zero, q)], axis=0)
            s = _dot_nt(q2, kb) + bias2
            s_[u] = s + first_pen if j == 0 else s

        def softmax(u):
            m_[u] = jnp.max(s_[u], axis=-1, keepdims=True)
            p_[u] = jnp.exp2(s_[u] - m_[u]).astype(BF16)

        def values(u):
            r, j = u
            rows = pl.ds(j * A_BLOCK, A_BLOCK)
            pvl = _dot(p_[u], jnp.concatenate([vbs[u], ones], axis=1))
            den = pvl[:, LANES:]
            o = pvl[:, :LANES] / den
            lse = m_[u] * LN2 + jnp.log(den)
            o_ref[r, rows, cols] = jnp.where(head0, o[:A_BLOCK], o[A_BLOCK:]).astype(o_ref.dtype)
            cur = lse_ref[r, rows, :]
            cur = jnp.where(lane == 2 * hp, lse[:A_BLOCK], cur)
            cur = jnp.where(lane == 2 * hp + 1, lse[A_BLOCK:], cur)
            lse_ref[r, rows, :] = cur

        return blocks, scores, softmax, values

    def hp_pair(hh, carry):
        stages = [hp_stages(A_HP_GROUP * hh + i) for i in range(A_HP_GROUP)]
        for k in range(3):
            for blocks, *fns in stages:
                for u in blocks:
                    fns[k](u)
        return carry

    lax.fori_loop(0, A_HEADS // 2 // A_HP_GROUP, hp_pair, 0)


def _proj_perm_body(x_ref, w_ref, o_ref, xb_ref, y_ref, *, dil):
    tm, k = x_ref.shape
    n = tm // dil

    @pl.when(pl.program_id(1) == 0)
    def _():
        if dil == 1:
            xb_ref[...] = x_ref[...].astype(BF16)
        else:
            nc = k // LANES
            for c in range(nc):
                y_ref[c] = x_ref[:, c * LANES:(c + 1) * LANES]
            for r in range(dil):
                xb_ref[r * n:(r + 1) * n, :] = jnp.concatenate(
                    [y_ref[c, pl.ds(r, n, stride=dil), :] for c in range(nc)], axis=1).astype(BF16)

    cw = 1024
    for c in range(w_ref.shape[1] // cw):
        y = _dot(xb_ref[...], w_ref[:, c * cw:(c + 1) * cw])
        for r in range(dil):
            o_ref[r, :, c * cw:(c + 1) * cw] = y[r * n:(r + 1) * n].astype(o_ref.dtype)


def _proj_perm(x2, w, g, dil, bsz, seq, tm=1024, tn=3072):
    t, k = x2.shape
    hd = A_HEADS * A_HEAD_DIM
    tpb = seq // tm
    return pl.pallas_call(
        functools.partial(_proj_perm_body, dil=dil),
        grid=(t // tm, 3 * hd // tn),
        in_specs=[pl.BlockSpec((tm, k), lambda i, j: (i, 0)),
                  pl.BlockSpec((k, tn), lambda i, j: (0, g * (3 * hd // tn) + j))],
        out_specs=pl.BlockSpec((None, dil, tm // dil, tn), lambda i, j: (i // tpb, 0, i % tpb, j)),
        out_shape=jax.ShapeDtypeStruct((bsz, dil, seq // dil, 3 * hd), BF16),
        scratch_shapes=[pltpu.VMEM((tm, k), BF16), pltpu.VMEM((k // LANES, tm, LANES), F32)],
        compiler_params=_cparams(2),
        name=f"proj_d{dil}",
    )(x2, w)


def _dilated_group(qkv, dil, bias, bsz, seq):
    hd = A_HEADS * A_HEAD_DIM
    L = seq // dil
    rows_per_step = 512
    tl = min(rows_per_step, L)
    nblk = tl // A_BLOCK
    nres = rows_per_step // tl

    def blk(col):
        return pl.BlockSpec((None, nres, tl, hd), lambda b, r, li: (b, r, li, col))

    def prev(col):
        return pl.BlockSpec((None, nres, A_BLOCK, hd),
                            lambda b, r, li: (b, r, jnp.maximum(li * nblk - 1, 0), col))

    return pl.pallas_call(
        functools.partial(_attn_body, nres=nres, nblk=nblk),
        grid=(bsz, dil // nres, L // tl),
        in_specs=[blk(0), blk(1), blk(2), prev(1), prev(2), _const_spec((A_HEADS, A_BLOCK, 2 * A_BLOCK))],
        out_specs=[pl.BlockSpec((None, nres, tl, hd), lambda b, r, li: (b, r, li, 0)),
                   pl.BlockSpec((None, nres, tl, LANES), lambda b, r, li: (b, r, li, 0))],
        out_shape=[jax.ShapeDtypeStruct((bsz, dil, L, hd), BF16),
                   jax.ShapeDtypeStruct((bsz, dil, L, LANES), F32)],
        compiler_params=_cparams(3),
        name=f"dilated_attn_d{dil}",
    )(qkv, qkv, qkv, qkv, qkv, bias)


def _attn_out_body(o1_ref, o2_ref, o3_ref, l1_ref, l2_ref, l3_ref, e_ref, w_ref, x_ref, g_ref, b_ref,
                   xo_ref, p_ref, so_ref, sl2_ref, sl3_ref, *, dils):
    def natural(ref, scr, dil):
        if dil == 1:
            return ref[0].astype(F32)
        n = ref.shape[1]
        nc = ref.shape[2] // LANES
        for r in range(dil):
            blk = ref[r].astype(F32)
            for c in range(nc):
                scr[c, pl.ds(r, n, stride=dil), :] = blk[:, c * LANES:(c + 1) * LANES]
        return jnp.concatenate([scr[c] for c in range(nc)], axis=1) if nc > 1 else scr[0]

    l1 = natural(l1_ref, None, dils[0])
    l2 = natural(l2_ref, sl2_ref, dils[1])
    l3 = natural(l3_ref, sl3_ref, dils[2])
    m = jnp.maximum(jnp.maximum(l1, l2), l3)
    e1, e2, e3 = jnp.exp(l1 - m), jnp.exp(l2 - m), jnp.exp(l3 - m)
    inv = 1.0 / (e1 + e2 + e3)
    ex = e_ref[...]

    def expand(wt):
        h = wt.astype(BF16)
        lo = (wt - h.astype(F32)).astype(BF16)
        return _dot(h, ex) + _dot(lo, ex)

    comb = expand(e1 * inv) * natural(o1_ref, None, dils[0])
    comb = comb + expand(e2 * inv) * natural(o2_ref, so_ref, dils[1])
    comb = comb + expand(e3 * inv) * natural(o3_ref, so_ref, dils[2])
    y = _dot(comb.astype(BF16), w_ref[...])
    xn = _ln_rows(ALPHA * x_ref[...] + y, g_ref[...], b_ref[...])
    xo_ref[...] = xn
    p_ref[...] = _pack_pairs(xn)


def _attn_out(os, lses, dils, w_out, x2, g, b, seq, tm=512):
    t, d = x2.shape
    tpb = seq // tm
    expand = np.zeros((LANES, d), np.float32)
    for h in range(A_HEADS):
        expand[h, h * A_HEAD_DIM:(h + 1) * A_HEAD_DIM] = 1.0
    row = lambda n: pl.BlockSpec((tm, n), lambda i: (i, 0))
    res = lambda dil, n: pl.BlockSpec((None, dil, tm // dil, n), lambda i: (i // tpb, 0, i % tpb, 0))
    return pl.pallas_call(
        functools.partial(_attn_out_body, dils=dils),
        grid=(t // tm,),
        in_specs=[res(dils[0], d), res(dils[1], d), res(dils[2], d),
                  res(dils[0], LANES), res(dils[1], LANES), res(dils[2], LANES),
                  _const_spec((LANES, d)), _const_spec((d, d)), row(d), _const_spec((1, d)), _const_spec((1, d))],
        out_specs=[row(d), row(d // 2)],
        out_shape=[jax.ShapeDtypeStruct((t, d), F32), jax.ShapeDtypeStruct((t, d // 2), U32)],
        scratch_shapes=[pltpu.VMEM((d // LANES, tm, LANES), F32), pltpu.VMEM((1, tm, LANES), F32),
                        pltpu.VMEM((1, tm, LANES), F32)],
        compiler_params=_cparams(1),
        name="attn_out",
    )(*os, *lses, jnp.asarray(expand, BF16), w_out, x2, g.reshape(1, d), b.reshape(1, d))


def _mixer_a(x2, w_in, w_out, rel_bias, g, b, bsz, seq):
    hd = A_HEADS * A_HEAD_DIM
    scale = np.ones((9 * hd,), np.float32)
    for gi in range(len(A_GROUPS)):
        scale[3 * gi * hd:(3 * gi + 1) * hd] = A_HEAD_DIM ** -0.5 * LOG2E
    wb = (w_in * scale).astype(BF16)
    os, lses = [], []
    for gi, (window, dil) in enumerate(A_GROUPS):
        qkv = _proj_perm(x2, wb, gi, dil, bsz, seq)
        o, lse = _dilated_group(qkv, dil, _attn_bias(rel_bias, window, dil), bsz, seq)
        os.append(o)
        lses.append(lse)
    return _attn_out(os, lses, tuple(dl for _, dl in A_GROUPS), w_out.astype(BF16), x2, g, b, seq)


def _sgu_body(x_ref, wu_ref, wv_ref, ng_ref, nb_ref, wc_ref, bs_ref, a_ref, vb_ref, *, tm):
    xb = x_ref[...].astype(BF16)
    v = _gelu_tanh(_dot(xb, wv_ref[...]))
    vb_ref[...] = _ln_rows(v, ng_ref[...], nb_ref[...]).astype(BF16)
    gw = B_WIDTH // B_GROUPS
    ucols = 512
    for j in range(B_WIDTH // ucols):
        u = _gelu_tanh(_dot(xb, wu_ref[:, j * ucols:(j + 1) * ucols]))
        nchunk = tm // B_CHUNK
        for gg in range(ucols // gw):
            gi = j * (ucols // gw) + gg
            cols = slice(gi * gw, (gi + 1) * gw)
            vcat = jnp.concatenate([vb_ref[c * B_CHUNK:(c + 1) * B_CHUNK, cols] for c in range(nchunk)], axis=1)
            fcat = _dot(wc_ref[gi], vcat)
            for c in range(nchunk):
                rows = slice(c * B_CHUNK, (c + 1) * B_CHUNK)
                f = fcat[:, c * gw:(c + 1) * gw] + bs_ref[:, cols]
                a_ref[rows, cols] = (u[rows, gg * gw:(gg + 1) * gw] * f).astype(a_ref.dtype)


def _mixer_b(x2, w_in, norm_g, norm_b, w_s, b_s, w_out, g, b, tm=512):
    t, d = x2.shape
    wu = w_in[:, :B_WIDTH].astype(BF16)
    wv = w_in[:, B_WIDTH:].astype(BF16)
    wc = (w_s * jnp.tril(jnp.ones((B_CHUNK, B_CHUNK), w_s.dtype))).astype(BF16)
    bs_full = jnp.repeat(b_s.T, B_WIDTH // B_GROUPS, axis=1)
    a = pl.pallas_call(
        functools.partial(_sgu_body, tm=tm),
        grid=(t // tm,),
        in_specs=[pl.BlockSpec((tm, d), lambda i: (i, 0)), _const_spec((d, B_WIDTH)), _const_spec((d, B_WIDTH)),
                  _const_spec((1, B_WIDTH)), _const_spec((1, B_WIDTH)),
                  _const_spec((B_GROUPS, B_CHUNK, B_CHUNK)), _const_spec((B_CHUNK, B_WIDTH))],
        out_specs=pl.BlockSpec((tm, B_WIDTH), lambda i: (i, 0)),
        out_shape=jax.ShapeDtypeStruct((t, B_WIDTH), BF16),
        scratch_shapes=[pltpu.VMEM((tm, B_WIDTH), BF16)],
        compiler_params=_cparams(1),
        name="sgu",
    )(x2, wu, wv, norm_g.reshape(1, -1), norm_b.reshape(1, -1), wc, bs_full)
    return _mm_res_ln(a, w_out.astype(BF16), x2, g, b)


def _delta_body(qkvz_ref, gates_ref, cw_ref, alog_ref, dtb_ref, nw_ref, o_ref, ext_ref, state_ref, vnew_ref):
    ts = C_SUPER
    hd = C_HEADS * C_HEAD_DIM
    dk = C_HEAD_DIM
    nch = ts // C_CHUNK
    pr = 2 * C_CHUNK

    @pl.when(pl.program_id(1) == 0)
    def _():
        ext_ref[0:8, :] = jnp.zeros((8, 3 * hd), F32)
        state_ref[...] = jnp.zeros_like(state_ref)

    ext_ref[8:8 + ts, :] = qkvz_ref[:, 0:3 * hd].astype(F32)

    r_i = lax.broadcasted_iota(I32, (ts, ts), 0)
    c_i = lax.broadcasted_iota(I32, (ts, ts), 1)
    same = (r_i >> 6) == (c_i >> 6)
    u_cum = jnp.where(jnp.logical_and(same, r_i <= c_i), 1.0, 0.0).astype(BF16)
    u_tot = jnp.where(same, 1.0, 0.0).astype(BF16)
    r_p = lax.broadcasted_iota(I32, (pr, pr), 0)
    c_p = lax.broadcasted_iota(I32, (pr, pr), 1)
    same_p = (r_p >> 6) == (c_p >> 6)
    le = jnp.logical_and(same_p, c_p <= r_p)
    strict = jnp.logical_and(same_p, c_p < r_p)
    eye = jnp.where(r_p == c_p, 1.0, 0.0).astype(F32)

    gt = gates_ref[...].T
    zt = gt + dtb_ref[...]
    g_rows = -jnp.exp(alog_ref[...]) * (jnp.maximum(zt, 0.0) + jnp.log(1.0 + jnp.exp(-jnp.abs(zt))))
    gh, gm, gl = _split3(g_rows)
    gcum_rows = _dot(gh, u_cum) + _dot(gm, u_cum) + _dot(gl, u_cum)
    gtot_rows = _dot(gh, u_tot) + _dot(gm, u_tot) + _dot(gl, u_tot)
    gcum_cols = gcum_rows.T
    gtot_cols = gtot_rows.T
    beta_cols = _sigmoid(gates_ref[...])
    lane = lax.broadcasted_iota(I32, (1, LANES), 1)
    sub = lax.broadcasted_iota(I32, (LANES, 1), 0)
    csel = lax.broadcasted_iota(I32, (1, pr), 1) >> 6

    def conv_silu(c0):
        cols = pl.ds(pl.multiple_of(c0, dk), dk)
        full = ext_ref[:, cols]
        y = cw_ref[3:4, cols] * full[8:]
        for s in range(1, C_CONV):
            y = y + cw_ref[3 - s:4 - s, cols] * pltpu.roll(full, s, axis=0)[8:]
        return y * _sigmoid(y)

    def head_group(hg, carry):
        heads = [hg * C_GROUP + i for i in range(C_GROUP)]
        hv = []
        for h in heads:
            c0 = pl.multiple_of(h * dk, dk)
            pick_a = lane == (C_HEADS + h)
            gcol = jnp.sum(jnp.where(pick_a, gcum_cols, 0.0), axis=1, keepdims=True)
            glcol = jnp.sum(jnp.where(pick_a, gtot_cols, 0.0), axis=1, keepdims=True)
            bcol = jnp.sum(jnp.where(lane == h, beta_cols, 0.0), axis=1, keepdims=True)
            pick_r = sub == (C_HEADS + h)
            grow = jnp.sum(jnp.where(pick_r, gcum_rows, 0.0), axis=0, keepdims=True)
            glrow = jnp.sum(jnp.where(pick_r, gtot_rows, 0.0), axis=0, keepdims=True)
            q = conv_silu(c0)
            k = conv_silu(c0 + hd)
            v = conv_silu(c0 + 2 * hd)
            q = q * lax.rsqrt(jnp.sum(q * q, axis=-1, keepdims=True) + RMS_EPS) * (dk ** -0.5)
            k = k * lax.rsqrt(jnp.sum(k * k, axis=-1, keepdims=True) + RMS_EPS)
            eg = jnp.exp(gcol)
            kb = k * bcol
            hv.append(dict(
                c0=c0, gcol=gcol, grow=grow, glrow=glrow, kbf=k.astype(BF16), kbb=kb.astype(BF16),
                qbf=q.astype(BF16), rhs=jnp.concatenate([v * bcol, kb * eg], axis=1).astype(BF16),
                qe=q * eg, ktil_t=(k * jnp.exp(glcol - gcol)).T.astype(BF16)))

        inst = [(i, p) for i in range(C_GROUP) for p in range(ts // pr)]
        mpow, tinv, intra = {}, {}, {}
        for (i, p) in inst:
            d_ = hv[i]
            rp = slice(p * pr, (p + 1) * pr)
            decay = jnp.exp(jnp.where(le, d_["gcol"][rp] - d_["grow"][:, rp], NEG))
            lower = jnp.where(strict, _dot_nt(d_["kbb"][rp], d_["kbf"][rp]) * decay, 0.0)
            intra[i, p] = (_dot_nt(d_["qbf"][rp], d_["kbf"][rp]) * decay).astype(BF16)
            mpow[i, p] = -lower
            tinv[i, p] = eye - lower
        for _ in range(5):
            for key in inst:
                mb = mpow[key].astype(BF16)
                mpow[key] = _dot(mb, mb)
            for key in inst:
                tinv[key] = tinv[key] + _dot(tinv[key].astype(BF16), mpow[key].astype(BF16))
        wv_, kc = {}, {}
        for (i, p) in inst:
            wk = _dot(tinv[i, p].astype(BF16), hv[i]["rhs"][p * pr:(p + 1) * pr])
            wv_[i, p] = wk[:, :dk]
            kc[i, p] = wk[:, dk:]

        st = []
        for i, h in enumerate(heads):
            vnew_ref[i] = jnp.zeros(vnew_ref.shape[1:], vnew_ref.dtype)
            st.append(state_ref[h])
        outs = [[] for _ in heads]
        for j in range(nch):
            p, jj = j // 2, j % 2
            rows = slice(j * C_CHUNK, (j + 1) * C_CHUNK)
            lrows = slice(jj * C_CHUNK, (jj + 1) * C_CHUNK)
            rp = slice(p * pr, (p + 1) * pr)
            a1 = [_dot(jnp.concatenate([kc[i, p][lrows], hv[i]["qe"][rows]], axis=0).astype(BF16),
                       st[i].astype(BF16)) for i in range(C_GROUP)]
            for i in range(C_GROUP):
                vnew_ref[i, rows, :] = (wv_[i, p][lrows] - a1[i][:C_CHUNK]).astype(BF16)
            for i in range(C_GROUP):
                vn_pair = vnew_ref[i, rp, :]
                outs[i].append(a1[i][C_CHUNK:] + _dot(intra[i, p][lrows, :], vn_pair))
                kt_j = jnp.where(csel == jj, hv[i]["ktil_t"][:, rp], jnp.zeros((), BF16))
                dg = jnp.exp(jnp.sum(jnp.where(csel == jj, hv[i]["glrow"][:, rp], 0.0), axis=1, keepdims=True)
                             * (1.0 / C_CHUNK))
                st[i] = st[i] * dg + _dot(kt_j, vn_pair)
        for i, h in enumerate(heads):
            state_ref[h] = st[i]
            o = jnp.concatenate(outs[i], axis=0)
            o = o * lax.rsqrt(jnp.mean(o * o, axis=-1, keepdims=True) + RMS_EPS) * nw_ref[...]
            c0 = hv[i]["c0"]
            z = qkvz_ref[:, pl.ds(pl.multiple_of(c0 + 3 * hd, dk), dk)].astype(F32)
            o_ref[:, pl.ds(c0, dk)] = (o * (z * _sigmoid(z))).astype(o_ref.dtype)
        return carry

    lax.fori_loop(0, C_HEADS // C_GROUP, head_group, 0)
    ext_ref[0:8, :] = ext_ref[ts:ts + 8, :]


def _mixer_c(x2, w_in, conv_w, a_log, dt_bias, norm_w, w_out, g, b, bsz, seq):
    t, d = x2.shape
    hd = C_HEADS * C_HEAD_DIM
    qkvz = _proj(x2, w_in[:, :4 * hd].astype(BF16), BF16)
    wg = jnp.zeros((d, LANES), F32).at[:, :2 * C_HEADS].set(w_in[:, 4 * hd:])
    gates = _proj3(x2, wg)
    col = lambda v: jnp.zeros((LANES, 1), F32).at[C_HEADS:2 * C_HEADS, 0].set(v.astype(F32))
    nsteps = seq // C_SUPER
    o = pl.pallas_call(
        _delta_body,
        grid=(bsz, nsteps),
        in_specs=[pl.BlockSpec((C_SUPER, 4 * hd), lambda bi, i: (bi * nsteps + i, 0)),
                  pl.BlockSpec((C_SUPER, LANES), lambda bi, i: (bi * nsteps + i, 0)),
                  _const_spec((C_CONV, 3 * hd)), _const_spec((LANES, 1)), _const_spec((LANES, 1)),
                  _const_spec((1, C_HEAD_DIM))],
        out_specs=pl.BlockSpec((C_SUPER, hd), lambda bi, i: (bi * nsteps + i, 0)),
        out_shape=jax.ShapeDtypeStruct((t, hd), BF16),
        scratch_shapes=[pltpu.VMEM((C_SUPER + 8, 3 * hd), F32),
                        pltpu.VMEM((C_HEADS, C_HEAD_DIM, C_HEAD_DIM), F32),
                        pltpu.VMEM((C_GROUP, C_SUPER, C_HEAD_DIM), BF16)],
        compiler_params=_cparams(2),
        name="deltanet",
    )(qkvz, gates, conv_w.astype(F32), col(a_log), col(dt_bias), norm_w.reshape(1, -1).astype(F32))
    return _mm_res_ln(o, w_out.astype(BF16), x2, g, b)


def _route_rows(x, wh_ref, wl_ref, o_ref, g1_ref, g2_ref, cnt_ref, run_ref, tri_ref):
    tm = x.shape[0]

    @pl.when(pl.program_id(0) == 0)
    def _():
        run_ref[...] = jnp.zeros_like(run_ref)
        r_i = lax.broadcasted_iota(I32, (tm, tm), 0)
        c_i = lax.broadcasted_iota(I32, (tm, tm), 1)
        tri_ref[...] = jnp.where(c_i < r_i, 1.0, 0.0).astype(BF16)

    xh = x.astype(BF16)
    xl = (x - xh.astype(F32)).astype(BF16)
    wh = wh_ref[...]
    logits = _dot(xh, wh) + _dot(xl, wh) + _dot(xh, wl_ref[...])
    lane = lax.broadcasted_iota(I32, (1, LANES), 1)
    lane_f = lane.astype(F32)

    def top1(vals):
        m = jnp.max(vals, axis=-1, keepdims=True)
        idx = jnp.min(jnp.where(vals == m, lane_f, 1e9), axis=-1, keepdims=True)
        return m, idx.astype(I32)

    lc = jnp.where(lane < MOE_GROUPS, logits, NEG)
    mc, grp = top1(lc)
    p_grp = 1.0 / jnp.sum(jnp.exp(lc - mc), axis=-1, keepdims=True)
    lo = MOE_GROUPS + MOE_PER_GROUP * grp
    lf = jnp.where(jnp.logical_and(lane >= lo, lane < lo + MOE_PER_GROUP), logits, NEG)
    m1, i1 = top1(lf)
    m2, i2 = top1(jnp.where(lane == i1, NEG, lf))
    e21 = jnp.exp(m2 - m1)
    g1 = p_grp / (1.0 + e21)
    g2 = p_grp * e21 / (1.0 + e21)
    ex1 = i1 - MOE_GROUPS
    ex2 = i2 - MOE_GROUPS

    oh1 = jnp.where(lane == ex1, 1.0, 0.0).astype(F32)
    oh2 = jnp.where(lane == ex2, 1.0, 0.0).astype(F32)
    ohs = oh1 + oh2
    before = _dot(tri_ref[...], ohs.astype(BF16)) + run_ref[...]
    rank1 = jnp.sum(oh1 * before, axis=-1, keepdims=True)
    rank2 = jnp.sum(oh2 * before, axis=-1, keepdims=True)
    run_ref[...] = run_ref[...] + jnp.sum(ohs, axis=0, keepdims=True)

    out = jnp.where(lane == 0, ex1.astype(F32), 0.0)
    out = jnp.where(lane == 1, ex2.astype(F32), out)
    out = jnp.where(lane == 2, g1, out)
    out = jnp.where(lane == 3, g2, out)
    out = jnp.where(lane == 4, rank1, out)
    out = jnp.where(lane == 5, rank2, out)
    o_ref[...] = out
    g1_ref[...] = jnp.broadcast_to(g1, g1_ref.shape)
    g2_ref[...] = jnp.broadcast_to(g2, g2_ref.shape)
    cnt_ref[...] = jnp.broadcast_to(run_ref[...], cnt_ref.shape)


def _route_weights(w_coarse, w_fine):
    d = w_coarse.shape[0]
    wr = jnp.zeros((d, LANES), F32)
    wr = wr.at[:, :MOE_GROUPS].set(w_coarse).at[:, MOE_GROUPS:MOE_GROUPS + MOE_EXPERTS].set(
        w_fine.reshape(d, MOE_EXPERTS))
    wh = wr.astype(BF16)
    wl = (wr - wh.astype(F32)).astype(BF16)
    return wh, wl


def _route_body(x_ref, wh_ref, wl_ref, o_ref, g1_ref, g2_ref, cnt_ref, run_ref, tri_ref):
    _route_rows(x_ref[...], wh_ref, wl_ref, o_ref, g1_ref, g2_ref, cnt_ref, run_ref, tri_ref)


def _route(x2, w_coarse, w_fine, tm=512):
    t, d = x2.shape
    wh, wl = _route_weights(w_coarse, w_fine)
    row = pl.BlockSpec((tm, LANES), lambda i: (i, 0))
    return pl.pallas_call(
        _route_body,
        grid=(t // tm,),
        in_specs=[pl.BlockSpec((tm, d), lambda i: (i, 0)), _const_spec((d, LANES)), _const_spec((d, LANES))],
        out_specs=[row, row, row, pl.BlockSpec((8, LANES), lambda i: (0, 0))],
        out_shape=[jax.ShapeDtypeStruct((t, LANES), F32), jax.ShapeDtypeStruct((t, LANES), F32),
                   jax.ShapeDtypeStruct((t, LANES), F32), jax.ShapeDtypeStruct((8, LANES), F32)],
        scratch_shapes=[pltpu.VMEM((1, LANES), F32), pltpu.VMEM((tm, tm), BF16)],
        compiler_params=_cparams(1),
        name="moe_route",
    )(x2, wh, wl)


def _dispatch_body(p1_ref, p2_ref, x_ref, o_hbm, slab_ref, sem, *, tm):
    s = pl.program_id(0)
    i = pl.program_id(1)

    @pl.when(jnp.logical_and(s == 0, i == 0))
    def _():
        slab_ref[...] = jnp.zeros_like(slab_ref)

    def row(r, carry):
        v = x_ref[pl.ds(r, 1), :]
        slab_ref[pl.ds(p1_ref[0, r], 1), :] = v
        slab_ref[pl.ds(p2_ref[0, r], 1), :] = v
        return carry
    lax.fori_loop(0, tm, row, 0, unroll=ROW_UNROLL)

    @pl.when(i == pl.num_programs(1) - 1)
    def _():
        cp = pltpu.make_async_copy(slab_ref, o_hbm.at[s], sem)
        cp.start()
        cp.wait()


def _dispatch(xp, pos1, pos2, s_pad, tm=2048):
    t, w = xp.shape
    nslab = w // LANES
    nt = t // tm
    return pl.pallas_call(
        functools.partial(_dispatch_body, tm=tm),
        grid=(nslab, nt),
        in_specs=[pl.BlockSpec((None, 1, tm), lambda s, i: (i, 0, 0), memory_space=pltpu.SMEM),
                  pl.BlockSpec((None, 1, tm), lambda s, i: (i, 0, 0), memory_space=pltpu.SMEM),
                  pl.BlockSpec((tm, LANES), lambda s, i: (i, s))],
        out_specs=pl.BlockSpec(memory_space=pl.ANY),
        out_shape=jax.ShapeDtypeStruct((nslab, s_pad, LANES), U32),
        scratch_shapes=[pltpu.VMEM((s_pad, LANES), U32), pltpu.SemaphoreType.DMA(())],
        compiler_params=_cparams(2),
        name="moe_dispatch",
    )(pos1.reshape(nt, 1, tm), pos2.reshape(nt, 1, tm), xp)


def _expert_body(te_ref, nv_ref, first_ref, nxt_ref, slot_ref, xs_ref, wg_hbm, wu_hbm, wd_hbm, y_ref,
                 wgb_ref, wub_ref, wdb_ref, wg_buf, wu_buf, wd_buf, sem, *, layer):
    i = pl.program_id(0)
    half = D_MODEL // 2
    nslab = half // LANES

    def copies(e, sl):
        return (pltpu.make_async_copy(wg_hbm.at[layer, e], wg_buf.at[sl], sem.at[sl, 0]),
                pltpu.make_async_copy(wu_hbm.at[layer, e], wu_buf.at[sl], sem.at[sl, 1]),
                pltpu.make_async_copy(wd_hbm.at[layer, e], wd_buf.at[sl], sem.at[sl, 2]))

    @pl.when(i < nv_ref[0])
    def _():
        @pl.when(first_ref[i] == 1)
        def _():
            sl = slot_ref[i]

            @pl.when(i == 0)
            def _():
                for cp in copies(te_ref[0], sl):
                    cp.start()

            for cp in copies(te_ref[i], sl):
                cp.wait()

            @pl.when(nxt_ref[i] >= 0)
            def _():
                for cp in copies(nxt_ref[i], 1 - sl):
                    cp.start()

            wgb_ref[...] = wg_buf[sl].astype(BF16)
            wub_ref[...] = wu_buf[sl].astype(BF16)
            wdb_ref[...] = wd_buf[sl].astype(BF16)

        tm = xs_ref.shape[1]
        parts = [slice(a * (tm // MOE_PARTS), (a + 1) * (tm // MOE_PARTS)) for a in range(MOE_PARTS)]
        xlo = [jnp.concatenate([_unpack_lo(xs_ref[s, rs, :]) for s in range(nslab)], axis=1).astype(BF16)
               for rs in parts]
        xhi = [jnp.concatenate([_unpack_hi(xs_ref[s, rs, :]) for s in range(nslab)], axis=1).astype(BF16)
               for rs in parts]
        hg = [_dot(xlo[a], wgb_ref[0:half, :]) + _dot(xhi[a], wgb_ref[half:, :]) for a in range(MOE_PARTS)]
        hu = [_dot(xlo[a], wub_ref[0:half, :]) + _dot(xhi[a], wub_ref[half:, :]) for a in range(MOE_PARTS)]
        h = [(hg[a] * _sigmoid(hg[a]) * hu[a]).astype(BF16) for a in range(MOE_PARTS)]
        yd = [_dot(h[a], wdb_ref[...]) for a in range(MOE_PARTS)]
        for a, rs in enumerate(parts):
            packed = _pack_pairs(yd[a])
            for s in range(nslab):
                y_ref[s, rs, :] = packed[:, s * LANES:(s + 1) * LANES]

    @pl.when(i >= nv_ref[0])
    def _():
        y_ref[...] = jnp.zeros_like(y_ref)


def _experts(xs, tile_expert, nvalid, first, nxt, slot, w_gate, w_up, w_down, layer, n_tiles):
    tm = MOE_TM
    d, hdn = D_MODEL, MOE_HIDDEN
    nslab = d // 2 // LANES
    any_spec = pl.BlockSpec(memory_space=pl.ANY)
    return pl.pallas_call(
        functools.partial(_expert_body, layer=layer),
        grid_spec=pltpu.PrefetchScalarGridSpec(
            num_scalar_prefetch=5,
            grid=(n_tiles,),
            in_specs=[pl.BlockSpec((nslab, tm, LANES), lambda i, te, nv, *_: (0, jnp.minimum(i, nv[0] - 1), 0)),
                      any_spec, any_spec, any_spec],
            out_specs=pl.BlockSpec((nslab, tm, LANES), lambda i, *_: (0, i, 0)),
            scratch_shapes=[pltpu.VMEM((d, hdn), BF16), pltpu.VMEM((d, hdn), BF16), pltpu.VMEM((hdn, d), BF16),
                            pltpu.VMEM((2, d, hdn), F32), pltpu.VMEM((2, d, hdn), F32),
                            pltpu.VMEM((2, hdn, d), F32), pltpu.SemaphoreType.DMA((2, 3))],
        ),
        out_shape=jax.ShapeDtypeStruct((nslab, n_tiles * tm, LANES), U32),
        compiler_params=_cparams(1),
        name="moe_experts",
    )(tile_expert, nvalid, first, nxt, slot, xs, w_gate, w_up, w_down)


def _combine_body(p1_ref, p2_ref, y_ref, g1_ref, g2_ref, lo_ref, hi_ref, b1_ref, b2_ref, *, tm):
    def row(r, carry):
        b1_ref[pl.ds(r, 1), :] = y_ref[pl.ds(p1_ref[0, r], 1), :]
        b2_ref[pl.ds(r, 1), :] = y_ref[pl.ds(p2_ref[0, r], 1), :]
        return carry
    lax.fori_loop(0, tm, row, 0, unroll=ROW_UNROLL)
    g1 = g1_ref[...]
    g2 = g2_ref[...]
    y1 = b1_ref[...]
    y2 = b2_ref[...]
    lo_ref[...] = g1 * _unpack_lo(y1) + g2 * _unpack_lo(y2)
    hi_ref[...] = g1 * _unpack_hi(y1) + g2 * _unpack_hi(y2)


def _combine(y, pos1, pos2, g1b, g2b, tm=2048):
    nslab, s_pad, _ = y.shape
    t = pos1.shape[0]
    nt = t // tm
    return pl.pallas_call(
        functools.partial(_combine_body, tm=tm),
        grid=(nslab, nt),
        in_specs=[pl.BlockSpec((None, 1, tm), lambda s, i: (i, 0, 0), memory_space=pltpu.SMEM),
                  pl.BlockSpec((None, 1, tm), lambda s, i: (i, 0, 0), memory_space=pltpu.SMEM),
                  pl.BlockSpec((None, s_pad, LANES), lambda s, i: (s, 0, 0), pipeline_mode=pl.Buffered(1)),
                  pl.BlockSpec((tm, LANES), lambda s, i: (i, 0)), pl.BlockSpec((tm, LANES), lambda s, i: (i, 0))],
        out_specs=[pl.BlockSpec((tm, LANES), lambda s, i: (i, s)), pl.BlockSpec((tm, LANES), lambda s, i: (i, s))],
        out_shape=[jax.ShapeDtypeStruct((t, nslab * LANES), F32), jax.ShapeDtypeStruct((t, nslab * LANES), F32)],
        scratch_shapes=[pltpu.VMEM((tm, LANES), U32), pltpu.VMEM((tm, LANES), U32)],
        compiler_params=_cparams(2),
        name="moe_combine",
    )(pos1.reshape(nt, 1, tm), pos2.reshape(nt, 1, tm), y, g1b, g2b)


def _res_ln_body(x_ref, lo_ref, hi_ref, g_ref, b_ref, o_ref):
    h = jnp.concatenate([lo_ref[...], hi_ref[...]], axis=1)
    o_ref[...] = _ln_rows(ALPHA * x_ref[...] + h, g_ref[...], b_ref[...])


def _res_ln(x2, lo, hi, g, b, tm=1024):
    t, d = x2.shape
    return pl.pallas_call(
        _res_ln_body,
        grid=(t // tm,),
        in_specs=[pl.BlockSpec((tm, d), lambda i: (i, 0)), pl.BlockSpec((tm, d // 2), lambda i: (i, 0)),
                  pl.BlockSpec((tm, d // 2), lambda i: (i, 0)), _const_spec((1, d)), _const_spec((1, d))],
        out_specs=pl.BlockSpec((tm, d), lambda i: (i, 0)),
        out_shape=jax.ShapeDtypeStruct((t, d), F32),
        compiler_params=_cparams(1),
        name="res_ln",
    )(x2, lo, hi, g.reshape(1, d), b.reshape(1, d))


def _moe(x2, xp, w_coarse, w_fine, w_gate, w_up, w_down, layer, g, b):
    t = x2.shape[0]
    tm = MOE_TM
    n_tiles = (2 * t) // tm + MOE_EXPERTS
    route, g1b, g2b, cnt = _route(x2, w_coarse, w_fine)
    ex = route[:, 0:2].astype(I32)
    rank = route[:, 4:6].astype(I32)
    counts = cnt[0, :MOE_EXPERTS].astype(I32)
    ptiles = (counts + tm - 1) // tm
    tile_end = jnp.cumsum(ptiles)
    pstart = (tile_end - ptiles) * tm
    eids = jnp.arange(MOE_EXPERTS, dtype=I32)
    pos = jnp.sum(jnp.where(ex[:, :, None] == eids, pstart, 0), axis=-1) + rank
    nvalid = tile_end[-1:].astype(I32)
    tidx = jnp.arange(n_tiles, dtype=I32)
    te = jnp.sum((tile_end[None, :] <= jnp.minimum(tidx, nvalid[0] - 1)[:, None]).astype(I32), axis=1)
    nonempty = ptiles > 0
    slot_e = (jnp.cumsum(nonempty.astype(I32)) - 1) & 1
    later = jnp.where(nonempty, eids, MOE_EXPERTS)
    nxt_incl = lax.cummin(later, axis=0, reverse=True)
    nxt_e = jnp.concatenate([nxt_incl[1:], jnp.full((1,), MOE_EXPERTS, I32)])
    nxt_e = jnp.where(nxt_e >= MOE_EXPERTS, -1, nxt_e)
    first = jnp.logical_and(tidx < nvalid[0],
                            jnp.logical_or(tidx == 0, te != jnp.concatenate([te[:1], te[:-1]]))).astype(I32)
    xs = _dispatch(xp, pos[:, 0], pos[:, 1], n_tiles * tm)
    y = _experts(xs, te, nvalid, first, nxt_e[te], slot_e[te], w_gate, w_up, w_down, layer, n_tiles)
    lo, hi = _combine(y, pos[:, 0], pos[:, 1], g1b, g2b)
    return _res_ln(x2, lo, hi, g, b)


def kernel(x, rel_bias, a_w_in, a_w_out, b_w_in, b_norm_g, b_norm_b, b_w_s, b_b_s, b_w_out, c_w_in, c_conv,
           c_a_log, c_dt_bias, c_norm_w, c_w_out, ln_g, ln_b, moe_w_coarse, moe_w_fine, moe_w_gate, moe_w_up,
           moe_w_down):
    bsz, seq, d = x.shape
    x2 = x.reshape(bsz * seq, d)
    for i in range(DEPTH):
        kind, j = i % 3, i // 3
        g1, b1 = ln_g[i, 0], ln_b[i, 0]
        if kind == 0:
            x2, xp = _mixer_a(x2, a_w_in[j], a_w_out[j], rel_bias, g1, b1, bsz, seq)
        elif kind == 1:
            x2, xp = _mixer_b(x2, b_w_in[j], b_norm_g[j], b_norm_b[j], b_w_s[j], b_b_s[j], b_w_out[j], g1, b1)
        else:
            x2, xp = _mixer_c(x2, c_w_in[j], c_conv[j], c_a_log[j], c_dt_bias[j], c_norm_w[j], c_w_out[j],
                              g1, b1, bsz, seq)
        x2 = _moe(x2, xp, moe_w_coarse[i], moe_w_fine[i], moe_w_gate, moe_w_up, moe_w_down, i,
                  ln_g[i, 1], ln_b[i, 1])
    return x2.reshape(bsz, seq, d)
```
